```python
import math
import jax, jax.numpy as jnp
from jax import lax
import numpy as np

D_MODEL = 4096
BATCH = 2
SEQ = 4096
DEPTH = 2
DEC_BATCH = 16
DEC_SEQ = 16
PAST_LEN = 4096

CHUNK = 64
QBLK = 128
EPS = 1e-6
ROPE_THETA = 10000.0
GLA_HEADS = 4
GLA_DK = 256
GLA_DV = 512
GLA_RANK = 16
GLA_TAU = 16.0
DSA_HEADS = 16
DSA_KV_HEADS = 4
DSA_HD = 128
IDX_HEADS = 16
IDX_HD = 128
IDX_TOPK = 256
DIFF_HEADS = 16
DIFF_HD = 128
D_FF = 11008
CONV_W = 3
N_AB = (DEPTH + 1) // 2
N_C = DEPTH // 2

AB_SIZES = (GLA_HEADS * GLA_DK, GLA_HEADS * GLA_DK, GLA_HEADS * GLA_DV, GLA_HEADS * GLA_DV, GLA_RANK,
            DSA_HEADS * DSA_HD, DSA_KV_HEADS * DSA_HD, DSA_KV_HEADS * DSA_HD,
            IDX_HEADS * IDX_HD, IDX_HD, IDX_HEADS)
AB_IN = sum(AB_SIZES)
AB_OUT = GLA_HEADS * GLA_DV + DSA_HEADS * DSA_HD
C_IN = 3 * DIFF_HEADS * 2 * DIFF_HD
C_OUT = DIFF_HEADS * 2 * DIFF_HD

kernel_name = 'hybrid_gla_dsa_diffattn_convffn_stream_step'


def rms_norm(x, g):
    xf = x.astype(jnp.float32)
    y = xf * lax.rsqrt(jnp.mean(xf * xf, axis=-1, keepdims=True) + EPS)
    return (y * g.astype(jnp.float32)).astype(x.dtype)


def rope(x, pos):
    d = x.shape[-1]
    half = d // 2
    inv = 1.0 / (ROPE_THETA ** (jnp.arange(half, dtype=jnp.float32) * (2.0 / d)))
    ang = pos.astype(jnp.float32)[:, None] * inv[None, :]
    shape = (1, pos.shape[0]) + (1,) * (x.ndim - 3) + (half,)
    cos = jnp.cos(ang).reshape(shape)
    sin = jnp.sin(ang).reshape(shape)
    xf = x.astype(jnp.float32)
    x1, x2 = xf[..., :half], xf[..., half:]
    return jnp.concatenate([x1 * cos - x2 * sin, x2 * cos + x1 * sin], axis=-1).astype(x.dtype)


def chunk_visible(q_pos, k_pos):
    return (k_pos[None, :] // CHUNK) <= (q_pos[:, None] // CHUNK)


def over_query_blocks(fn, qs, q_pos):
    T = q_pos.shape[0]
    if T <= QBLK:
        return fn(qs, q_pos)
    n = T // QBLK

    def to_blocks(a):
        return jnp.moveaxis(a.reshape((a.shape[0], n, QBLK) + a.shape[2:]), 1, 0)

    out = lax.map(lambda args: fn(args[0], args[1]),
                  (tuple(to_blocks(a) for a in qs), q_pos.reshape(n, QBLK)))
    out = jnp.moveaxis(out, 0, 1)
    return out.reshape((out.shape[0], T) + out.shape[3:])


def gla_chunk(state, q, k, v, loga):
    c = q.shape[1]
    b = jnp.cumsum(loga, axis=1)
    o_inter = jnp.einsum('bthd,bhde->bthe', q * jnp.exp(b), state)
    causal = jnp.tril(jnp.ones((c, c), dtype=bool))
    rel = jnp.where(causal[None, :, :, None, None], b[:, :, None] - b[:, None, :], -jnp.inf)
    att = jnp.einsum('bthd,bshd,btshd->bhts', q, k, jnp.exp(rel))
    o_intra = jnp.einsum('bhts,bshe->bthe', att, v)
    b_last = b[:, -1]
    new_state = jnp.exp(b_last)[..., None] * state + jnp.einsum(
        'bshd,bshe->bhde', k * jnp.exp(b_last[:, None] - b), v)
    return new_state, o_inter + o_intra


def gla_run(state, q, k, v, loga):
    T = q.shape[1]
    if T <= CHUNK:
        return gla_chunk(state, q, k, v, loga)
    n = T // CHUNK

    def to_chunks(a):
        return jnp.moveaxis(a.reshape((a.shape[0], n, CHUNK) + a.shape[2:]), 1, 0)

    state, o = lax.scan(lambda s, xs: gla_chunk(s, *xs), state,
                        tuple(to_chunks(a) for a in (q, k, v, loga)))
    o = jnp.moveaxis(o, 0, 1)
    return state, o.reshape((o.shape[0], T) + o.shape[3:])


def dsa_attend(q, iq, iw, q_pos, k_all, v_all, ik_all, k_pos, topk):
    B, Tq = q.shape[:2]
    isc = jax.nn.relu(jnp.einsum('bqhd,bsd->bqhs', iq, ik_all).astype(jnp.float32))
    isc = jnp.einsum('bqh,bqhs->bqs', iw.astype(jnp.float32), isc)
    isc = jnp.where(chunk_visible(q_pos, k_pos)[None], isc, -jnp.inf)
    top_val, top_idx = lax.top_k(isc, topk)
    valid = top_val > -jnp.inf
    kg = jax.vmap(lambda kb, ib: kb[ib])(k_all, top_idx)
    vg = jax.vmap(lambda vb, ib: vb[ib])(v_all, top_idx)
    qg = q.reshape(B, Tq, DSA_KV_HEADS, DSA_HEADS // DSA_KV_HEADS, DSA_HD)
    s = jnp.einsum('bqhgd,bqjhd->bqhgj', qg, kg).astype(jnp.float32) * (DSA_HD ** -0.5)
    s = jnp.where(valid[:, :, None, None, :], s, -jnp.inf)
    p = jax.nn.softmax(s, axis=-1).astype(v_all.dtype)
    o = jnp.einsum('bqhgj,bqjhd->bqhgd', p, vg)
    return o.reshape(B, Tq, DSA_HEADS * DSA_HD)


def diff_attend(q, q_pos, k_all, v_all, k_pos, lam):
    s = jnp.einsum('bqhcd,bshcd->bhcqs', q, k_all).astype(jnp.float32) * (DIFF_HD ** -0.5)
    s = jnp.where(chunk_visible(q_pos, k_pos)[None, None, None], s, -jnp.inf)
    p = jax.nn.softmax(s, axis=-1)
    a = (p[:, :, 0] - lam * p[:, :, 1]).astype(v_all.dtype)
    return jnp.einsum('bhqs,bshe->bqhe', a, v_all)


def mixer_ab(h, gla_state, k_past, v_past, ik_past, w_in, w_a2, b_a, gla_g, w_out):
    f32 = jnp.float32
    B, T, _ = h.shape
    past = 0 if k_past is None else k_past.shape[1]
    k_pos = jnp.arange(past + T, dtype=jnp.int32)
    pos = k_pos[past:]
    gq, gk, gv, gg, ga, dq, dk, dv, iq, ik, iw = jnp.split(
        h @ w_in, np.cumsum(AB_SIZES)[:-1].tolist(), axis=-1)
    gq = gq.reshape(B, T, GLA_HEADS, GLA_DK).astype(f32) * (GLA_DK ** -0.5)
    gk = gk.reshape(B, T, GLA_HEADS, GLA_DK).astype(f32)
    gv = gv.reshape(B, T, GLA_HEADS, GLA_DV).astype(f32)
    loga = jax.nn.log_sigmoid((ga @ w_a2 + b_a).astype(f32)).reshape(B, T, GLA_HEADS, GLA_DK) / GLA_TAU
    new_state, go = gla_run(gla_state.astype(f32), gq, gk, gv, loga)
    go = rms_norm(go.astype(h.dtype), gla_g) * jax.nn.silu(gg.reshape(B, T, GLA_HEADS, GLA_DV))
    dq = rope(dq.reshape(B, T, DSA_HEADS, DSA_HD), pos)
    dk = rope(dk.reshape(B, T, DSA_KV_HEADS, DSA_HD), pos)
    dv = dv.reshape(B, T, DSA_KV_HEADS, DSA_HD)
    iq = rope(iq.reshape(B, T, IDX_HEADS, IDX_HD), pos)
    ik = rope(ik, pos)
    if k_past is None:
        k_all, v_all, ik_all = dk, dv, ik
    else:
        k_all = jnp.concatenate([k_past.astype(dk.dtype), dk], axis=1)
        v_all = jnp.concatenate([v_past.astype(dv.dtype), dv], axis=1)
        ik_all = jnp.concatenate([ik_past.astype(ik.dtype), ik], axis=1)
    topk = min(IDX_TOPK, (past + T) // 4)
    do = over_query_blocks(
        lambda qs, qp: dsa_attend(qs[0], qs[1], qs[2], qp, k_all, v_all, ik_all, k_pos, topk),
        (dq, iq, iw), pos)
    y = jnp.concatenate([go.reshape(B, T, GLA_HEADS * GLA_DV), do], axis=-1) @ w_out
    return y, new_state.astype(h.dtype), dk, dv, ik


def mixer_c(h, k_past, v_past, w_in, lq1, lk1, lq2, lk2, g, w_out, lam_init):
    f32 = jnp.float32
    B, T, _ = h.shape
    past = 0 if k_past is None else k_past.shape[1]
    k_pos = jnp.arange(past + T, dtype=jnp.int32)
    pos = k_pos[past:]
    q, k, v = jnp.split(h @ w_in, 3, axis=-1)
    q = rope(q.reshape(B, T, DIFF_HEADS, 2, DIFF_HD), pos)
    k = rope(k.reshape(B, T, DIFF_HEADS, 2, DIFF_HD), pos)
    v = v.reshape(B, T, DIFF_HEADS, 2 * DIFF_HD)
    lam = (jnp.exp(jnp.sum(lq1.astype(f32) * lk1.astype(f32)))
           - jnp.exp(jnp.sum(lq2.astype(f32) * lk2.astype(f32))) + lam_init)
    if k_past is None:
        k_all, v_all = k, v
    else:
        k_all = jnp.concatenate([k_past.astype(k.dtype), k], axis=1)
        v_all = jnp.concatenate([v_past.astype(v.dtype), v], axis=1)
    o = over_query_blocks(lambda qs, qp: diff_attend(qs[0], qp, k_all, v_all, k_pos, lam), (q,), pos)
    o = rms_norm(o, g) * (1.0 - lam_init)
    return o.reshape(B, T, C_OUT) @ w_out, k, v


def conv_ffn(x, prev, w_up, cw, cb, w_down):
    T = x.shape[1]
    u = x @ w_up
    up = jnp.concatenate([prev.astype(u.dtype), u], axis=1)
    c = cb + sum(up[:, j:j + T] * cw[j] for j in range(CONV_W))
    a, gate = jnp.split(c, 2, axis=-1)
    return (jax.nn.silu(gate) * a) @ w_down, up[:, T:]


def setup_inputs(seed: int = 0) -> dict:
    key = jax.random.key(seed)
    k = jax.random.split(key, 28)
    f32 = jnp.float32

    def nrm(i, shape, scale=1.0):
        return jax.random.normal(k[i], shape, f32) * scale

    return {
        'x_prompt': nrm(0, (BATCH, SEQ, D_MODEL)),
        'x_sample': nrm(1, (DEC_BATCH, DEC_SEQ, D_MODEL)),
        'cache_gla_state': nrm(2, (N_AB, DEC_BATCH, GLA_HEADS, GLA_DK, GLA_DV)),
        'cache_dsa_k': nrm(3, (N_AB, DEC_BATCH, PAST_LEN, DSA_KV_HEADS, DSA_HD)),
        'cache_dsa_v': nrm(4, (N_AB, DEC_BATCH, PAST_LEN, DSA_KV_HEADS, DSA_HD)),
        'cache_idx_k': nrm(5, (N_AB, DEC_BATCH, PAST_LEN, IDX_HD)),
        'cache_diff_k': nrm(6, (N_C, DEC_BATCH, PAST_LEN, DIFF_HEADS, 2, DIFF_HD)),
        'cache_diff_v': nrm(7, (N_C, DEC_BATCH, PAST_LEN, DIFF_HEADS, 2 * DIFF_HD)),
        'state_ffn_conv': nrm(8, (DEPTH, DEC_BATCH, CONV_W - 1, 2 * D_FF)),
        'norm_mix': 1.0 + nrm(9, (DEPTH, D_MODEL), 0.02),
        'norm_ffn': 1.0 + nrm(10, (DEPTH, D_MODEL), 0.02),
        'norm_final': 1.0 + nrm(11, (D_MODEL,), 0.02),
        'w_in_ab': nrm(12, (N_AB, D_MODEL, AB_IN), D_MODEL ** -0.5),
        'w_gla_a2': nrm(13, (N_AB, GLA_RANK, GLA_HEADS * GLA_DK), GLA_RANK ** -0.5),
        'b_gla_a': nrm(14, (N_AB, GLA_HEADS * GLA_DK), 0.1),
        'gla_norm': 1.0 + nrm(15, (N_AB, GLA_DV), 0.02),
        'w_out_ab': nrm(16, (N_AB, AB_OUT, D_MODEL), AB_OUT ** -0.5),
        'w_in_c': nrm(17, (N_C, D_MODEL, C_IN), D_MODEL ** -0.5),
        'lambda_q1': nrm(18, (N_C, DIFF_HD), 0.1),
        'lambda_k1': nrm(19, (N_C, DIFF_HD), 0.1),
        'lambda_q2': nrm(20, (N_C, DIFF_HD), 0.1),
        'lambda_k2': nrm(21, (N_C, DIFF_HD), 0.1),
        'diff_norm': 1.0 + nrm(22, (N_C, 2 * DIFF_HD), 0.02),
        'w_out_c': nrm(23, (N_C, C_OUT, D_MODEL), C_OUT ** -0.5),
        'w_up': nrm(24, (DEPTH, D_MODEL, 2 * D_FF), D_MODEL ** -0.5),
        'conv_w': nrm(25, (DEPTH, CONV_W, 2 * D_FF), CONV_W ** -0.5),
        'conv_b': nrm(26, (DEPTH, 2 * D_FF), 0.01),
        'w_down': nrm(27, (DEPTH, D_FF, D_MODEL), D_FF ** -0.5),
    }


def reference(x_prompt, x_sample, cache_gla_state, cache_dsa_k, cache_dsa_v, cache_idx_k,
              cache_diff_k, cache_diff_v, state_ffn_conv, norm_mix, norm_ffn, norm_final,
              w_in_ab, w_gla_a2, b_gla_a, gla_norm, w_out_ab, w_in_c, lambda_q1, lambda_k1,
              lambda_q2, lambda_k2, diff_norm, w_out_c, w_up, conv_w, conv_b, w_down):
    B = x_prompt.shape[0]
    hp, hs = x_prompt, x_sample
    p_gla, p_dk, p_dv, p_ik, p_ck, p_cv, p_conv = [], [], [], [], [], [], []
    s_gla, s_dk, s_dv, s_ik, s_ck, s_cv, s_conv = [], [], [], [], [], [], []
    for l in range(DEPTH):
        i = l // 2
        if l % 2 == 0:
            ab = (w_in_ab[i], w_gla_a2[i], b_gla_a[i], gla_norm[i], w_out_ab[i])
            zero_state = jnp.zeros((B, GLA_HEADS, GLA_DK, GLA_DV), hp.dtype)
            yp, gsp, kp, vp, ikp = mixer_ab(rms_norm(hp, norm_mix[l]), zero_state, None, None, None, *ab)
            ys, gss, ks, vs, iks = mixer_ab(rms_norm(hs, norm_mix[l]), cache_gla_state[i],
                                            cache_dsa_k[i], cache_dsa_v[i], cache_idx_k[i], *ab)
            p_gla.append(gsp); p_dk.append(kp); p_dv.append(vp); p_ik.append(ikp)
            s_gla.append(gss); s_dk.append(ks); s_dv.append(vs); s_ik.append(iks)
        else:
            lam_init = 0.8 - 0.6 * math.exp(-0.3 * l)
            c = (w_in_c[i], lambda_q1[i], lambda_k1[i], lambda_q2[i], lambda_k2[i], diff_norm[i], w_out_c[i], lam_init)
            yp, kp, vp = mixer_c(rms_norm(hp, norm_mix[l]), None, None, *c)
            ys, ks, vs = mixer_c(rms_norm(hs, norm_mix[l]), cache_diff_k[i], cache_diff_v[i], *c)
            p_ck.append(kp); p_cv.append(vp)
            s_ck.append(ks); s_cv.append(vs)
        hp = hp + yp
        hs = hs + ys
        ffn = (w_up[l], conv_w[l], conv_b[l], w_down[l])
        fp, cp = conv_ffn(rms_norm(hp, norm_ffn[l]), jnp.zeros((B, CONV_W - 1, 2 * D_FF), hp.dtype), *ffn)
        fs, cs = conv_ffn(rms_norm(hs, norm_ffn[l]), state_ffn_conv[l], *ffn)
        p_conv.append(cp); s_conv.append(cs)
        hp = hp + fp
        hs = hs + fs
    y_prompt = rms_norm(hp, norm_final)
    y_sample = rms_norm(hs, norm_final)
    p_gla_state = jnp.stack(p_gla); p_dsa_k = jnp.stack(p_dk); p_dsa_v = jnp.stack(p_dv)
    p_idx_k = jnp.stack(p_ik); p_diff_k = jnp.stack(p_ck); p_diff_v = jnp.stack(p_cv)
    p_ffn_conv = jnp.stack(p_conv)
    s_gla_state = jnp.stack(s_gla); s_dsa_k = jnp.stack(s_dk); s_dsa_v = jnp.stack(s_dv)
    s_idx_k = jnp.stack(s_ik); s_diff_k = jnp.stack(s_ck); s_diff_v = jnp.stack(s_cv)
    s_ffn_conv = jnp.stack(s_conv)
    return (y_prompt, y_sample, p_gla_state, p_dsa_k, p_dsa_v, p_idx_k, p_diff_k, p_diff_v, p_ffn_conv,
            s_gla_state, s_dsa_k, s_dsa_v, s_idx_k, s_diff_k, s_diff_v, s_ffn_conv)
```

```python
import functools
import math

import numpy as np
import jax
import jax.numpy as jnp
from jax import lax
from jax.experimental import pallas as pl
from jax.experimental.pallas import tpu as pltpu

CHUNK = 64
EPS = 1e-6
ROPE_THETA = 10000.0
GLA_RANK = 16
GLA_TAU = 16.0
DSA_HD = 128
IDX_HD = 128
IDX_TOPK = 256
DIFF_HD = 128
CONV_W = 3

LANE = 128
GLA_SUB = 16
DSA_TQ, DSA_KB = 128, 512
DIFF_TQ, DIFF_TK = 1024, 512
VMEM_LIMIT = 56 * 1024 * 1024
MM_VMEM_BUDGET = 44 * 1024 * 1024
NEG_INIT = -1e30
NEG_MASK = -3e38
INT_MIN = -2147483648

f32 = jnp.float32
bf16 = jnp.bfloat16


def _cparams(sem):
    return pltpu.CompilerParams(dimension_semantics=sem, vmem_limit_bytes=VMEM_LIMIT)


def _pick(dim, cands):
    for c in cands:
        if c <= dim and dim % c == 0:
            return c
    return dim


def _rmsnorm_kernel(x_ref, g_ref, o_ref):
    x = x_ref[...]
    y = x * lax.rsqrt(jnp.mean(x * x, axis=-1, keepdims=True) + EPS)
    o_ref[...] = (y * g_ref[...]).astype(o_ref.dtype)


def rmsnorm(x, g, out_dtype):
    M, D = x.shape
    tm = _pick(M, (256, 128, 64, 32, 16, 8))
    return pl.pallas_call(
        _rmsnorm_kernel,
        grid=(M // tm,),
        in_specs=[pl.BlockSpec((tm, D), lambda i: (i, 0)), pl.BlockSpec((1, D), lambda i: (0, 0))],
        out_specs=pl.BlockSpec((tm, D), lambda i: (i, 0)),
        out_shape=jax.ShapeDtypeStruct((M, D), out_dtype),
        compiler_params=_cparams(("parallel",)),
        name="rmsnorm",
    )(x, g.reshape(1, D).astype(f32))


def _mm_kernel(x_ref, w_ref, o_ref):
    o_ref[...] = jnp.dot(x_ref[...], w_ref[...], preferred_element_type=f32).astype(o_ref.dtype)


def _mm_res_kernel(x_ref, w_ref, r_ref, o_ref):
    acc = jnp.dot(x_ref[...], w_ref[...], preferred_element_type=f32)
    o_ref[...] = (r_ref[...] + acc).astype(o_ref.dtype)


def matmul(x, w, out_dtype, res=None, name="matmul"):
    M, K = x.shape
    N = w.shape[1]
    ob = jnp.dtype(out_dtype).itemsize
    best = None
    for tm in (1024, 768, 512, 384, 256, 128, 64, 32, 16, 8):
        if M % tm:
            continue
        for tn in (1024, 768, 512, 384, 256, 128):
            if N % tn:
                continue
            vm = 2 * (tm * K * 2 + K * tn * 2 + tm * tn * ob + (tm * tn * 4 if res is not None else 0))
            if vm > MM_VMEM_BUDGET:
                continue
            score = (tm * tn) / (tm + tn)
            if best is None or score > best[0]:
                best = (score, tm, tn)
    _, tm, tn = best
    in_specs = [pl.BlockSpec((tm, K), lambda i, j: (i, 0)), pl.BlockSpec((K, tn), lambda i, j: (0, j))]
    args = [x, w]
    kern = _mm_kernel
    if res is not None:
        in_specs.append(pl.BlockSpec((tm, tn), lambda i, j: (i, j)))
        args.append(res)
        kern = _mm_res_kernel
    return pl.pallas_call(
        kern,
        grid=(M // tm, N // tn),
        in_specs=in_specs,
        out_specs=pl.BlockSpec((tm, tn), lambda i, j: (i, j)),
        out_shape=jax.ShapeDtypeStruct((M, N), out_dtype),
        compiler_params=_cparams(("parallel", "arbitrary")),
        name=name,
    )(*args)


def _rope_kernel(x_ref, c_ref, s_ref, of_ref, ob_ref, *, gpb):
    cos, sin = c_ref[...], s_ref[...]
    for g in range(gpb):
        x = x_ref[:, g * LANE:(g + 1) * LANE]
        y = x * cos + pltpu.roll(x, LANE // 2, 1) * sin
        of_ref[:, g * LANE:(g + 1) * LANE] = y
        ob_ref[:, g * LANE:(g + 1) * LANE] = y.astype(bf16)


def rope_cols(z, cos_t, sin_t, col0, ngroups):
    M = z.shape[0]
    tm = _pick(M, (256, 128, 64, 32, 16, 8))
    g0 = col0 // LANE
    gpb = next(g for g in (8, 4, 2, 1) if ngroups % g == 0 and g0 % g == 0)
    w = gpb * LANE
    spec_o = pl.BlockSpec((tm, w), lambda i, j: (i, j))
    return pl.pallas_call(
        functools.partial(_rope_kernel, gpb=gpb),
        grid=(M // tm, ngroups // gpb),
        in_specs=[pl.BlockSpec((tm, w), lambda i, j: (i, g0 // gpb + j)),
                  pl.BlockSpec((tm, LANE), lambda i, j: (i, 0)),
                  pl.BlockSpec((tm, LANE), lambda i, j: (i, 0))],
        out_specs=[spec_o, spec_o],
        out_shape=[jax.ShapeDtypeStruct((M, ngroups * LANE), f32),
                   jax.ShapeDtypeStruct((M, ngroups * LANE), bf16)],
        compiler_params=_cparams(("parallel", "arbitrary")),
        name="rope",
    )(z, cos_t, sin_t)


def rope_tables(pos):
    half = LANE // 2
    inv = 1.0 / (ROPE_THETA ** (jnp.arange(half, dtype=f32) * (2.0 / LANE)))
    ang = pos.astype(f32)[:, None] * inv[None, :]
    cos, sin = jnp.cos(ang), jnp.sin(ang)
    return jnp.concatenate([cos, cos], axis=-1), jnp.concatenate([-sin, sin], axis=-1)


def _gla_kernel(q_ref, k_ref, v_ref, gg_ref, misc_ref, wa2_ref, ba_ref, gn_ref, s0_ref,
                o_ref, sout_ref, st_ref, b_s, k_s, *, c, n_inner, dk, dv):
    step = pl.program_id(2)

    @pl.when(step == 0)
    def _():
        st_ref[...] = s0_ref[0, 0].T

    nsub = c // GLA_SUB
    row_c = lax.broadcasted_iota(jnp.int32, (c, dk), 0)
    row_s = lax.broadcasted_iota(jnp.int32, (GLA_SUB, LANE), 0)
    lane_s = lax.broadcasted_iota(jnp.int32, (GLA_SUB, LANE), 1)
    nt = (((1,), (1,)), ((), ()))
    tn = (((0,), (0,)), ((), ()))

    def chunk(ci, carry):
        r = pl.multiple_of(ci * c, c)
        q = q_ref[pl.ds(r, c), :] * (dk ** -0.5)
        k = k_ref[pl.ds(r, c), :]
        v = v_ref[pl.ds(r, c), :]
        ga = misc_ref[pl.ds(r, c), :][:, :GLA_RANK]
        x = jnp.dot(ga.astype(bf16), wa2_ref[...], preferred_element_type=f32) + ba_ref[...]
        loga = (jnp.minimum(x, 0.0) - jnp.log1p(jnp.exp(-jnp.abs(x)))) / GLA_TAU
        b = loga
        d = 1
        while d < c:
            b = b + jnp.where(row_c >= d, pltpu.roll(b, d, 0), 0.0)
            d *= 2
        b_s[...] = b
        k_s[...] = k
        st = st_ref[...]
        vb = v.astype(bf16)
        o_inter = lax.dot_general((q * jnp.exp(b)).astype(bf16), st.astype(bf16), nt,
                                  preferred_element_type=f32)
        outs = []
        for i in range(nsub):
            r0 = i * GLA_SUB
            q_i = q[r0:r0 + GLA_SUB]
            b_i = b[r0:r0 + GLA_SUB]

            def diag_body(s, att, q_i=q_i, b_i=b_i, r0=r0):
                bs_row = b_s[pl.ds(r0 + s, 1), :]
                ks_row = k_s[pl.ds(r0 + s, 1), :]
                e = jnp.exp(jnp.minimum(b_i - bs_row, 0.0))
                col = jnp.sum(q_i * ks_row * e, axis=-1, keepdims=True)
                return jnp.where(lane_s == s, col, att)

            att = lax.fori_loop(0, GLA_SUB, diag_body, jnp.zeros((GLA_SUB, LANE), f32))
            att = jnp.where(row_s >= lane_s, att, 0.0)[:, :GLA_SUB]
            o_i = jnp.dot(att.astype(bf16), vb[r0:r0 + GLA_SUB], preferred_element_type=f32)
            if i > 0:
                bref = b[r0 - 1:r0]
                qq = q_i * jnp.exp(b_i - bref)
                kk = k[:r0] * jnp.exp(bref - b[:r0])
                a_off = lax.dot_general(qq.astype(bf16), kk.astype(bf16), nt, preferred_element_type=f32)
                o_i = o_i + jnp.dot(a_off.astype(bf16), vb[:r0], preferred_element_type=f32)
            outs.append(o_i)
        o = o_inter + (jnp.concatenate(outs, axis=0) if nsub > 1 else outs[0])
        b_last = b[c - 1:c]
        kd = k * jnp.exp(b_last - b)
        st_ref[...] = jnp.exp(b_last) * st + lax.dot_general(vb, kd.astype(bf16), tn,
                                                            preferred_element_type=f32)
        y = o * lax.rsqrt(jnp.mean(o * o, axis=-1, keepdims=True) + EPS) * gn_ref[...]
        gg = gg_ref[pl.ds(r, c), :]
        y = y * (gg * (1.0 / (1.0 + jnp.exp(-gg))))
        o_ref[pl.ds(r, c), :] = y.astype(o_ref.dtype)
        return carry

    lax.fori_loop(0, n_inner, chunk, 0)

    @pl.when(step == pl.num_programs(2) - 1)
    def _():
        sout_ref[0, 0] = st_ref[...].T


def gla(z, s0, wa2, ba, gn, *, n_seq, T, row0, heads, dk, dv, cols):
    c = min(CHUNK, T)
    tb = _pick(T, (256, 128, 64, 32, 16))
    tb = max(tb, c)
    n_steps = T // tb
    rb0 = row0 // tb
    cq, ck, cv, cg, cm = cols

    def rows(b, s):
        return rb0 + b * n_steps + s

    in_specs = [
        pl.BlockSpec((tb, dk), lambda b, h, s: (rows(b, s), cq // dk + h)),
        pl.BlockSpec((tb, dk), lambda b, h, s: (rows(b, s), ck // dk + h)),
        pl.BlockSpec((tb, dv), lambda b, h, s: (rows(b, s), cv // dv + h)),
        pl.BlockSpec((tb, dv), lambda b, h, s: (rows(b, s), cg // dv + h)),
        pl.BlockSpec((tb, LANE), lambda b, h, s: (rows(b, s), cm // LANE)),
        pl.BlockSpec((GLA_RANK, dk), lambda b, h, s: (0, h)),
        pl.BlockSpec((1, dk), lambda b, h, s: (0, h)),
        pl.BlockSpec((1, dv), lambda b, h, s: (0, 0)),
        pl.BlockSpec((1, 1, dk, dv), lambda b, h, s: (b, h, 0, 0)),
    ]
    out_specs = [
        pl.BlockSpec((tb, dv), lambda b, h, s: (b * n_steps + s, h)),
        pl.BlockSpec((1, 1, dk, dv), lambda b, h, s: (b, h, 0, 0)),
    ]
    return pl.pallas_call(
        functools.partial(_gla_kernel, c=c, n_inner=tb // c, dk=dk, dv=dv),
        grid=(n_seq, heads, n_steps),
        in_specs=in_specs,
        out_specs=out_specs,
        out_shape=[jax.ShapeDtypeStruct((n_seq * T, heads * dv), bf16),
                   jax.ShapeDtypeStruct((n_seq, heads, dk, dv), f32)],
        scratch_shapes=[pltpu.VMEM((dv, dk), f32), pltpu.VMEM((c, dk), f32), pltpu.VMEM((c, dk), f32)],
        compiler_params=_cparams(("parallel", "parallel", "arbitrary")),
        name="gla",
    )(z, z, z, z, z, wa2, ba, gn, s0)


def _dsa_kernel(dq_ref, iq_ref, misc_ref, k_ref, v_ref, ik_ref, o_ref, key_s, m_s, l_s, acc_s,
                *, tq, kb, s_valid, pos0, topk, heads, kv_heads, idx_heads):
    j = pl.program_id(1)
    qpos0 = pos0 + j * tq
    vis_end = jnp.minimum(((qpos0 + tq - 1) // CHUNK + 1) * CHUNK, s_valid)
    nkb = (vis_end + kb - 1) // kb
    qchunk = (qpos0 + lax.broadcasted_iota(jnp.int32, (tq, 1), 0)) // CHUNK
    w = misc_ref[...][:, GLA_RANK:GLA_RANK + idx_heads]
    nt = (((1,), (1,)), ((), ()))
    group = heads // kv_heads
    scale = DSA_HD ** -0.5

    def score_body(b, carry):
        c0 = pl.multiple_of(b * kb, kb)
        ikb = ik_ref[0, pl.ds(c0, kb), :]
        acc = jnp.zeros((tq, kb), f32)
        for h in range(idx_heads):
            s = lax.dot_general(iq_ref[:, h * IDX_HD:(h + 1) * IDX_HD], ikb, nt, preferred_element_type=f32)
            acc = acc + w[:, h:h + 1] * jnp.maximum(s, 0.0)
        kpos = c0 + lax.broadcasted_iota(jnp.int32, (1, kb), 1)
        vis = jnp.logical_and(kpos // CHUNK <= qchunk, kpos < s_valid)
        bits = pltpu.bitcast(acc, jnp.int32)
        key = bits ^ ((bits >> 31) & 0x7FFFFFFF)
        key_s[:, pl.ds(c0, kb)] = jnp.where(vis, key, INT_MIN)
        return carry

    lax.fori_loop(0, nkb, score_body, 0)

    one, zero = jnp.int32(1), jnp.int32(0)
    izeros = jnp.zeros((tq, LANE), jnp.int32)
    lane_i = lax.broadcasted_iota(jnp.int32, (1, LANE), 1)

    def count(pred):
        def body(b, cacc):
            c0 = pl.multiple_of(b * kb, kb)
            blk = key_s[:, pl.ds(c0, kb)]
            for t in range(kb // LANE):
                cacc = cacc + jnp.where(pred(blk[:, t * LANE:(t + 1) * LANE], c0 + t * LANE + lane_i), one, zero)
            return cacc
        return jnp.sum(lax.fori_loop(0, nkb, body, izeros), axis=-1, keepdims=True)

    def bit_body(i, carry):
        t_u, c_at = carry
        cand_u = t_u | lax.shift_left(one, 31 - i)
        cand = cand_u ^ INT_MIN
        cnt = count(lambda kt, it: kt >= cand)
        ok = cnt >= topk
        return jnp.where(ok, cand_u, t_u), jnp.where(ok, cnt, c_at)

    t_u, c_at = lax.fori_loop(0, 32, bit_body, (izeros, izeros))
    thr = jnp.maximum(t_u ^ INT_MIN, INT_MIN + 1)

    @pl.when(jnp.max(c_at) > topk)
    def _():
        need = topk - count(lambda kt, it: kt > thr)
        nbits = int(s_valid).bit_length()

        def idx_body(i, lo):
            cand = lo | lax.shift_left(one, nbits - 1 - i)
            c = count(lambda kt, it: jnp.logical_and(kt == thr, it < cand))
            return jnp.where(c < need, cand, lo)

        lo = lax.fori_loop(0, nbits, idx_body, izeros)

        def drop_body(b, carry):
            c0 = pl.multiple_of(b * kb, kb)
            blk = key_s[:, pl.ds(c0, kb)]
            tiles = []
            for t in range(kb // LANE):
                kt = blk[:, t * LANE:(t + 1) * LANE]
                drop = jnp.logical_and(kt == thr, c0 + t * LANE + lane_i > lo)
                tiles.append(jnp.where(drop, INT_MIN, kt))
            key_s[:, pl.ds(c0, kb)] = jnp.concatenate(tiles, axis=1)
            return carry

        lax.fori_loop(0, nkb, drop_body, 0)

    for g in range(kv_heads):
        qg = jnp.concatenate([dq_ref[:, (g * group + i) * DSA_HD:(g * group + i + 1) * DSA_HD]
                              for i in range(group)], axis=0)
        m_s[...] = jnp.full(m_s.shape, NEG_INIT, f32)
        l_s[...] = jnp.zeros(l_s.shape, f32)
        acc_s[...] = jnp.zeros(acc_s.shape, f32)

        def kv_body(b, carry, g=g, qg=qg):
            c0 = pl.multiple_of(b * kb, kb)
            kblk = k_ref[0, pl.ds(c0, kb), g * DSA_HD:(g + 1) * DSA_HD]
            vblk = v_ref[0, pl.ds(c0, kb), g * DSA_HD:(g + 1) * DSA_HD]
            keys = key_s[:, pl.ds(c0, kb)]
            bias = jnp.concatenate(
                [jnp.where(keys[:, t * LANE:(t + 1) * LANE] >= thr, 0.0, NEG_MASK) for t in range(kb // LANE)],
                axis=1)
            bias = jnp.concatenate([bias] * group, axis=0)
            s = lax.dot_general(qg, kblk, nt, preferred_element_type=f32) * scale + bias
            m_old = m_s[...]
            m_new = jnp.maximum(m_old, jnp.max(s, axis=-1, keepdims=True))
            p = jnp.exp(s - m_new)
            alpha = jnp.exp(m_old - m_new)
            l_s[...] = alpha * l_s[...] + jnp.sum(p, axis=-1, keepdims=True)
            acc_s[...] = alpha * acc_s[...] + jnp.dot(p.astype(bf16), vblk, preferred_element_type=f32)
            m_s[...] = m_new
            return carry

        lax.fori_loop(0, nkb, kv_body, 0)
        out = acc_s[...] / l_s[...]
        for i in range(group):
            hh = g * group + i
            o_ref[:, hh * DSA_HD:(hh + 1) * DSA_HD] = out[i * tq:(i + 1) * tq].astype(o_ref.dtype)


def dsa(dq, iq, z, k_all, v_all, ik_all, *, n_seq, T, row0, s_valid, pos0, heads, kv_heads, idx_heads,
        misc_col, tq, kb):
    s_pad = k_all.shape[1]
    nq = T // tq
    rb0 = row0 // tq
    topk = min(IDX_TOPK, s_valid // 4)
    group = heads // kv_heads
    kern = functools.partial(_dsa_kernel, tq=tq, kb=kb, s_valid=s_valid, pos0=pos0, topk=topk, heads=heads,
                             kv_heads=kv_heads, idx_heads=idx_heads)
    return pl.pallas_call(
        kern,
        grid=(n_seq, nq),
        in_specs=[
            pl.BlockSpec((tq, heads * DSA_HD), lambda b, j: (rb0 + b * nq + j, 0)),
            pl.BlockSpec((tq, idx_heads * IDX_HD), lambda b, j: (rb0 + b * nq + j, 0)),
            pl.BlockSpec((tq, LANE), lambda b, j: (rb0 + b * nq + j, misc_col // LANE)),
            pl.BlockSpec((1, s_pad, kv_heads * DSA_HD), lambda b, j: (b, 0, 0)),
            pl.BlockSpec((1, s_pad, kv_heads * DSA_HD), lambda b, j: (b, 0, 0)),
            pl.BlockSpec((1, s_pad, IDX_HD), lambda b, j: (b, 0, 0)),
        ],
        out_specs=pl.BlockSpec((tq, heads * DSA_HD), lambda b, j: (b * nq + j, 0)),
        out_shape=jax.ShapeDtypeStruct((n_seq * T, heads * DSA_HD), bf16),
        scratch_shapes=[pltpu.VMEM((tq, s_pad), jnp.int32), pltpu.VMEM((group * tq, 1), f32),
                        pltpu.VMEM((group * tq, 1), f32), pltpu.VMEM((group * tq, DSA_HD), f32)],
        compiler_params=_cparams(("parallel", "arbitrary")),
        name="dsa",
    )(dq, iq, z, k_all, v_all, ik_all)


def _diff_lambda(lq1, lk1, lq2, lk2, lam_init):
    return (jnp.exp(jnp.sum(lq1[...] * lk1[...], axis=-1, keepdims=True))
            - jnp.exp(jnp.sum(lq2[...] * lk2[...], axis=-1, keepdims=True)) + lam_init)


def _diff_finish(o, gn_ref, lam_init):
    y = o * lax.rsqrt(jnp.mean(o * o, axis=-1, keepdims=True) + EPS) * gn_ref[...]
    return y * (1.0 - lam_init)


def _diff_prompt_kernel(q_ref, k_ref, v_ref, lq1, lk1, lq2, lk2, gn_ref, o_ref, m_s, l_s, acc_s,
                        *, tq, tk, lam_init):
    qi = pl.program_id(2)
    ki = pl.program_id(3)
    last_k = ((qi + 1) * tq - 1) // tk
    nt = (((1,), (1,)), ((), ()))
    scale = DIFF_HD ** -0.5

    @pl.when(ki == 0)
    def _():
        m_s[...] = jnp.full(m_s.shape, NEG_INIT, f32)
        l_s[...] = jnp.zeros(l_s.shape, f32)
        acc_s[...] = jnp.zeros(acc_s.shape, f32)

    def update(masked):
        vb = v_ref[...]
        if masked:
            qc = (qi * tq + lax.broadcasted_iota(jnp.int32, (tq, 1), 0)) // CHUNK
            kc = (ki * tk + lax.broadcasted_iota(jnp.int32, (1, tk), 1)) // CHUNK
            bias = jnp.where(kc <= qc, 0.0, NEG_MASK)
        for c in range(2):
            s = lax.dot_general(q_ref[:, c * DIFF_HD:(c + 1) * DIFF_HD], k_ref[:, c * DIFF_HD:(c + 1) * DIFF_HD],
                                nt, preferred_element_type=f32) * scale
            if masked:
                s = s + bias
            m_old = m_s[c]
            m_new = jnp.maximum(m_old, jnp.max(s, axis=-1, keepdims=True))
            p = jnp.exp(s - m_new)
            alpha = jnp.exp(m_old - m_new)
            l_s[c] = alpha * l_s[c] + jnp.sum(p, axis=-1, keepdims=True)
            acc_s[c] = alpha * acc_s[c] + jnp.dot(p.astype(bf16), vb, preferred_element_type=f32)
            m_s[c] = m_new

    needs_mask = (ki + 1) * tk > qi * tq + CHUNK

    @pl.when(jnp.logical_and(ki <= last_k, needs_mask))
    def _():
        update(True)

    @pl.when(jnp.logical_and(ki <= last_k, jnp.logical_not(needs_mask)))
    def _():
        update(False)

    @pl.when(ki == last_k)
    def _():
        lam = _diff_lambda(lq1, lk1, lq2, lk2, lam_init)
        o = acc_s[0] / l_s[0] - lam * (acc_s[1] / l_s[1])
        o_ref[...] = _diff_finish(o, gn_ref, lam_init).astype(o_ref.dtype)


def diff_prompt(q, k, v, lams, gn, *, n_seq, T, heads, lam_init):
    hw = 2 * DIFF_HD
    tq = _pick(T, (DIFF_TQ, 512, 256, 128))
    tk = _pick(T, (DIFF_TK, 256, 128))
    nq, nk = T // tq, T // tk

    def kv_map(b, h, qi, ki):
        return (b * nk + jnp.minimum(ki, ((qi + 1) * tq - 1) // tk), h)

    vec = pl.BlockSpec((1, DIFF_HD), lambda b, h, qi, ki: (0, 0))
    return pl.pallas_call(
        functools.partial(_diff_prompt_kernel, tq=tq, tk=tk, lam_init=lam_init),
        grid=(n_seq, heads, nq, nk),
        in_specs=[pl.BlockSpec((tq, hw), lambda b, h, qi, ki: (b * nq + qi, h)),
                  pl.BlockSpec((tk, hw), kv_map), pl.BlockSpec((tk, hw), kv_map),
                  vec, vec, vec, vec, pl.BlockSpec((1, hw), lambda b, h, qi, ki: (0, 0))],
        out_specs=pl.BlockSpec((tq, hw), lambda b, h, qi, ki: (b * nq + qi, h)),
        out_shape=jax.ShapeDtypeStruct((n_seq * T, heads * hw), bf16),
        scratch_shapes=[pltpu.VMEM((2, tq, 1), f32), pltpu.VMEM((2, tq, 1), f32), pltpu.VMEM((2, tq, hw), f32)],
        compiler_params=_cparams(("parallel", "parallel", "parallel", "arbitrary")),
        name="diff_prompt",
    )(q, k, v, *lams, gn)


def _diff_sample_kernel(q_ref, kn_ref, vn_ref, kp_ref, vp_ref, lq1, lk1, lq2, lk2, gn_ref, o_ref, *, lam_init):
    nt = (((1,), (1,)), ((), ()))
    scale = DIFF_HD ** -0.5
    lam = _diff_lambda(lq1, lk1, lq2, lk2, lam_init)
    kp = kp_ref[0].astype(bf16)
    a_p = None
    a_n = None
    for c in range(2):
        qc = q_ref[:, c * DIFF_HD:(c + 1) * DIFF_HD]
        sp = lax.dot_general(qc, kp[:, c * DIFF_HD:(c + 1) * DIFF_HD], nt, preferred_element_type=f32) * scale
        sn = lax.dot_general(qc, kn_ref[:, c * DIFF_HD:(c + 1) * DIFF_HD], nt, preferred_element_type=f32) * scale
        m = jnp.maximum(jnp.max(sp, axis=-1, keepdims=True), jnp.max(sn, axis=-1, keepdims=True))
        ep = jnp.exp(sp - m)
        en = jnp.exp(sn - m)
        inv = 1.0 / (jnp.sum(ep, axis=-1, keepdims=True) + jnp.sum(en, axis=-1, keepdims=True))
        if c == 0:
            a_p, a_n = ep * inv, en * inv
        else:
            a_p, a_n = a_p - lam * (ep * inv), a_n - lam * (en * inv)
    o = (jnp.dot(a_p.astype(bf16), vp_ref[0].astype(bf16), preferred_element_type=f32)
         + jnp.dot(a_n.astype(bf16), vn_ref[...], preferred_element_type=f32))
    o_ref[...] = _diff_finish(o, gn_ref, lam_init).astype(o_ref.dtype)


def diff_sample(q, k, v, k_past, v_past, lams, gn, *, n_seq, T, row0, heads, lam_init):
    hw = 2 * DIFF_HD
    past = k_past.shape[1]
    assert (past + T - 1) // CHUNK == past // CHUNK and past % CHUNK == 0
    rb0 = row0 // T
    new = pl.BlockSpec((T, hw), lambda b, h: (rb0 + b, h))
    old = pl.BlockSpec((1, past, hw), lambda b, h: (b, 0, h))
    vec = pl.BlockSpec((1, DIFF_HD), lambda b, h: (0, 0))
    return pl.pallas_call(
        functools.partial(_diff_sample_kernel, lam_init=lam_init),
        grid=(n_seq, heads),
        in_specs=[new, new, new, old, old, vec, vec, vec, vec, pl.BlockSpec((1, hw), lambda b, h: (0, 0))],
        out_specs=pl.BlockSpec((T, hw), lambda b, h: (b, h)),
        out_shape=jax.ShapeDtypeStruct((n_seq * T, heads * hw), bf16),
        compiler_params=_cparams(("parallel", "parallel")),
        name="diff_sample",
    )(q, k, v, k_past, v_past, *lams, gn)


def _conv_gate_kernel(ua_ref, ug_ref, pa_ref, pg_ref, wa_ref, wg_ref, ba_ref, bg_ref, o_ref):
    def conv(u_ref, p_ref, w_ref, b_ref):
        u = u_ref[...].astype(f32)
        row = lax.broadcasted_iota(jnp.int32, u.shape, 0)
        p0 = p_ref[0, 0:1, :]
        p1 = p_ref[0, 1:2, :]
        u1 = jnp.where(row == 0, p1, pltpu.roll(u, 1, 0))
        u2 = jnp.where(row == 0, p0, jnp.where(row == 1, p1, pltpu.roll(u, 2, 0)))
        return b_ref[...] + u2 * w_ref[0:1, :] + u1 * w_ref[1:2, :] + u * w_ref[2:3, :]

    a = conv(ua_ref, pa_ref, wa_ref, ba_ref)
    g = conv(ug_ref, pg_ref, wg_ref, bg_ref)
    o_ref[...] = (g * (1.0 / (1.0 + jnp.exp(-g))) * a).astype(o_ref.dtype)


def conv_gate(u, prev, cw, cb, *, n_seq, T, row0, dff):
    tn = LANE if T > 256 else _pick(dff, (5504, 1024, 512, 256, 128))
    nb = dff // tn
    rb0 = row0 // T
    cb2 = cb.reshape(1, 2 * dff)
    return pl.pallas_call(
        _conv_gate_kernel,
        grid=(n_seq, nb),
        in_specs=[pl.BlockSpec((T, tn), lambda b, j: (rb0 + b, j)),
                  pl.BlockSpec((T, tn), lambda b, j: (rb0 + b, nb + j)),
                  pl.BlockSpec((1, CONV_W - 1, tn), lambda b, j: (b, 0, j)),
                  pl.BlockSpec((1, CONV_W - 1, tn), lambda b, j: (b, 0, nb + j)),
                  pl.BlockSpec((CONV_W, tn), lambda b, j: (0, j)),
                  pl.BlockSpec((CONV_W, tn), lambda b, j: (0, nb + j)),
                  pl.BlockSpec((1, tn), lambda b, j: (0, j)),
                  pl.BlockSpec((1, tn), lambda b, j: (0, nb + j))],
        out_specs=pl.BlockSpec((T, tn), lambda b, j: (b, j)),
        out_shape=jax.ShapeDtypeStruct((n_seq * T, dff), bf16),
        compiler_params=_cparams(("parallel", "parallel")),
        name="conv_gate",
    )(u, u, prev, prev, cw, cw, cb2, cb2)


def _conv_ffn(h, g_norm, w_up, cw, cb, w_down, prev_s, *, B, S, DB, DS):
    mp = B * S
    dff = w_down.shape[0]
    hn = rmsnorm(h, g_norm, bf16)
    u = matmul(hn, w_up.astype(bf16), bf16, name="ffn_up")
    prev_p = jnp.zeros((B, CONV_W - 1, 2 * dff), f32)
    g_p = conv_gate(u, prev_p, cw, cb, n_seq=B, T=S, row0=0, dff=dff)
    g_s = conv_gate(u, prev_s, cw, cb, n_seq=DB, T=DS, row0=mp, dff=dff)
    g = jnp.concatenate([g_p, g_s], axis=0)
    h = matmul(g, w_down.astype(bf16), f32, res=h, name="ffn_down")
    conv_p = u[:mp].reshape(B, S, 2 * dff)[:, S - (CONV_W - 1):].astype(f32)
    conv_s = u[mp:].reshape(DB, DS, 2 * dff)[:, DS - (CONV_W - 1):].astype(f32)
    return h, conv_p, conv_s


def kernel(x_prompt, x_sample, cache_gla_state, cache_dsa_k, cache_dsa_v, cache_idx_k, cache_diff_k, cache_diff_v, state_ffn_conv, norm_mix, norm_ffn, norm_final, w_in_ab, w_gla_a2, b_gla_a, gla_norm, w_out_ab, w_in_c, lambda_q1, lambda_k1, lambda_q2, lambda_k2, diff_norm, w_out_c, w_up, conv_w, conv_b, w_down):
    B, S, D = x_prompt.shape
    DB, DS, _ = x_sample.shape
    depth = norm_mix.shape[0]
    _, _, gh, gdk, gdv = cache_gla_state.shape
    past = cache_dsa_k.shape[2]
    kvh = cache_dsa_k.shape[3]
    dh = w_out_ab.shape[1] - gh * gdv
    dsa_heads = dh // DSA_HD
    idx_heads = (w_in_ab.shape[2] - (2 * gh * gdk + 2 * gh * gdv + GLA_RANK + dh + 2 * kvh * DSA_HD + IDX_HD)) // (IDX_HD + 1)
    diff_heads = cache_diff_k.shape[3]
    mp, ms = B * S, DB * DS
    M = mp + ms

    pos = jnp.concatenate([jnp.tile(jnp.arange(S, dtype=jnp.int32), B),
                           jnp.tile(past + jnp.arange(DS, dtype=jnp.int32), DB)])
    cos_t, sin_t = rope_tables(pos)

    h = jnp.concatenate([x_prompt.reshape(mp, D), x_sample.reshape(ms, D)], axis=0)
    outs = {k: [] for k in ("p_gla", "p_dk", "p_dv", "p_ik", "p_ck", "p_cv", "p_conv",
                            "s_gla", "s_dk", "s_dv", "s_ik", "s_ck", "s_cv", "s_conv")}

    for l in range(depth):
        i = l // 2
        hn = rmsnorm(h, norm_mix[l], bf16)
        if l % 2 == 0:
            sizes = (gh * gdk, gh * gdk, gh * gdv, gh * gdv, GLA_RANK, dh, kvh * DSA_HD, kvh * DSA_HD,
                     idx_heads * IDX_HD, IDX_HD, idx_heads)
            offs = np.concatenate([[0], np.cumsum(sizes)]).tolist()
            wi = w_in_ab[i]
            seg = lambda n: wi[:, offs[n]:offs[n + 1]]
            pad = LANE - GLA_RANK - idx_heads
            w_perm = jnp.concatenate([seg(0), seg(1), seg(2), seg(3), seg(5), seg(6), seg(7), seg(8), seg(9),
                                      seg(4), seg(10), jnp.zeros((D, pad), wi.dtype)], axis=1).astype(bf16)
            c_gq = 0
            c_gk = c_gq + gh * gdk
            c_gv = c_gk + gh * gdk
            c_gg = c_gv + gh * gdv
            c_dq = c_gg + gh * gdv
            c_dk = c_dq + dh
            c_dv = c_dk + kvh * DSA_HD
            c_iq = c_dv + kvh * DSA_HD
            c_ik = c_iq + idx_heads * IDX_HD
            c_misc = c_ik + IDX_HD
            z = matmul(hn, w_perm, f32, name="in_ab")

            qk_f, qk_b = rope_cols(z, cos_t, sin_t, c_dq, dsa_heads + kvh)
            _, iq_b = rope_cols(z, cos_t, sin_t, c_iq, idx_heads)
            ik_f, ik_b = rope_cols(z, cos_t, sin_t, c_ik, 1)
            dq_b = qk_b[:, :dh]
            dk_f, dk_b = qk_f[:, dh:], qk_b[:, dh:]
            dv_f = z[:, c_dv:c_dv + kvh * DSA_HD]
            dv_b = dv_f.astype(bf16)

            wa2 = w_gla_a2[i].astype(bf16)
            ba = b_gla_a[i].reshape(1, gh * gdk)
            gn = gla_norm[i].reshape(1, gdv)
            gcols = (c_gq, c_gk, c_gv, c_gg, c_misc)
            go_p, st_p = gla(z, jnp.zeros((B, gh, gdk, gdv), f32), wa2, ba, gn, n_seq=B, T=S, row0=0,
                             heads=gh, dk=gdk, dv=gdv, cols=gcols)
            go_s, st_s = gla(z, cache_gla_state[i], wa2, ba, gn, n_seq=DB, T=DS, row0=mp,
                             heads=gh, dk=gdk, dv=gdv, cols=gcols)

            kvw = kvh * DSA_HD
            tq_p = _pick(S, (DSA_TQ,))
            kb_p = _pick(S, (DSA_KB, 256, 128))
            do_p = dsa(dq_b, iq_b, z, dk_b[:mp].reshape(B, S, kvw), dv_b[:mp].reshape(B, S, kvw),
                       ik_b[:mp].reshape(B, S, IDX_HD), n_seq=B, T=S, row0=0, s_valid=S, pos0=0,
                       heads=dsa_heads, kv_heads=kvh, idx_heads=idx_heads, misc_col=c_misc, tq=tq_p, kb=kb_p)
            kb_s = DSA_KB
            s_valid = past + DS
            s_pad = -(-s_valid // kb_s) * kb_s

            def cat(cache, new, width):
                return jnp.concatenate([cache.reshape(DB, past, width).astype(bf16), new[mp:].reshape(DB, DS, width),
                                        jnp.zeros((DB, s_pad - s_valid, width), bf16)], axis=1)

            do_s = dsa(dq_b, iq_b, z, cat(cache_dsa_k[i], dk_b, kvw), cat(cache_dsa_v[i], dv_b, kvw),
                       cat(cache_idx_k[i], ik_b, IDX_HD), n_seq=DB, T=DS, row0=mp, s_valid=s_valid, pos0=past,
                       heads=dsa_heads, kv_heads=kvh, idx_heads=idx_heads, misc_col=c_misc, tq=DS, kb=kb_s)

            mix = jnp.concatenate([jnp.concatenate([go_p, go_s], axis=0),
                                   jnp.concatenate([do_p, do_s], axis=0)], axis=1)
            h = matmul(mix, w_out_ab[i].astype(bf16), f32, res=h, name="out_ab")

            outs["p_gla"].append(st_p)
            outs["s_gla"].append(st_s)
            outs["p_dk"].append(dk_f[:mp].reshape(B, S, kvh, DSA_HD))
            outs["s_dk"].append(dk_f[mp:].reshape(DB, DS, kvh, DSA_HD))
            outs["p_dv"].append(dv_f[:mp].reshape(B, S, kvh, DSA_HD))
            outs["s_dv"].append(dv_f[mp:].reshape(DB, DS, kvh, DSA_HD))
            outs["p_ik"].append(ik_f[:mp].reshape(B, S, IDX_HD))
            outs["s_ik"].append(ik_f[mp:].reshape(DB, DS, IDX_HD))
        else:
            lam_init = 0.8 - 0.6 * math.exp(-0.3 * l)
            cw_ = diff_heads * 2 * DIFF_HD
            z = matmul(hn, w_in_c[i].astype(bf16), f32, name="in_c")
            qk_f, qk_b = rope_cols(z, cos_t, sin_t, 0, 2 * cw_ // LANE)
            q_b, k_b = qk_b[:, :cw_], qk_b[:, cw_:]
            k_f = qk_f[:, cw_:]
            v_f = z[:, 2 * cw_:]
            v_b = v_f.astype(bf16)
            lams = tuple(a[i].reshape(1, DIFF_HD) for a in (lambda_q1, lambda_k1, lambda_q2, lambda_k2))
            gn = diff_norm[i].reshape(1, 2 * DIFF_HD)
            o_p = diff_prompt(q_b, k_b, v_b, lams, gn, n_seq=B, T=S, heads=diff_heads, lam_init=lam_init)
            o_s = diff_sample(q_b, k_b, v_b, cache_diff_k[i].reshape(DB, past, cw_),
                              cache_diff_v[i].reshape(DB, past, cw_), lams, gn, n_seq=DB, T=DS, row0=mp,
                              heads=diff_heads, lam_init=lam_init)
            h = matmul(jnp.concatenate([o_p, o_s], axis=0), w_out_c[i].astype(bf16), f32, res=h, name="out_c")
            outs["p_ck"].append(k_f[:mp].reshape(B, S, diff_heads, 2, DIFF_HD))
            outs["s_ck"].append(k_f[mp:].reshape(DB, DS, diff_heads, 2, DIFF_HD))
            outs["p_cv"].append(v_f[:mp].reshape(B, S, diff_heads, 2 * DIFF_HD))
            outs["s_cv"].append(v_f[mp:].reshape(DB, DS, diff_heads, 2 * DIFF_HD))

        h, conv_p, conv_s = _conv_ffn(h, norm_ffn[l], w_up[l], conv_w[l], conv_b[l], w_down[l], state_ffn_conv[l],
                                      B=B, S=S, DB=DB, DS=DS)
        outs["p_conv"].append(conv_p)
        outs["s_conv"].append(conv_s)

    y = rmsnorm(h, norm_final, f32)
    st = lambda k: jnp.stack(outs[k])
    return (y[:mp].reshape(B, S, D), y[mp:].reshape(DB, DS, D),
            st("p_gla"), st("p_dk"), st("p_dv"), st("p_ik"), st("p_ck"), st("p_cv"), st("p_conv"),
            st("s_gla"), st("s_dk"), st("s_dv"), st("s_ik"), st("s_ck"), st("s_cv"), st("s_conv"))
```

```python
import functools
import math

import numpy as np
import jax
import jax.numpy as jnp
from jax import lax
from jax.experimental import pallas as pl
from jax.experimental.pallas import tpu as pltpu

CHUNK = 64
EPS = 1e-6
ROPE_THETA = 10000.0
GLA_RANK = 16
GLA_TAU = 16.0
DSA_HD = 128
IDX_HD = 128
IDX_TOPK = 256
DIFF_HD = 128
CONV_W = 3

LANE = 128
GLA_SUB = 16
DSA_TQ, DSA_KB = 128, 512
DIFF_TQ, DIFF_TK = 512, 1024
VMEM_LIMIT = 60 * 1024 * 1024
MM_VMEM_BUDGET = 52 * 1024 * 1024
NEG_INIT = -1e30
NEG_MASK = -3e38
INT_MIN = -2147483648
LOG2E = 1.4426950408889634

f32 = jnp.float32
bf16 = jnp.bfloat16
NT = (((1,), (1,)), ((), ()))
TN = (((0,), (0,)), ((), ()))


def _cparams(sem):
    return pltpu.CompilerParams(dimension_semantics=sem, vmem_limit_bytes=VMEM_LIMIT)


def _pick(dim, cands):
    for c in cands:
        if c <= dim and dim % c == 0:
            return c
    return dim


def _rmsnorm_kernel(x_ref, g_ref, o_ref):
    x = x_ref[...]
    y = x * lax.rsqrt(jnp.mean(x * x, axis=-1, keepdims=True) + EPS)
    o_ref[...] = (y * g_ref[...]).astype(o_ref.dtype)


def rmsnorm(x, g, out_dtype):
    M, D = x.shape
    tm = _pick(M, (256, 128, 64, 32, 16, 8))
    return pl.pallas_call(
        _rmsnorm_kernel,
        grid=(M // tm,),
        in_specs=[pl.BlockSpec((tm, D), lambda i: (i, 0)), pl.BlockSpec((1, D), lambda i: (0, 0))],
        out_specs=pl.BlockSpec((tm, D), lambda i: (i, 0)),
        out_shape=jax.ShapeDtypeStruct((M, D), out_dtype),
        compiler_params=_cparams(("parallel",)),
        name="rmsnorm",
    )(x, g.reshape(1, D).astype(f32))


def _mm_kernel(*refs, n_x, has_res):
    x_refs, w_ref = refs[:n_x], refs[n_x]
    o_ref = refs[-1]
    acc = None
    k0 = 0
    for x_ref in x_refs:
        kw = x_ref.shape[1]
        part = jnp.dot(x_ref[...], w_ref[k0:k0 + kw, :], preferred_element_type=f32)
        acc = part if acc is None else acc + part
        k0 += kw
    if has_res:
        acc = refs[n_x + 1][...] + acc
    o_ref[...] = acc.astype(o_ref.dtype)


def matmul(xs, w, out_dtype, res=None, name="matmul"):
    xs = tuple(xs) if isinstance(xs, (tuple, list)) else (xs,)
    M = xs[0].shape[0]
    K, N = w.shape
    ob = jnp.dtype(out_dtype).itemsize
    best = None
    for tm in (1024, 768, 512, 384, 256, 128, 64, 32, 16, 8):
        if M % tm:
            continue
        for tn in (1024, 768, 512, 384, 256, 128):
            if N % tn:
                continue
            vm = 2 * (tm * K * 2 + K * tn * 2 + tm * tn * ob + (tm * tn * 4 if res is not None else 0))
            if vm > MM_VMEM_BUDGET:
                continue
            score = (tm * tn) / (tm + tn)
            if best is None or score > best[0]:
                best = (score, tm, tn)
    _, tm, tn = best
    in_specs = [pl.BlockSpec((tm, x.shape[1]), lambda i, j: (i, 0)) for x in xs]
    in_specs.append(pl.BlockSpec((K, tn), lambda i, j: (0, j)))
    args = list(xs) + [w]
    if res is not None:
        in_specs.append(pl.BlockSpec((tm, tn), lambda i, j: (i, j)))
        args.append(res)
    return pl.pallas_call(
        functools.partial(_mm_kernel, n_x=len(xs), has_res=res is not None),
        grid=(M // tm, N // tn),
        in_specs=in_specs,
        out_specs=pl.BlockSpec((tm, tn), lambda i, j: (i, j)),
        out_shape=jax.ShapeDtypeStruct((M, N), out_dtype),
        compiler_params=_cparams(("parallel", "arbitrary")),
        name=name,
    )(*args)


def _cols_kernel(x_ref, c_ref, s_ref, *o_refs, gpb, rotate):
    for g in range(gpb):
        y = x_ref[:, g * LANE:(g + 1) * LANE]
        if rotate:
            y = y * c_ref[...] + pltpu.roll(y, LANE // 2, 1) * s_ref[...]
        for o_ref in o_refs:
            o_ref[:, g * LANE:(g + 1) * LANE] = y.astype(o_ref.dtype)


def take_cols(z, col0, ngroups, tables, rotate, dtypes):
    M = z.shape[0]
    tm = _pick(M, (256, 128, 64, 32, 16, 8))
    g0 = col0 // LANE
    gpb = next(g for g in (8, 4, 2, 1) if ngroups % g == 0 and g0 % g == 0)
    w = gpb * LANE
    spec_o = pl.BlockSpec((tm, w), lambda i, j: (i, j))
    tab = pl.BlockSpec((tm, LANE), lambda i, j: (i, 0))
    outs = pl.pallas_call(
        functools.partial(_cols_kernel, gpb=gpb, rotate=rotate),
        grid=(M // tm, ngroups // gpb),
        in_specs=[pl.BlockSpec((tm, w), lambda i, j: (i, g0 // gpb + j)), tab, tab],
        out_specs=[spec_o] * len(dtypes),
        out_shape=[jax.ShapeDtypeStruct((M, ngroups * LANE), dt) for dt in dtypes],
        compiler_params=_cparams(("parallel", "arbitrary")),
        name="take_cols",
    )(z, *tables)
    return outs


def rope_tables(pos):
    half = LANE // 2
    inv = 1.0 / (ROPE_THETA ** (jnp.arange(half, dtype=f32) * (2.0 / LANE)))
    ang = pos.astype(f32)[:, None] * inv[None, :]
    cos, sin = jnp.cos(ang), jnp.sin(ang)
    return jnp.concatenate([cos, cos], axis=-1), jnp.concatenate([-sin, sin], axis=-1)


def _gla_kernel(q_ref, k_ref, v_ref, gg_ref, misc_ref, wa2_ref, ba_ref, gn_ref, s0_ref,
                o_ref, sout_ref, st_ref, *, c, n_inner, dk, dv):
    step = pl.program_id(2)

    @pl.when(step == 0)
    def _():
        st_ref[...] = s0_ref[0, 0].T

    nsub = c // GLA_SUB
    row_c = lax.broadcasted_iota(jnp.int32, (c, dk), 0)
    row_a = lax.broadcasted_iota(jnp.int32, (c, c), 0)
    lane_a = lax.broadcasted_iota(jnp.int32, (c, c), 1)
    sub_a = row_a % GLA_SUB

    def chunk(ci, carry):
        r = pl.multiple_of(ci * c, c)
        q = q_ref[pl.ds(r, c), :] * (dk ** -0.5)
        k = k_ref[pl.ds(r, c), :]
        v = v_ref[pl.ds(r, c), :]
        ga = misc_ref[pl.ds(r, c), :][:, :GLA_RANK]
        x = jnp.dot(ga.astype(bf16), wa2_ref[...], preferred_element_type=f32) + ba_ref[...]
        loga = (jnp.minimum(x, 0.0) - jnp.log1p(jnp.exp(-jnp.abs(x)))) / GLA_TAU
        b = loga
        d = 1
        while d < c:
            b = b + jnp.where(row_c >= d, pltpu.roll(b, d, 0), 0.0)
            d *= 2
        st = st_ref[...]
        vb = v.astype(bf16)
        o_inter = lax.dot_general((q * jnp.exp(b)).astype(bf16), st.astype(bf16), NT,
                                  preferred_element_type=f32)
        att = jnp.zeros((c, c), f32)
        for delta in range(GLA_SUB):
            if delta == 0:
                w = q * k
            else:
                w = q * pltpu.roll(k, delta, 0) * jnp.exp(jnp.minimum(b - pltpu.roll(b, delta, 0), 0.0))
            col = jnp.sum(w, axis=-1, keepdims=True)
            att = jnp.where(jnp.logical_and(lane_a == row_a - delta, sub_a >= delta), col, att)
        if nsub > 1:
            offs = [jnp.zeros((GLA_SUB, c), f32)]
            for i in range(1, nsub):
                r0 = i * GLA_SUB
                bref = b[r0 - 1:r0]
                qq = q[r0:r0 + GLA_SUB] * jnp.exp(b[r0:r0 + GLA_SUB] - bref)
                kk = jnp.where(row_c < r0, k * jnp.exp(jnp.minimum(bref - b, 0.0)), 0.0)
                offs.append(lax.dot_general(qq.astype(bf16), kk.astype(bf16), NT, preferred_element_type=f32))
            att = att + jnp.concatenate(offs, axis=0)
        o = o_inter + jnp.dot(att.astype(bf16), vb, preferred_element_type=f32)
        b_last = b[c - 1:c]
        kd = k * jnp.exp(b_last - b)
        st_ref[...] = jnp.exp(b_last) * st + lax.dot_general(vb, kd.astype(bf16), TN, preferred_element_type=f32)
        y = o * lax.rsqrt(jnp.mean(o * o, axis=-1, keepdims=True) + EPS) * gn_ref[...]
        gg = gg_ref[pl.ds(r, c), :]
        y = y * (gg * (1.0 / (1.0 + jnp.exp(-gg))))
        o_ref[pl.ds(r, c), :] = y.astype(o_ref.dtype)
        return carry

    lax.fori_loop(0, n_inner, chunk, 0)

    @pl.when(step == pl.num_programs(2) - 1)
    def _():
        sout_ref[0, 0] = st_ref[...].T


def gla(z, s0, sb0, wa2, ba, gn, *, n_seq, T, heads, dk, dv, cols):
    c = min(CHUNK, T)
    tb = max(_pick(T, (256, 128, 64, 32, 16)), c)
    n_steps = T // tb
    cq, ck, cv, cg, cm = cols
    in_specs = [
        pl.BlockSpec((tb, dk), lambda b, h, s: (b * n_steps + s, cq // dk + h)),
        pl.BlockSpec((tb, dk), lambda b, h, s: (b * n_steps + s, ck // dk + h)),
        pl.BlockSpec((tb, dv), lambda b, h, s: (b * n_steps + s, cv // dv + h)),
        pl.BlockSpec((tb, dv), lambda b, h, s: (b * n_steps + s, cg // dv + h)),
        pl.BlockSpec((tb, LANE), lambda b, h, s: (b * n_steps + s, cm // LANE)),
        pl.BlockSpec((GLA_RANK, dk), lambda b, h, s: (0, h)),
        pl.BlockSpec((1, dk), lambda b, h, s: (0, h)),
        pl.BlockSpec((1, dv), lambda b, h, s: (0, 0)),
        pl.BlockSpec((1, 1, dk, dv), lambda b, h, s: (sb0 + b, h, 0, 0)),
    ]
    out_specs = [
        pl.BlockSpec((tb, dv), lambda b, h, s: (b * n_steps + s, h)),
        pl.BlockSpec((1, 1, dk, dv), lambda b, h, s: (b, h, 0, 0)),
    ]
    return pl.pallas_call(
        functools.partial(_gla_kernel, c=c, n_inner=tb // c, dk=dk, dv=dv),
        grid=(n_seq, heads, n_steps),
        in_specs=in_specs,
        out_specs=out_specs,
        out_shape=[jax.ShapeDtypeStruct((n_seq * T, heads * dv), bf16),
                   jax.ShapeDtypeStruct((n_seq, heads, dk, dv), f32)],
        scratch_shapes=[pltpu.VMEM((dv, dk), f32)],
        compiler_params=_cparams(("parallel", "parallel", "arbitrary")),
        name="gla",
    )(z, z, z, z, z, wa2, ba, gn, s0)


def _dsa_kernel(dq_ref, iq_ref, misc_ref, k_ref, v_ref, ik_ref, o_ref, key_s, m_s, l_s, acc_s,
                *, tq, kb, s_valid, pos0, topk, heads, kv_heads, idx_heads):
    j = pl.program_id(1)
    qpos0 = pos0 + j * tq
    vis_end = jnp.minimum(((qpos0 + tq - 1) // CHUNK + 1) * CHUNK, s_valid)
    nkb = (vis_end + kb - 1) // kb
    qchunk = (qpos0 + lax.broadcasted_iota(jnp.int32, (tq, 1), 0)) // CHUNK
    w = misc_ref[...][:, GLA_RANK:GLA_RANK + idx_heads]
    group = heads // kv_heads
    scale = DSA_HD ** -0.5

    def score_body(b, carry):
        c0 = pl.multiple_of(b * kb, kb)
        ikb = ik_ref[0, pl.ds(c0, kb), :]
        acc = jnp.zeros((tq, kb), f32)
        for h in range(idx_heads):
            s = lax.dot_general(iq_ref[:, h * IDX_HD:(h + 1) * IDX_HD], ikb, NT, preferred_element_type=f32)
            acc = acc + w[:, h:h + 1] * jnp.maximum(s, 0.0)
        kpos = c0 + lax.broadcasted_iota(jnp.int32, (1, kb), 1)
        vis = jnp.logical_and(kpos // CHUNK <= qchunk, kpos < s_valid)
        bits = pltpu.bitcast(acc, jnp.int32)
        key = bits ^ ((bits >> 31) & 0x7FFFFFFF)
        key_s[:, pl.ds(c0, kb)] = jnp.where(vis, key, INT_MIN)
        return carry

    lax.fori_loop(0, nkb, score_body, 0)

    one, zero = jnp.int32(1), jnp.int32(0)
    izeros = jnp.zeros((tq, LANE), jnp.int32)
    lane_i = lax.broadcasted_iota(jnp.int32, (1, LANE), 1)

    def count(pred):
        def body(b, cacc):
            c0 = pl.multiple_of(b * kb, kb)
            blk = key_s[:, pl.ds(c0, kb)]
            for t in range(kb // LANE):
                cacc = cacc + jnp.where(pred(blk[:, t * LANE:(t + 1) * LANE], c0 + t * LANE + lane_i), one, zero)
            return cacc
        return jnp.sum(lax.fori_loop(0, nkb, body, izeros), axis=-1, keepdims=True)

    def bit_body(i, carry):
        t_u, c_at = carry
        cand_u = t_u | lax.shift_left(one, 31 - i)
        cand = cand_u ^ INT_MIN
        cnt = count(lambda kt, it: kt >= cand)
        ok = cnt >= topk
        return jnp.where(ok, cand_u, t_u), jnp.where(ok, cnt, c_at)

    t_u, c_at = lax.fori_loop(0, 32, bit_body, (izeros, izeros))
    thr = jnp.maximum(t_u ^ INT_MIN, INT_MIN + 1)

    @pl.when(jnp.max(c_at) > topk)
    def _():
        need = topk - count(lambda kt, it: kt > thr)
        nbits = int(s_valid).bit_length()

        def idx_body(i, lo):
            cand = lo | lax.shift_left(one, nbits - 1 - i)
            c = count(lambda kt, it: jnp.logical_and(kt == thr, it < cand))
            return jnp.where(c < need, cand, lo)

        lo = lax.fori_loop(0, nbits, idx_body, izeros)

        def drop_body(b, carry):
            c0 = pl.multiple_of(b * kb, kb)
            blk = key_s[:, pl.ds(c0, kb)]
            tiles = []
            for t in range(kb // LANE):
                kt = blk[:, t * LANE:(t + 1) * LANE]
                drop = jnp.logical_and(kt == thr, c0 + t * LANE + lane_i > lo)
                tiles.append(jnp.where(drop, INT_MIN, kt))
            key_s[:, pl.ds(c0, kb)] = jnp.concatenate(tiles, axis=1)
            return carry

        lax.fori_loop(0, nkb, drop_body, 0)

    for g in range(kv_heads):
        qg = jnp.concatenate([dq_ref[:, (g * group + i) * DSA_HD:(g * group + i + 1) * DSA_HD]
                              for i in range(group)], axis=0)
        m_s[...] = jnp.full(m_s.shape, NEG_INIT, f32)
        l_s[...] = jnp.zeros(l_s.shape, f32)
        acc_s[...] = jnp.zeros(acc_s.shape, f32)

        def kv_body(b, carry, g=g, qg=qg):
            c0 = pl.multiple_of(b * kb, kb)
            kblk = k_ref[0, pl.ds(c0, kb), g * DSA_HD:(g + 1) * DSA_HD]
            vblk = v_ref[0, pl.ds(c0, kb), g * DSA_HD:(g + 1) * DSA_HD]
            keys = key_s[:, pl.ds(c0, kb)]
            bias = jnp.concatenate(
                [jnp.where(keys[:, t * LANE:(t + 1) * LANE] >= thr, 0.0, NEG_MASK) for t in range(kb // LANE)],
                axis=1)
            bias = jnp.concatenate([bias] * group, axis=0)
            s = lax.dot_general(qg, kblk, NT, preferred_element_type=f32) * scale + bias
            m_old = m_s[...]
            m_new = jnp.maximum(m_old, jnp.max(s, axis=-1, keepdims=True))
            p = jnp.exp(s - m_new)
            alpha = jnp.exp(m_old - m_new)
            l_s[...] = alpha * l_s[...] + jnp.sum(p, axis=-1, keepdims=True)
            acc_s[...] = alpha * acc_s[...] + jnp.dot(p.astype(bf16), vblk, preferred_element_type=f32)
            m_s[...] = m_new
            return carry

        lax.fori_loop(0, nkb, kv_body, 0)
        out = acc_s[...] / l_s[...]
        for i in range(group):
            hh = g * group + i
            o_ref[:, hh * DSA_HD:(hh + 1) * DSA_HD] = out[i * tq:(i + 1) * tq].astype(o_ref.dtype)


def dsa(dq, iq, z, k_all, v_all, ik_all, *, n_seq, T, s_valid, pos0, heads, kv_heads, idx_heads, misc_col, tq, kb):
    s_pad = k_all.shape[1]
    nq = T // tq
    topk = min(IDX_TOPK, s_valid // 4)
    group = heads // kv_heads
    kern = functools.partial(_dsa_kernel, tq=tq, kb=kb, s_valid=s_valid, pos0=pos0, topk=topk, heads=heads,
                             kv_heads=kv_heads, idx_heads=idx_heads)
    return pl.pallas_call(
        kern,
        grid=(n_seq, nq),
        in_specs=[
            pl.BlockSpec((tq, heads * DSA_HD), lambda b, j: (b * nq + j, 0)),
            pl.BlockSpec((tq, idx_heads * IDX_HD), lambda b, j: (b * nq + j, 0)),
            pl.BlockSpec((tq, LANE), lambda b, j: (b * nq + j, misc_col // LANE)),
            pl.BlockSpec((1, s_pad, kv_heads * DSA_HD), lambda b, j: (b, 0, 0)),
            pl.BlockSpec((1, s_pad, kv_heads * DSA_HD), lambda b, j: (b, 0, 0)),
            pl.BlockSpec((1, s_pad, IDX_HD), lambda b, j: (b, 0, 0)),
        ],
        out_specs=pl.BlockSpec((tq, heads * DSA_HD), lambda b, j: (b * nq + j, 0)),
        out_shape=jax.ShapeDtypeStruct((n_seq * T, heads * DSA_HD), bf16),
        scratch_shapes=[pltpu.VMEM((tq, s_pad), jnp.int32), pltpu.VMEM((group * tq, 1), f32),
                        pltpu.VMEM((group * tq, 1), f32), pltpu.VMEM((group * tq, DSA_HD), f32)],
        compiler_params=_cparams(("parallel", "arbitrary")),
        name="dsa",
    )(dq, iq, z, k_all, v_all, ik_all)


def _diff_lambda(lq1, lk1, lq2, lk2, lam_init):
    return (jnp.exp(jnp.sum(lq1[...] * lk1[...], axis=-1, keepdims=True))
            - jnp.exp(jnp.sum(lq2[...] * lk2[...], axis=-1, keepdims=True)) + lam_init)


def _diff_finish(o, gn_ref, lam_init):
    y = o * lax.rsqrt(jnp.mean(o * o, axis=-1, keepdims=True) + EPS) * gn_ref[...]
    return y * (1.0 - lam_init)


def _diff_prompt_kernel(qt_ref, kt_ref, q_ref, k_ref, v_ref, lq1, lk1, lq2, lk2, gn_ref, o_ref, m_s, l_s, acc_s,
                        *, tq, tk, lam_init):
    p_id = pl.program_id(2)
    qi = qt_ref[p_id]
    ki = kt_ref[p_id]
    last_k = ((qi + 1) * tq - 1) // tk
    c2 = DIFF_HD ** -0.5 * LOG2E

    @pl.when(ki == 0)
    def _():
        m_s[...] = jnp.full(m_s.shape, NEG_INIT, f32)
        l_s[...] = jnp.zeros(l_s.shape, f32)
        acc_s[...] = jnp.zeros(acc_s.shape, f32)

    def update(masked):
        vb = v_ref[...]
        if masked:
            qc = (qi * tq + lax.broadcasted_iota(jnp.int32, (tq, 1), 0)) // CHUNK
            kc = (ki * tk + lax.broadcasted_iota(jnp.int32, (1, tk), 1)) // CHUNK
            bias = jnp.where(kc <= qc, 0.0, NEG_MASK)
        for c in range(2):
            s = lax.dot_general(q_ref[:, c * DIFF_HD:(c + 1) * DIFF_HD], k_ref[:, c * DIFF_HD:(c + 1) * DIFF_HD],
                                NT, preferred_element_type=f32) * c2
            if masked:
                s = s + bias
            m_old = m_s[c]
            m_new = jnp.maximum(m_old, jnp.max(s, axis=-1, keepdims=True))
            p = jnp.exp2(s - m_new)
            alpha = jnp.exp2(m_old - m_new)
            l_s[c] = alpha * l_s[c] + jnp.sum(p, axis=-1, keepdims=True)
            acc_s[c] = alpha * acc_s[c] + jnp.dot(p.astype(bf16), vb, preferred_element_type=f32)
            m_s[c] = m_new

    needs_mask = (ki + 1) * tk > qi * tq + CHUNK

    @pl.when(needs_mask)
    def _():
        update(True)

    @pl.when(jnp.logical_not(needs_mask))
    def _():
        update(False)

    @pl.when(ki == last_k)
    def _():
        lam = _diff_lambda(lq1, lk1, lq2, lk2, lam_init)
        o = acc_s[0] / l_s[0] - lam * (acc_s[1] / l_s[1])
        o_ref[...] = _diff_finish(o, gn_ref, lam_init).astype(o_ref.dtype)


def diff_prompt(q, k, v, lams, gn, *, n_seq, T, heads, lam_init):
    hw = 2 * DIFF_HD
    tq = _pick(T, (DIFF_TQ, 256, 128))
    tk = _pick(T, (DIFF_TK, 512, 256, 128))
    nq, nk = T // tq, T // tk
    pairs = [(qi, ki) for qi in range(nq) for ki in range(((qi + 1) * tq - 1) // tk + 1)]
    qt = jnp.asarray(np.array([p[0] for p in pairs], np.int32))
    kt = jnp.asarray(np.array([p[1] for p in pairs], np.int32))
    vec = pl.BlockSpec((1, DIFF_HD), lambda b, h, p, qt, kt: (0, 0))
    kv = pl.BlockSpec((tk, hw), lambda b, h, p, qt, kt: (b * nk + kt[p], h))
    qo = pl.BlockSpec((tq, hw), lambda b, h, p, qt, kt: (b * nq + qt[p], h))
    return pl.pallas_call(
        functools.partial(_diff_prompt_kernel, tq=tq, tk=tk, lam_init=lam_init),
        grid_spec=pltpu.PrefetchScalarGridSpec(
            num_scalar_prefetch=2,
            grid=(n_seq, heads, len(pairs)),
            in_specs=[qo, kv, kv, vec, vec, vec, vec, pl.BlockSpec((1, hw), lambda b, h, p, qt, kt: (0, 0))],
            out_specs=qo,
            scratch_shapes=[pltpu.VMEM((2, tq, 1), f32), pltpu.VMEM((2, tq, 1), f32), pltpu.VMEM((2, tq, hw), f32)],
        ),
        out_shape=jax.ShapeDtypeStruct((n_seq * T, heads * hw), bf16),
        compiler_params=_cparams(("parallel", "parallel", "arbitrary")),
        name="diff_prompt",
    )(qt, kt, q, k, v, *lams, gn)


def _diff_sample_kernel(q_ref, kn_ref, vn_ref, kp_ref, vp_ref, lq1, lk1, lq2, lk2, gn_ref, o_ref, *, lam_init):
    scale = DIFF_HD ** -0.5
    lam = _diff_lambda(lq1, lk1, lq2, lk2, lam_init)
    kp = kp_ref[0].astype(bf16)
    a_p = None
    a_n = None
    for c in range(2):
        qc = q_ref[:, c * DIFF_HD:(c + 1) * DIFF_HD]
        sp = lax.dot_general(qc, kp[:, c * DIFF_HD:(c + 1) * DIFF_HD], NT, preferred_element_type=f32) * scale
        sn = lax.dot_general(qc, kn_ref[:, c * DIFF_HD:(c + 1) * DIFF_HD], NT, preferred_element_type=f32) * scale
        m = jnp.maximum(jnp.max(sp, axis=-1, keepdims=True), jnp.max(sn, axis=-1, keepdims=True))
        ep = jnp.exp(sp - m)
        en = jnp.exp(sn - m)
        inv = 1.0 / (jnp.sum(ep, axis=-1, keepdims=True) + jnp.sum(en, axis=-1, keepdims=True))
        if c == 0:
            a_p, a_n = ep * inv, en * inv
        else:
            a_p, a_n = a_p - lam * (ep * inv), a_n - lam * (en * inv)
    o = (jnp.dot(a_p.astype(bf16), vp_ref[0].astype(bf16), preferred_element_type=f32)
         + jnp.dot(a_n.astype(bf16), vn_ref[...], preferred_element_type=f32))
    o_ref[...] = _diff_finish(o, gn_ref, lam_init).astype(o_ref.dtype)


def diff_sample(q, k, v, k_past, v_past, pb0, lams, gn, *, n_seq, T, heads, lam_init):
    hw = 2 * DIFF_HD
    past = k_past.shape[1]
    assert (past + T - 1) // CHUNK == past // CHUNK and past % CHUNK == 0
    new = pl.BlockSpec((T, hw), lambda b, h: (b, h))
    old = pl.BlockSpec((1, past, hw), lambda b, h: (pb0 + b, 0, h))
    vec = pl.BlockSpec((1, DIFF_HD), lambda b, h: (0, 0))
    return pl.pallas_call(
        functools.partial(_diff_sample_kernel, lam_init=lam_init),
        grid=(n_seq, heads),
        in_specs=[new, new, new, old, old, vec, vec, vec, vec, pl.BlockSpec((1, hw), lambda b, h: (0, 0))],
        out_specs=pl.BlockSpec((T, hw), lambda b, h: (b, h)),
        out_shape=jax.ShapeDtypeStruct((n_seq * T, heads * hw), bf16),
        compiler_params=_cparams(("parallel", "parallel")),
        name="diff_sample",
    )(q, k, v, k_past, v_past, *lams, gn)


def _conv_gate_kernel(ua_ref, ug_ref, pa_ref, pg_ref, wa_ref, wg_ref, ba_ref, bg_ref, o_ref):
    def conv(u_ref, p_ref, w_ref, b_ref):
        u = u_ref[...].astype(f32)
        row = lax.broadcasted_iota(jnp.int32, u.shape, 0)
        p0 = p_ref[0, 0:1, :]
        p1 = p_ref[0, 1:2, :]
        u1 = jnp.where(row == 0, p1, pltpu.roll(u, 1, 0))
        u2 = jnp.where(row == 0, p0, jnp.where(row == 1, p1, pltpu.roll(u, 2, 0)))
        return b_ref[...] + u2 * w_ref[0:1, :] + u1 * w_ref[1:2, :] + u * w_ref[2:3, :]

    a = conv(ua_ref, pa_ref, wa_ref, ba_ref)
    g = conv(ug_ref, pg_ref, wg_ref, bg_ref)
    o_ref[...] = (g * (1.0 / (1.0 + jnp.exp(-g))) * a).astype(o_ref.dtype)


def conv_gate(u, prev, pb0, cw, cb, *, n_seq, T, dff):
    tn = LANE if T > 256 else _pick(dff, (5504, 1024, 512, 256, 128))
    nb = dff // tn
    cb2 = cb.reshape(1, 2 * dff)
    return pl.pallas_call(
        _conv_gate_kernel,
        grid=(n_seq, nb),
        in_specs=[pl.BlockSpec((T, tn), lambda b, j: (b, j)),
                  pl.BlockSpec((T, tn), lambda b, j: (b, nb + j)),
                  pl.BlockSpec((1, CONV_W - 1, tn), lambda b, j: (pb0 + b, 0, j)),
                  pl.BlockSpec((1, CONV_W - 1, tn), lambda b, j: (pb0 + b, 0, nb + j)),
                  pl.BlockSpec((CONV_W, tn), lambda b, j: (0, j)),
                  pl.BlockSpec((CONV_W, tn), lambda b, j: (0, nb + j)),
                  pl.BlockSpec((1, tn), lambda b, j: (0, j)),
                  pl.BlockSpec((1, tn), lambda b, j: (0, nb + j))],
        out_specs=pl.BlockSpec((T, tn), lambda b, j: (b, j)),
        out_shape=jax.ShapeDtypeStruct((n_seq * T, dff), bf16),
        compiler_params=_cparams(("parallel", "parallel")),
        name="conv_gate",
    )(u, u, prev, prev, cw, cw, cb2, cb2)


class _Stream:
    def __init__(self, x, pos0, past):
        self.n_seq, self.T, d = x.shape
        self.h = x.reshape(self.n_seq * self.T, d)
        self.pos0 = pos0
        self.past = past
        pos = jnp.tile(pos0 + jnp.arange(self.T, dtype=jnp.int32), self.n_seq)
        self.tables = rope_tables(pos)


def kernel(x_prompt, x_sample, cache_gla_state, cache_dsa_k, cache_dsa_v, cache_idx_k, cache_diff_k, cache_diff_v, state_ffn_conv, norm_mix, norm_ffn, norm_final, w_in_ab, w_gla_a2, b_gla_a, gla_norm, w_out_ab, w_in_c, lambda_q1, lambda_k1, lambda_q2, lambda_k2, diff_norm, w_out_c, w_up, conv_w, conv_b, w_down):
    B, S, D = x_prompt.shape
    DB, DS, _ = x_sample.shape
    depth = norm_mix.shape[0]
    n_ab, _, gh, gdk, gdv = cache_gla_state.shape
    n_c = cache_diff_k.shape[0]
    past = cache_dsa_k.shape[2]
    kvh = cache_dsa_k.shape[3]
    kvw = kvh * DSA_HD
    dh = w_out_ab.shape[1] - gh * gdv
    dsa_heads = dh // DSA_HD
    idx_heads = (w_in_ab.shape[2] - (2 * gh * gdk + 2 * gh * gdv + GLA_RANK + dh + 2 * kvw + IDX_HD)) // (IDX_HD + 1)
    diff_heads = cache_diff_k.shape[3]
    dff = w_down.shape[1]
    cw_ = diff_heads * 2 * DIFF_HD

    streams = (_Stream(x_prompt, 0, 0), _Stream(x_sample, past, past))
    names = ("gla", "dk", "dv", "ik", "ck", "cv", "conv")
    outs = [{k: [] for k in names} for _ in streams]

    c_gla = cache_gla_state.reshape(n_ab * DB, gh, gdk, gdv)
    c_dk = cache_dsa_k.reshape(n_ab, DB, past, kvw)
    c_dv = cache_dsa_v.reshape(n_ab, DB, past, kvw)
    c_ik = cache_idx_k.reshape(n_ab, DB, past, IDX_HD)
    c_ck = cache_diff_k.reshape(n_c * DB, past, cw_)
    c_cv = cache_diff_v.reshape(n_c * DB, past, cw_)
    c_conv = state_ffn_conv.reshape(depth * DB, CONV_W - 1, 2 * dff)
    zero_state = jnp.zeros((B, gh, gdk, gdv), f32)
    zero_conv = jnp.zeros((B, CONV_W - 1, 2 * dff), f32)

    for l in range(depth):
        i = l // 2
        if l % 2 == 0:
            sizes = (gh * gdk, gh * gdk, gh * gdv, gh * gdv, GLA_RANK, dh, kvw, kvw, idx_heads * IDX_HD, IDX_HD,
                     idx_heads)
            offs = np.concatenate([[0], np.cumsum(sizes)]).tolist()
            wi = w_in_ab[i]
            seg = lambda n: wi[:, offs[n]:offs[n + 1]]
            pad = LANE - GLA_RANK - idx_heads
            w_perm = jnp.concatenate([seg(0), seg(1), seg(2), seg(3), seg(5), seg(6), seg(7), seg(8), seg(9),
                                      seg(4), seg(10), jnp.zeros((D, pad), wi.dtype)], axis=1).astype(bf16)
            c_gq = 0
            c_gk = c_gq + gh * gdk
            c_gv = c_gk + gh * gdk
            c_gg = c_gv + gh * gdv
            c_dq = c_gg + gh * gdv
            c_dk_ = c_dq + dh
            c_dv_ = c_dk_ + kvw
            c_iq = c_dv_ + kvw
            c_ik_ = c_iq + idx_heads * IDX_HD
            c_misc = c_ik_ + IDX_HD
            wa2 = w_gla_a2[i].astype(bf16)
            ba = b_gla_a[i].reshape(1, gh * gdk)
            gn = gla_norm[i].reshape(1, gdv)
            w_out = w_out_ab[i].astype(bf16)
            for si, st in enumerate(streams):
                hn = rmsnorm(st.h, norm_mix[l], bf16)
                z = matmul(hn, w_perm, f32, name="in_ab")
                (dq_b,) = take_cols(z, c_dq, dsa_heads, st.tables, True, (bf16,))
                dk_f, dk_b = take_cols(z, c_dk_, kvh, st.tables, True, (f32, bf16))
                dv_f, dv_b = take_cols(z, c_dv_, kvh, st.tables, False, (f32, bf16))
                (iq_b,) = take_cols(z, c_iq, idx_heads, st.tables, True, (bf16,))
                ik_f, ik_b = take_cols(z, c_ik_, 1, st.tables, True, (f32, bf16))
                if st.past:
                    s0, sb0 = c_gla, i * DB
                else:
                    s0, sb0 = zero_state, 0
                go, gst = gla(z, s0, sb0, wa2, ba, gn, n_seq=st.n_seq, T=st.T, heads=gh, dk=gdk, dv=gdv,
                              cols=(c_gq, c_gk, c_gv, c_gg, c_misc))
                s_valid = st.past + st.T
                if st.past:
                    tq, kb = st.T, DSA_KB
                    s_pad = -(-s_valid // kb) * kb

                    def cat(cache, new, width):
                        return jnp.concatenate([cache.astype(bf16), new.reshape(st.n_seq, st.T, width),
                                                jnp.zeros((st.n_seq, s_pad - s_valid, width), bf16)], axis=1)

                    k_all, v_all, ik_all = cat(c_dk[i], dk_b, kvw), cat(c_dv[i], dv_b, kvw), cat(c_ik[i], ik_b, IDX_HD)
                else:
                    tq, kb = _pick(st.T, (DSA_TQ,)), _pick(st.T, (DSA_KB, 256, 128))
                    k_all = dk_b.reshape(st.n_seq, st.T, kvw)
                    v_all = dv_b.reshape(st.n_seq, st.T, kvw)
                    ik_all = ik_b.reshape(st.n_seq, st.T, IDX_HD)
                do = dsa(dq_b, iq_b, z, k_all, v_all, ik_all, n_seq=st.n_seq, T=st.T, s_valid=s_valid, pos0=st.pos0,
                         heads=dsa_heads, kv_heads=kvh, idx_heads=idx_heads, misc_col=c_misc, tq=tq, kb=kb)
                st.h = matmul((go, do), w_out, f32, res=st.h, name="out_ab")
                o = outs[si]
                o["gla"].append(gst)
                o["dk"].append(dk_f.reshape(st.n_seq, st.T, kvh, DSA_HD))
                o["dv"].append(dv_f.reshape(st.n_seq, st.T, kvh, DSA_HD))
                o["ik"].append(ik_f.reshape(st.n_seq, st.T, IDX_HD))
        else:
            lam_init = 0.8 - 0.6 * math.exp(-0.3 * l)
            w_in = w_in_c[i].astype(bf16)
            w_out = w_out_c[i].astype(bf16)
            lams = tuple(a[i].reshape(1, DIFF_HD) for a in (lambda_q1, lambda_k1, lambda_q2, lambda_k2))
            gn = diff_norm[i].reshape(1, 2 * DIFF_HD)
            for si, st in enumerate(streams):
                hn = rmsnorm(st.h, norm_mix[l], bf16)
                z = matmul(hn, w_in, f32, name="in_c")
                (q_b,) = take_cols(z, 0, cw_ // LANE, st.tables, True, (bf16,))
                k_f, k_b = take_cols(z, cw_, cw_ // LANE, st.tables, True, (f32, bf16))
                v_f, v_b = take_cols(z, 2 * cw_, cw_ // LANE, st.tables, False, (f32, bf16))
                if st.past:
                    at = diff_sample(q_b, k_b, v_b, c_ck, c_cv, i * DB, lams, gn, n_seq=st.n_seq, T=st.T,
                                     heads=diff_heads, lam_init=lam_init)
                else:
                    at = diff_prompt(q_b, k_b, v_b, lams, gn, n_seq=st.n_seq, T=st.T, heads=diff_heads,
                                     lam_init=lam_init)
                st.h = matmul(at, w_out, f32, res=st.h, name="out_c")
                o = outs[si]
                o["ck"].append(k_f.reshape(st.n_seq, st.T, diff_heads, 2, DIFF_HD))
                o["cv"].append(v_f.reshape(st.n_seq, st.T, diff_heads, 2 * DIFF_HD))

        wu = w_up[l].astype(bf16)
        wd = w_down[l].astype(bf16)
        for si, st in enumerate(streams):
            hn = rmsnorm(st.h, norm_ffn[l], bf16)
            u = matmul(hn, wu, bf16, name="ffn_up")
            prev, pb0 = (c_conv, l * DB) if st.past else (zero_conv, 0)
            g = conv_gate(u, prev, pb0, conv_w[l], conv_b[l], n_seq=st.n_seq, T=st.T, dff=dff)
            st.h = matmul(g, wd, f32, res=st.h, name="ffn_down")
            outs[si]["conv"].append(u.reshape(st.n_seq, st.T, 2 * dff)[:, st.T - (CONV_W - 1):].astype(f32))

    ys = [rmsnorm(st.h, norm_final, f32).reshape(st.n_seq, st.T, D) for st in streams]
    tail = [jnp.stack(o[k]) for o in outs for k in names]
    return (ys[0], ys[1], *tail)
```

```python
import functools
import math

import numpy as np
import jax
import jax.numpy as jnp
from jax import lax
from jax.experimental import pallas as pl
from jax.experimental.pallas import tpu as pltpu

CHUNK = 64
EPS = 1e-6
ROPE_THETA = 10000.0
GLA_RANK = 16
GLA_TAU = 16.0
DSA_HD = 128
IDX_HD = 128
IDX_TOPK = 256
DIFF_HD = 128
CONV_W = 3

LANE = 128
GLA_SUB = 16
DSA_TQ, DSA_KB = 128, 512
DIFF_TQ, DIFF_TK = 512, 1024
FFN_TM = 1024
DIFF_SAMPLE_TK = 512
VMEM_LIMIT = 60 * 1024 * 1024
MM_VMEM_BUDGET = 52 * 1024 * 1024
NEG_INIT = -1e30
NEG_MASK = -3e38
INT_MIN = -2147483648
LOG2E = 1.4426950408889634

f32 = jnp.float32
bf16 = jnp.bfloat16
NT = (((1,), (1,)), ((), ()))
TN = (((0,), (0,)), ((), ()))


def _cparams(sem):
    return pltpu.CompilerParams(dimension_semantics=sem, vmem_limit_bytes=VMEM_LIMIT)


def _pick(dim, cands):
    for c in cands:
        if c <= dim and dim % c == 0:
            return c
    return dim


def _rmsnorm_kernel(x_ref, g_ref, o_ref):
    x = x_ref[...]
    y = x * lax.rsqrt(jnp.mean(x * x, axis=-1, keepdims=True) + EPS)
    o_ref[...] = (y * g_ref[...]).astype(o_ref.dtype)


def rmsnorm(x, g, out_dtype):
    M, D = x.shape
    tm = _pick(M, (256, 128, 64, 32, 16, 8))
    return pl.pallas_call(
        _rmsnorm_kernel,
        grid=(M // tm,),
        in_specs=[pl.BlockSpec((tm, D), lambda i: (i, 0)), pl.BlockSpec((1, D), lambda i: (0, 0))],
        out_specs=pl.BlockSpec((tm, D), lambda i: (i, 0)),
        out_shape=jax.ShapeDtypeStruct((M, D), out_dtype),
        compiler_params=_cparams(("parallel",)),
        name="rmsnorm",
    )(x, g.reshape(1, D).astype(f32))


def _mm_kernel(*refs, n_x, has_res):
    x_refs, w_ref = refs[:n_x], refs[n_x]
    o_ref = refs[-1]
    acc = None
    k0 = 0
    for x_ref in x_refs:
        kw = x_ref.shape[1]
        part = jnp.dot(x_ref[...], w_ref[k0:k0 + kw, :], preferred_element_type=f32)
        acc = part if acc is None else acc + part
        k0 += kw
    if has_res:
        acc = refs[n_x + 1][...] + acc
    o_ref[...] = acc.astype(o_ref.dtype)


def matmul(xs, w, out_dtype, res=None, name="matmul"):
    xs = tuple(xs) if isinstance(xs, (tuple, list)) else (xs,)
    M = xs[0].shape[0]
    K, N = w.shape
    ob = jnp.dtype(out_dtype).itemsize
    best = None
    for tm in (1024, 768, 512, 384, 256, 128, 64, 32, 16, 8):
        if M % tm:
            continue
        for tn in (1024, 768, 512, 384, 256, 128):
            if N % tn:
                continue
            vm = 2 * (tm * K * 2 + K * tn * 2 + tm * tn * ob + (tm * tn * 4 if res is not None else 0))
            if vm > MM_VMEM_BUDGET:
                continue
            score = (tm * tn) / (tm + tn)
            if best is None or score > best[0]:
                best = (score, tm, tn)
    _, tm, tn = best
    in_specs = [pl.BlockSpec((tm, x.shape[1]), lambda i, j: (i, 0)) for x in xs]
    in_specs.append(pl.BlockSpec((K, tn), lambda i, j: (0, j)))
    args = list(xs) + [w]
    if res is not None:
        in_specs.append(pl.BlockSpec((tm, tn), lambda i, j: (i, j)))
        args.append(res)
    return pl.pallas_call(
        functools.partial(_mm_kernel, n_x=len(xs), has_res=res is not None),
        grid=(M // tm, N // tn),
        in_specs=in_specs,
        out_specs=pl.BlockSpec((tm, tn), lambda i, j: (i, j)),
        out_shape=jax.ShapeDtypeStruct((M, N), out_dtype),
        compiler_params=_cparams(("parallel", "arbitrary")),
        name=name,
    )(*args)


def _cols_kernel(x_ref, c_ref, s_ref, *o_refs, gpb, rotate):
    for g in range(gpb):
        y = x_ref[:, g * LANE:(g + 1) * LANE]
        if rotate:
            y = y * c_ref[...] + pltpu.roll(y, LANE // 2, 1) * s_ref[...]
        for o_ref in o_refs:
            o_ref[:, g * LANE:(g + 1) * LANE] = y.astype(o_ref.dtype)


def take_cols(z, col0, ngroups, tables, rotate, dtypes):
    M = z.shape[0]
    tm = _pick(M, (256, 128, 64, 32, 16, 8))
    g0 = col0 // LANE
    gpb = next(g for g in (8, 4, 2, 1) if ngroups % g == 0 and g0 % g == 0)
    w = gpb * LANE
    spec_o = pl.BlockSpec((tm, w), lambda i, j: (i, j))
    tab = pl.BlockSpec((tm, LANE), lambda i, j: (i, 0))
    outs = pl.pallas_call(
        functools.partial(_cols_kernel, gpb=gpb, rotate=rotate),
        grid=(M // tm, ngroups // gpb),
        in_specs=[pl.BlockSpec((tm, w), lambda i, j: (i, g0 // gpb + j)), tab, tab],
        out_specs=[spec_o] * len(dtypes),
        out_shape=[jax.ShapeDtypeStruct((M, ngroups * LANE), dt) for dt in dtypes],
        compiler_params=_cparams(("parallel", "arbitrary")),
        name="take_cols",
    )(z, *tables)
    return outs


def rope_tables(pos):
    half = LANE // 2
    inv = 1.0 / (ROPE_THETA ** (jnp.arange(half, dtype=f32) * (2.0 / LANE)))
    ang = pos.astype(f32)[:, None] * inv[None, :]
    cos, sin = jnp.cos(ang), jnp.sin(ang)
    return jnp.concatenate([cos, cos], axis=-1), jnp.concatenate([-sin, sin], axis=-1)


def _gla_kernel(q_ref, k_ref, v_ref, gg_ref, misc_ref, wa2_ref, ba_ref, gn_ref, s0_ref,
                o_ref, sout_ref, st_ref, *, c, n_inner, dk, dv):
    step = pl.program_id(2)

    @pl.when(step == 0)
    def _():
        st_ref[...] = s0_ref[0, 0].T

    nsub = c // GLA_SUB
    row_c = lax.broadcasted_iota(jnp.int32, (c, dk), 0)
    row_a = lax.broadcasted_iota(jnp.int32, (c, c), 0)
    lane_a = lax.broadcasted_iota(jnp.int32, (c, c), 1)
    sub_a = row_a % GLA_SUB

    def chunk(ci, carry):
        r = pl.multiple_of(ci * c, c)
        q = q_ref[pl.ds(r, c), :] * (dk ** -0.5)
        k = k_ref[pl.ds(r, c), :]
        v = v_ref[pl.ds(r, c), :]
        ga = misc_ref[pl.ds(r, c), :][:, :GLA_RANK]
        x = jnp.dot(ga.astype(bf16), wa2_ref[...], preferred_element_type=f32) + ba_ref[...]
        loga = (jnp.minimum(x, 0.0) - jnp.log1p(jnp.exp(-jnp.abs(x)))) / GLA_TAU
        b = loga
        d = 1
        while d < c:
            b = b + jnp.where(row_c >= d, pltpu.roll(b, d, 0), 0.0)
            d *= 2
        st = st_ref[...]
        vb = v.astype(bf16)
        o_inter = lax.dot_general((q * jnp.exp(b)).astype(bf16), st.astype(bf16), NT,
                                  preferred_element_type=f32)
        att = jnp.zeros((c, c), f32)
        for delta in range(GLA_SUB):
            if delta == 0:
                w = q * k
            else:
                w = q * pltpu.roll(k, delta, 0) * jnp.exp(jnp.minimum(b - pltpu.roll(b, delta, 0), 0.0))
            col = jnp.sum(w, axis=-1, keepdims=True)
            att = jnp.where(jnp.logical_and(lane_a == row_a - delta, sub_a >= delta), col, att)
        if nsub > 1:
            offs = [jnp.zeros((GLA_SUB, c), f32)]
            for i in range(1, nsub):
                r0 = i * GLA_SUB
                bref = b[r0 - 1:r0]
                qq = q[r0:r0 + GLA_SUB] * jnp.exp(b[r0:r0 + GLA_SUB] - bref)
                kk = jnp.where(row_c < r0, k * jnp.exp(jnp.minimum(bref - b, 0.0)), 0.0)
                offs.append(lax.dot_general(qq.astype(bf16), kk.astype(bf16), NT, preferred_element_type=f32))
            att = att + jnp.concatenate(offs, axis=0)
        o = o_inter + jnp.dot(att.astype(bf16), vb, preferred_element_type=f32)
        b_last = b[c - 1:c]
        kd = k * jnp.exp(b_last - b)
        st_ref[...] = jnp.exp(b_last) * st + lax.dot_general(vb, kd.astype(bf16), TN, preferred_element_type=f32)
        y = o * lax.rsqrt(jnp.mean(o * o, axis=-1, keepdims=True) + EPS) * gn_ref[...]
        gg = gg_ref[pl.ds(r, c), :]
        y = y * (gg * (1.0 / (1.0 + jnp.exp(-gg))))
        o_ref[pl.ds(r, c), :] = y.astype(o_ref.dtype)
        return carry

    lax.fori_loop(0, n_inner, chunk, 0)

    @pl.when(step == pl.num_programs(2) - 1)
    def _():
        sout_ref[0, 0] = st_ref[...].T


def gla(z, s0, sb0, wa2, ba, gn, *, n_seq, T, heads, dk, dv, cols):
    c = min(CHUNK, T)
    tb = max(_pick(T, (256, 128, 64, 32, 16)), c)
    n_steps = T // tb
    cq, ck, cv, cg, cm = cols
    in_specs = [
        pl.BlockSpec((tb, dk), lambda b, h, s: (b * n_steps + s, cq // dk + h)),
        pl.BlockSpec((tb, dk), lambda b, h, s: (b * n_steps + s, ck // dk + h)),
        pl.BlockSpec((tb, dv), lambda b, h, s: (b * n_steps + s, cv // dv + h)),
        pl.BlockSpec((tb, dv), lambda b, h, s: (b * n_steps + s, cg // dv + h)),
        pl.BlockSpec((tb, LANE), lambda b, h, s: (b * n_steps + s, cm // LANE)),
        pl.BlockSpec((GLA_RANK, dk), lambda b, h, s: (0, h)),
        pl.BlockSpec((1, dk), lambda b, h, s: (0, h)),
        pl.BlockSpec((1, dv), lambda b, h, s: (0, 0)),
        pl.BlockSpec((1, 1, dk, dv), lambda b, h, s: (sb0 + b, h, 0, 0)),
    ]
    out_specs = [
        pl.BlockSpec((tb, dv), lambda b, h, s: (b * n_steps + s, h)),
        pl.BlockSpec((1, 1, dk, dv), lambda b, h, s: (b, h, 0, 0)),
    ]
    return pl.pallas_call(
        functools.partial(_gla_kernel, c=c, n_inner=tb // c, dk=dk, dv=dv),
        grid=(n_seq, heads, n_steps),
        in_specs=in_specs,
        out_specs=out_specs,
        out_shape=[jax.ShapeDtypeStruct((n_seq * T, heads * dv), bf16),
                   jax.ShapeDtypeStruct((n_seq, heads, dk, dv), f32)],
        scratch_shapes=[pltpu.VMEM((dv, dk), f32)],
        compiler_params=_cparams(("parallel", "parallel", "arbitrary")),
        name="gla",
    )(z, z, z, z, z, wa2, ba, gn, s0)


def _dsa_kernel(dq_ref, iq_ref, misc_ref, k_ref, v_ref, ik_ref, o_ref, key_s, q_s, m_s, l_s, acc_s,
                *, tq, kb, s_valid, pos0, topk, heads, kv_heads, idx_heads):
    j = pl.program_id(1)
    qpos0 = pos0 + j * tq
    vis_end = jnp.minimum(((qpos0 + tq - 1) // CHUNK + 1) * CHUNK, s_valid)
    nkb = (vis_end + kb - 1) // kb
    qchunk = (qpos0 + lax.broadcasted_iota(jnp.int32, (tq, 1), 0)) // CHUNK
    w = misc_ref[...][:, GLA_RANK:GLA_RANK + idx_heads]
    group = heads // kv_heads

    def score_body(b, carry):
        c0 = pl.multiple_of(b * kb, kb)
        ikb = ik_ref[0, pl.ds(c0, kb), :]
        acc = jnp.zeros((tq, kb), f32)
        for h in range(idx_heads):
            s = lax.dot_general(iq_ref[:, h * IDX_HD:(h + 1) * IDX_HD], ikb, NT, preferred_element_type=f32)
            acc = acc + w[:, h:h + 1] * jnp.maximum(s, 0.0)
        kpos = c0 + lax.broadcasted_iota(jnp.int32, (1, kb), 1)
        vis = jnp.logical_and(kpos // CHUNK <= qchunk, kpos < s_valid)
        bits = pltpu.bitcast(acc, jnp.int32)
        key = bits ^ ((bits >> 31) & 0x7FFFFFFF)
        key_s[:, pl.ds(c0, kb)] = jnp.where(vis, key, INT_MIN)
        return carry

    lax.fori_loop(0, nkb, score_body, 0)

    one, zero = jnp.int32(1), jnp.int32(0)
    izeros = jnp.zeros((tq, LANE), jnp.int32)
    lane_i = lax.broadcasted_iota(jnp.int32, (1, LANE), 1)

    def count(pred):
        def body(b, cacc):
            c0 = pl.multiple_of(b * kb, kb)
            blk = key_s[:, pl.ds(c0, kb)]
            for t in range(kb // LANE):
                cacc = cacc + jnp.where(pred(blk[:, t * LANE:(t + 1) * LANE], c0 + t * LANE + lane_i), one, zero)
            return cacc
        return jnp.sum(lax.fori_loop(0, nkb, body, izeros), axis=-1, keepdims=True)

    def bit_body(i, carry):
        t_u, c_at = carry
        cand_u = t_u | lax.shift_left(one, 31 - i)
        cand = cand_u ^ INT_MIN
        cnt = count(lambda kt, it: kt >= cand)
        ok = cnt >= topk
        return jnp.where(ok, cand_u, t_u), jnp.where(ok, cnt, c_at)

    t_u, c_at = lax.fori_loop(0, 32, bit_body, (izeros, izeros))
    thr = jnp.maximum(t_u ^ INT_MIN, INT_MIN + 1)

    @pl.when(jnp.max(c_at) > topk)
    def _():
        need = topk - count(lambda kt, it: kt > thr)
        nbits = int(s_valid).bit_length()

        def idx_body(i, lo):
            cand = lo | lax.shift_left(one, nbits - 1 - i)
            c = count(lambda kt, it: jnp.logical_and(kt == thr, it < cand))
            return jnp.where(c < need, cand, lo)

        lo = lax.fori_loop(0, nbits, idx_body, izeros)

        def drop_body(b, carry):
            c0 = pl.multiple_of(b * kb, kb)
            blk = key_s[:, pl.ds(c0, kb)]
            tiles = []
            for t in range(kb // LANE):
                kt = blk[:, t * LANE:(t + 1) * LANE]
                drop = jnp.logical_and(kt == thr, c0 + t * LANE + lane_i > lo)
                tiles.append(jnp.where(drop, INT_MIN, kt))
            key_s[:, pl.ds(c0, kb)] = jnp.concatenate(tiles, axis=1)
            return carry

        lax.fori_loop(0, nkb, drop_body, 0)

    c2 = DSA_HD ** -0.5 * LOG2E
    for g in range(kv_heads):
        for i in range(group):
            hh = g * group + i
            q_s[g, i * tq:(i + 1) * tq, :] = dq_ref[:, hh * DSA_HD:(hh + 1) * DSA_HD]
    m_s[...] = jnp.full(m_s.shape, NEG_INIT, f32)
    l_s[...] = jnp.zeros(l_s.shape, f32)
    acc_s[...] = jnp.zeros(acc_s.shape, f32)

    def kv_body(b, carry):
        c0 = pl.multiple_of(b * kb, kb)
        keys = key_s[:, pl.ds(c0, kb)]
        bias = jnp.concatenate(
            [jnp.where(keys[:, t * LANE:(t + 1) * LANE] >= thr, 0.0, NEG_MASK) for t in range(kb // LANE)],
            axis=1)
        bias = jnp.concatenate([bias] * group, axis=0)
        for g in range(kv_heads):
            kblk = k_ref[0, pl.ds(c0, kb), g * DSA_HD:(g + 1) * DSA_HD]
            vblk = v_ref[0, pl.ds(c0, kb), g * DSA_HD:(g + 1) * DSA_HD]
            s = lax.dot_general(q_s[g], kblk, NT, preferred_element_type=f32) * c2 + bias
            m_old = m_s[g]
            m_new = jnp.maximum(m_old, jnp.max(s, axis=-1, keepdims=True))
            p = jnp.exp2(s - m_new)
            alpha = jnp.exp2(m_old - m_new)
            l_s[g] = alpha * l_s[g] + jnp.sum(p, axis=-1, keepdims=True)
            acc_s[g] = alpha * acc_s[g] + jnp.dot(p.astype(bf16), vblk, preferred_element_type=f32)
            m_s[g] = m_new
        return carry

    lax.fori_loop(0, nkb, kv_body, 0)
    for g in range(kv_heads):
        out = acc_s[g] / l_s[g]
        for i in range(group):
            hh = g * group + i
            o_ref[:, hh * DSA_HD:(hh + 1) * DSA_HD] = out[i * tq:(i + 1) * tq].astype(o_ref.dtype)


def dsa(dq, iq, z, k_all, v_all, ik_all, *, n_seq, T, s_valid, pos0, heads, kv_heads, idx_heads, misc_col, tq, kb):
    s_pad = k_all.shape[1]
    nq = T // tq
    topk = min(IDX_TOPK, s_valid // 4)
    group = heads // kv_heads
    kern = functools.partial(_dsa_kernel, tq=tq, kb=kb, s_valid=s_valid, pos0=pos0, topk=topk, heads=heads,
                             kv_heads=kv_heads, idx_heads=idx_heads)
    return pl.pallas_call(
        kern,
        grid=(n_seq, nq),
        in_specs=[
            pl.BlockSpec((tq, heads * DSA_HD), lambda b, j: (b * nq + j, 0)),
            pl.BlockSpec((tq, idx_heads * IDX_HD), lambda b, j: (b * nq + j, 0)),
            pl.BlockSpec((tq, LANE), lambda b, j: (b * nq + j, misc_col // LANE)),
            pl.BlockSpec((1, s_pad, kv_heads * DSA_HD), lambda b, j: (b, 0, 0)),
            pl.BlockSpec((1, s_pad, kv_heads * DSA_HD), lambda b, j: (b, 0, 0)),
            pl.BlockSpec((1, s_pad, IDX_HD), lambda b, j: (b, 0, 0)),
        ],
        out_specs=pl.BlockSpec((tq, heads * DSA_HD), lambda b, j: (b * nq + j, 0)),
        out_shape=jax.ShapeDtypeStruct((n_seq * T, heads * DSA_HD), bf16),
        scratch_shapes=[pltpu.VMEM((tq, s_pad), jnp.int32), pltpu.VMEM((kv_heads, group * tq, DSA_HD), bf16),
                        pltpu.VMEM((kv_heads, group * tq, 1), f32), pltpu.VMEM((kv_heads, group * tq, 1), f32),
                        pltpu.VMEM((kv_heads, group * tq, DSA_HD), f32)],
        compiler_params=_cparams(("parallel", "arbitrary")),
        name="dsa",
    )(dq, iq, z, k_all, v_all, ik_all)


def _diff_lambda(lq1, lk1, lq2, lk2, lam_init):
    return (jnp.exp(jnp.sum(lq1[...] * lk1[...], axis=-1, keepdims=True))
            - jnp.exp(jnp.sum(lq2[...] * lk2[...], axis=-1, keepdims=True)) + lam_init)


def _diff_finish(o, gn_ref, lam_init):
    y = o * lax.rsqrt(jnp.mean(o * o, axis=-1, keepdims=True) + EPS) * gn_ref[...]
    return y * (1.0 - lam_init)


def _diff_prompt_kernel(qt_ref, kt_ref, q_ref, k_ref, v_ref, lq1, lk1, lq2, lk2, gn_ref, o_ref, m_s, l_s, acc_s,
                        *, tq, tk, lam_init):
    p_id = pl.program_id(2)
    qi = qt_ref[p_id]
    ki = kt_ref[p_id]
    last_k = ((qi + 1) * tq - 1) // tk
    c2 = DIFF_HD ** -0.5 * LOG2E

    @pl.when(ki == 0)
    def _():
        m_s[...] = jnp.full(m_s.shape, NEG_INIT, f32)
        l_s[...] = jnp.zeros(l_s.shape, f32)
        acc_s[...] = jnp.zeros(acc_s.shape, f32)

    def update(masked):
        vb = v_ref[...]
        if masked:
            qc = (qi * tq + lax.broadcasted_iota(jnp.int32, (tq, 1), 0)) // CHUNK
            kc = (ki * tk + lax.broadcasted_iota(jnp.int32, (1, tk), 1)) // CHUNK
            bias = jnp.where(kc <= qc, 0.0, NEG_MASK)
        for c in range(2):
            s = lax.dot_general(q_ref[:, c * DIFF_HD:(c + 1) * DIFF_HD], k_ref[:, c * DIFF_HD:(c + 1) * DIFF_HD],
                                NT, preferred_element_type=f32) * c2
            if masked:
                s = s + bias
            m_old = m_s[c]
            m_new = jnp.maximum(m_old, jnp.max(s, axis=-1, keepdims=True))
            p = jnp.exp2(s - m_new)
            alpha = jnp.exp2(m_old - m_new)
            l_s[c] = alpha * l_s[c] + jnp.sum(p, axis=-1, keepdims=True)
            acc_s[c] = alpha * acc_s[c] + jnp.dot(p.astype(bf16), vb, preferred_element_type=f32)
            m_s[c] = m_new

    needs_mask = (ki + 1) * tk > qi * tq + CHUNK

    @pl.when(needs_mask)
    def _():
        update(True)

    @pl.when(jnp.logical_not(needs_mask))
    def _():
        update(False)

    @pl.when(ki == last_k)
    def _():
        lam = _diff_lambda(lq1, lk1, lq2, lk2, lam_init)
        o = acc_s[0] / l_s[0] - lam * (acc_s[1] / l_s[1])
        o_ref[...] = _diff_finish(o, gn_ref, lam_init).astype(o_ref.dtype)


def diff_prompt(q, k, v, lams, gn, *, n_seq, T, heads, lam_init):
    hw = 2 * DIFF_HD
    tq = _pick(T, (DIFF_TQ, 256, 128))
    tk = _pick(T, (DIFF_TK, 512, 256, 128))
    nq, nk = T // tq, T // tk
    pairs = [(qi, ki) for qi in range(nq) for ki in range(((qi + 1) * tq - 1) // tk + 1)]
    qt = jnp.asarray(np.array([p[0] for p in pairs], np.int32))
    kt = jnp.asarray(np.array([p[1] for p in pairs], np.int32))
    vec = pl.BlockSpec((1, DIFF_HD), lambda b, h, p, qt, kt: (0, 0))
    kv = pl.BlockSpec((tk, hw), lambda b, h, p, qt, kt: (b * nk + kt[p], h))
    qo = pl.BlockSpec((tq, hw), lambda b, h, p, qt, kt: (b * nq + qt[p], h))
    return pl.pallas_call(
        functools.partial(_diff_prompt_kernel, tq=tq, tk=tk, lam_init=lam_init),
        grid_spec=pltpu.PrefetchScalarGridSpec(
            num_scalar_prefetch=2,
            grid=(n_seq, heads, len(pairs)),
            in_specs=[qo, kv, kv, vec, vec, vec, vec, pl.BlockSpec((1, hw), lambda b, h, p, qt, kt: (0, 0))],
            out_specs=qo,
            scratch_shapes=[pltpu.VMEM((2, tq, 1), f32), pltpu.VMEM((2, tq, 1), f32), pltpu.VMEM((2, tq, hw), f32)],
        ),
        out_shape=jax.ShapeDtypeStruct((n_seq * T, heads * hw), bf16),
        compiler_params=_cparams(("parallel", "parallel", "arbitrary")),
        name="diff_prompt",
    )(qt, kt, q, k, v, *lams, gn)


def _diff_sample_kernel(q_ref, kn_ref, vn_ref, k0_ref, k1_ref, vlo_ref, vhi_ref, lq1, lk1, lq2, lk2, gn_ref, o_ref,
                        m_s, l_s, acc_s, flat_s, *, tk, hpb, lam_init):
    kb = pl.program_id(2)
    T = q_ref.shape[0]
    hw = 2 * DIFF_HD
    c2 = DIFF_HD ** -0.5 * LOG2E

    @pl.when(kb == 0)
    def _():
        m_s[...] = jnp.full(m_s.shape, NEG_INIT, f32)
        l_s[...] = jnp.zeros(l_s.shape, f32)
        acc_s[...] = jnp.zeros(acc_s.shape, f32)

    def update(hl, kblks, vblk):
        ps = []
        for c in range(2):
            idx = 2 * hl + c
            s = lax.dot_general(q_ref[:, idx * DIFF_HD:(idx + 1) * DIFF_HD], kblks[c], NT,
                                preferred_element_type=f32) * c2
            m_old = m_s[idx]
            m_new = jnp.maximum(m_old, jnp.max(s, axis=-1, keepdims=True))
            p = jnp.exp2(s - m_new)
            alpha = jnp.exp2(m_old - m_new)
            l_s[idx] = alpha * l_s[idx] + jnp.sum(p, axis=-1, keepdims=True)
            m_s[idx] = m_new
            acc_s[idx] = alpha * acc_s[idx]
            ps.append(p.astype(bf16))
        pv = jnp.dot(jnp.concatenate(ps, axis=0), vblk, preferred_element_type=f32)
        acc_s[2 * hl] = acc_s[2 * hl] + pv[:T]
        acc_s[2 * hl + 1] = acc_s[2 * hl + 1] + pv[T:]

    for n, ref in enumerate((k0_ref, k1_ref, vlo_ref, vhi_ref)):
        flat_s[n] = ref[...].reshape(tk * 8, DIFF_HD)

    def rows(n, r):
        return flat_s[n, pl.ds(r, tk, stride=8), :]

    for hl in range(hpb):
        quad, j0 = hl // 4, (hl % 4) * 2
        kblks = tuple(rows(quad, j0 + c).astype(bf16) for c in range(2))
        vblk = jnp.concatenate([rows(2, hl), rows(3, hl)], axis=1).astype(bf16)
        update(hl, kblks, vblk)

    @pl.when(kb == pl.num_programs(2) - 1)
    def _():
        lam = _diff_lambda(lq1, lk1, lq2, lk2, lam_init)
        for hl in range(hpb):
            kblks = tuple(kn_ref[:, (2 * hl + c) * DIFF_HD:(2 * hl + c + 1) * DIFF_HD] for c in range(2))
            update(hl, kblks, vn_ref[:, hl * hw:(hl + 1) * hw])
            o = acc_s[2 * hl] / l_s[2 * hl] - lam * (acc_s[2 * hl + 1] / l_s[2 * hl + 1])
            o_ref[:, hl * hw:(hl + 1) * hw] = _diff_finish(o, gn_ref, lam_init).astype(o_ref.dtype)


def diff_sample(q, k, v, k_past, v_past, pb0, lams, gn, *, n_seq, T, heads, lam_init):
    hw = 2 * DIFF_HD
    hpb = 8
    past = k_past.shape[1]
    assert (past + T - 1) // CHUNK == past // CHUNK and past % CHUNK == 0
    assert heads % hpb == 0
    tk = _pick(past, (DIFF_SAMPLE_TK, 512, 256, 128))
    new = pl.BlockSpec((T, hpb * hw), lambda b, o, kb: (b, o))
    vec = pl.BlockSpec((1, DIFF_HD), lambda b, o, kb: (0, 0))
    tile = (None, tk, 8, DIFF_HD)
    return pl.pallas_call(
        functools.partial(_diff_sample_kernel, tk=tk, hpb=hpb, lam_init=lam_init),
        grid=(n_seq, heads // hpb, past // tk),
        in_specs=[new, new, new,
                  pl.BlockSpec(tile, lambda b, o, kb: (pb0 + b, kb, 2 * o, 0)),
                  pl.BlockSpec(tile, lambda b, o, kb: (pb0 + b, kb, 2 * o + 1, 0)),
                  pl.BlockSpec(tile, lambda b, o, kb: (pb0 + b, kb, o, 0)),
                  pl.BlockSpec(tile, lambda b, o, kb: (pb0 + b, kb, o, 1)),
                  vec, vec, vec, vec, pl.BlockSpec((1, hw), lambda b, o, kb: (0, 0))],
        out_specs=pl.BlockSpec((T, hpb * hw), lambda b, o, kb: (b, o)),
        out_shape=jax.ShapeDtypeStruct((n_seq * T, heads * hw), bf16),
        scratch_shapes=[pltpu.VMEM((2 * hpb, T, 1), f32), pltpu.VMEM((2 * hpb, T, 1), f32),
                        pltpu.VMEM((2 * hpb, T, hw), f32), pltpu.VMEM((4, tk * 8, DIFF_HD), f32)],
        compiler_params=_cparams(("parallel", "parallel", "arbitrary")),
        name="diff_sample",
    )(q, k, v, k_past, k_past, v_past, v_past, *lams, gn)


def _ffn_up_kernel(*refs, T, tm, tn, tiled_seq, emit_w):
    it = iter(refs)
    x_ref, wa_ref, wg_ref = next(it), next(it), next(it)
    if tiled_seq:
        prev_a, prev_g = (next(it),), (next(it),)
    else:
        prev_a = (next(it), next(it))
        prev_g = (next(it), next(it))
    cwa_ref, cwg_ref, cba_ref, cbg_ref = next(it), next(it), next(it), next(it)
    o_ref, ta_ref, tg_ref = next(it), next(it), next(it)
    wb_refs = (next(it), next(it)) if emit_w else None
    was_s, wgs_s, carry_s = next(it), next(it), next(it)
    i = pl.program_id(1)

    @pl.when(i == 0)
    def _():
        was_s[...] = wa_ref[...].astype(bf16)
        wgs_s[...] = wg_ref[...].astype(bf16)
        if emit_w:
            wb_refs[0][...] = was_s[...]
            wb_refs[1][...] = wgs_s[...]

    x = x_ref[...]
    row = lax.broadcasted_iota(jnp.int32, (tm, tn), 0)

    def half(idx, ws_s, prevs, cw_ref, cb_ref, tail_ref):
        u = jnp.dot(x, ws_s[...], preferred_element_type=f32)
        if tiled_seq:
            first = (i % (T // tm)) == 0
            c0 = jnp.where(first, prevs[0][0, 0:1, :], carry_s[idx, 0:1, :])
            c1 = jnp.where(first, prevs[0][0, 1:2, :], carry_s[idx, 1:2, :])
            u1 = jnp.where(row == 0, c1, pltpu.roll(u, 1, 0))
            u2 = jnp.where(row == 0, c0, jnp.where(row == 1, c1, pltpu.roll(u, 2, 0)))
            carry_s[idx] = u[tm - 2:tm]
            tail_ref[0] = u[tm - 2:tm]
        else:
            rmod = row % T
            p1 = prevs[1][...]
            u1 = jnp.where(rmod == 0, p1, pltpu.roll(u, 1, 0))
            u2 = jnp.where(rmod == 0, prevs[0][...], jnp.where(rmod == 1, pltpu.roll(p1, 1, 0), pltpu.roll(u, 2, 0)))
            tail_ref[...] = u
        return cb_ref[...] + u2 * cw_ref[0:1, :] + u1 * cw_ref[1:2, :] + u * cw_ref[2:3, :]

    a = half(0, was_s, prev_a, cwa_ref, cba_ref, ta_ref)
    g = half(1, wgs_s, prev_g, cwg_ref, cbg_ref, tg_ref)
    o_ref[...] = (g * (1.0 / (1.0 + jnp.exp(-g))) * a).astype(o_ref.dtype)


def ffn_up(x, w, prev, pb0, cw, cb, l, *, n_seq, T, dff):
    M, K = x.shape
    tn = _pick(dff, (256, 128))
    nb = dff // tn
    tm = _pick(M, (FFN_TM, 512, 256, 128, 64, 32, 16))
    tiled_seq = tm <= T
    assert T % tm == 0 if tiled_seq else tm % T == 0
    emit_w = not isinstance(w, tuple)
    if emit_w:
        w_args = [w, w]
        w_specs = [pl.BlockSpec((None, K, tn), lambda j, i: (l, 0, j)),
                   pl.BlockSpec((None, K, tn), lambda j, i: (l, 0, nb + j))]
    else:
        w_args = list(w)
        w_specs = [pl.BlockSpec((K, tn), lambda j, i: (0, j))] * 2
    if tiled_seq:
        spt = T // tm
        p_args = [prev, prev]
        p_specs = [pl.BlockSpec((1, CONV_W - 1, tn), lambda j, i: (pb0 + i // spt, 0, j)),
                   pl.BlockSpec((1, CONV_W - 1, tn), lambda j, i: (pb0 + i // spt, 0, nb + j))]
        tail_shape = jax.ShapeDtypeStruct((n_seq, CONV_W - 1, dff), f32)
        tail_spec = pl.BlockSpec((1, CONV_W - 1, tn), lambda j, i: (i // spt, 0, j))
    else:
        pv = prev[pb0:pb0 + n_seq]
        ex = [jnp.pad(pv[:, k:k + 1], ((0, 0), (0, T - 1), (0, 0))).reshape(M, 2 * dff) for k in range(CONV_W - 1)]
        p_args = [ex[0], ex[1], ex[0], ex[1]]
        p_specs = [pl.BlockSpec((tm, tn), lambda j, i: (i, j)), pl.BlockSpec((tm, tn), lambda j, i: (i, j)),
                   pl.BlockSpec((tm, tn), lambda j, i: (i, nb + j)), pl.BlockSpec((tm, tn), lambda j, i: (i, nb + j))]
        tail_shape = jax.ShapeDtypeStruct((M, dff), f32)
        tail_spec = pl.BlockSpec((tm, tn), lambda j, i: (i, j))
    c_specs = [pl.BlockSpec((None, CONV_W, tn), lambda j, i: (l, 0, j)),
               pl.BlockSpec((None, CONV_W, tn), lambda j, i: (l, 0, nb + j)),
               pl.BlockSpec((None, 1, tn), lambda j, i: (l, 0, j)),
               pl.BlockSpec((None, 1, tn), lambda j, i: (l, 0, nb + j))]
    cb3 = cb.reshape(cb.shape[0], 1, 2 * dff)
    out_shape = [jax.ShapeDtypeStruct((M, dff), bf16), tail_shape, tail_shape]
    out_specs = [pl.BlockSpec((tm, tn), lambda j, i: (i, j)), tail_spec, tail_spec]
    if emit_w:
        out_shape += [jax.ShapeDtypeStruct((K, dff), bf16)] * 2
        out_specs += [pl.BlockSpec((K, tn), lambda j, i: (0, j))] * 2
    outs = pl.pallas_call(
        functools.partial(_ffn_up_kernel, T=T, tm=tm, tn=tn, tiled_seq=tiled_seq, emit_w=emit_w),
        grid=(nb, M // tm),
        in_specs=[pl.BlockSpec((tm, K), lambda j, i: (i, 0))] + w_specs + p_specs + c_specs,
        out_specs=out_specs,
        out_shape=out_shape,
        scratch_shapes=[pltpu.VMEM((K, tn), bf16), pltpu.VMEM((K, tn), bf16), pltpu.VMEM((2, CONV_W - 1, tn), f32)],
        compiler_params=_cparams(("parallel", "arbitrary")),
        name="ffn_up",
    )(x, *w_args, *p_args, cw, cw, cb3, cb3)
    g, ta, tg = outs[:3]
    if tiled_seq:
        state = jnp.concatenate([ta, tg], axis=-1)
    else:
        state = jnp.concatenate([ta.reshape(n_seq, T, dff)[:, T - (CONV_W - 1):],
                                 tg.reshape(n_seq, T, dff)[:, T - (CONV_W - 1):]], axis=-1)
    return g, state, (tuple(outs[3:]) if emit_w else None)


class _Stream:
    def __init__(self, x, pos0, past):
        self.n_seq, self.T, d = x.shape
        self.h = x.reshape(self.n_seq * self.T, d)
        self.pos0 = pos0
        self.past = past
        pos = jnp.tile(pos0 + jnp.arange(self.T, dtype=jnp.int32), self.n_seq)
        self.tables = rope_tables(pos)


def kernel(x_prompt, x_sample, cache_gla_state, cache_dsa_k, cache_dsa_v, cache_idx_k, cache_diff_k, cache_diff_v, state_ffn_conv, norm_mix, norm_ffn, norm_final, w_in_ab, w_gla_a2, b_gla_a, gla_norm, w_out_ab, w_in_c, lambda_q1, lambda_k1, lambda_q2, lambda_k2, diff_norm, w_out_c, w_up, conv_w, conv_b, w_down):
    B, S, D = x_prompt.shape
    DB, DS, _ = x_sample.shape
    depth = norm_mix.shape[0]
    n_ab, _, gh, gdk, gdv = cache_gla_state.shape
    n_c = cache_diff_k.shape[0]
    past = cache_dsa_k.shape[2]
    kvh = cache_dsa_k.shape[3]
    kvw = kvh * DSA_HD
    dh = w_out_ab.shape[1] - gh * gdv
    dsa_heads = dh // DSA_HD
    idx_heads = (w_in_ab.shape[2] - (2 * gh * gdk + 2 * gh * gdv + GLA_RANK + dh + 2 * kvw + IDX_HD)) // (IDX_HD + 1)
    diff_heads = cache_diff_k.shape[3]
    dff = w_down.shape[1]
    cw_ = diff_heads * 2 * DIFF_HD

    streams = (_Stream(x_prompt, 0, 0), _Stream(x_sample, past, past))
    names = ("gla", "dk", "dv", "ik", "ck", "cv", "conv")
    outs = [{k: [] for k in names} for _ in streams]

    c_gla = cache_gla_state.reshape(n_ab * DB, gh, gdk, gdv)
    c_dk = cache_dsa_k.reshape(n_ab, DB, past, kvw)
    c_dv = cache_dsa_v.reshape(n_ab, DB, past, kvw)
    c_ik = cache_idx_k.reshape(n_ab, DB, past, IDX_HD)
    c_ck = cache_diff_k.reshape(n_c * DB, past, 2 * diff_heads, DIFF_HD)
    c_cv = cache_diff_v.reshape(n_c * DB, past, diff_heads, 2 * DIFF_HD)
    c_conv = state_ffn_conv.reshape(depth * DB, CONV_W - 1, 2 * dff)
    zero_state = jnp.zeros((B, gh, gdk, gdv), f32)
    zero_conv = jnp.zeros((B, CONV_W - 1, 2 * dff), f32)

    for l in range(depth):
        i = l // 2
        if l % 2 == 0:
            sizes = (gh * gdk, gh * gdk, gh * gdv, gh * gdv, GLA_RANK, dh, kvw, kvw, idx_heads * IDX_HD, IDX_HD,
                     idx_heads)
            offs = np.concatenate([[0], np.cumsum(sizes)]).tolist()
            wi = w_in_ab[i]
            seg = lambda n: wi[:, offs[n]:offs[n + 1]]
            pad = LANE - GLA_RANK - idx_heads
            w_perm = jnp.concatenate([seg(0), seg(1), seg(2), seg(3), seg(5), seg(6), seg(7), seg(8), seg(9),
                                      seg(4), seg(10), jnp.zeros((D, pad), wi.dtype)], axis=1).astype(bf16)
            c_gq = 0
            c_gk = c_gq + gh * gdk
            c_gv = c_gk + gh * gdk
            c_gg = c_gv + gh * gdv
            c_dq = c_gg + gh * gdv
            c_dk_ = c_dq + dh
            c_dv_ = c_dk_ + kvw
            c_iq = c_dv_ + kvw
            c_ik_ = c_iq + idx_heads * IDX_HD
            c_misc = c_ik_ + IDX_HD
            wa2 = w_gla_a2[i].astype(bf16)
            ba = b_gla_a[i].reshape(1, gh * gdk)
            gn = gla_norm[i].reshape(1, gdv)
            w_out = w_out_ab[i].astype(bf16)
            for si, st in enumerate(streams):
                hn = rmsnorm(st.h, norm_mix[l], bf16)
                z = matmul(hn, w_perm, f32, name="in_ab")
                (dq_b,) = take_cols(z, c_dq, dsa_heads, st.tables, True, (bf16,))
                dk_f, dk_b = take_cols(z, c_dk_, kvh, st.tables, True, (f32, bf16))
                dv_f, dv_b = take_cols(z, c_dv_, kvh, st.tables, False, (f32, bf16))
                (iq_b,) = take_cols(z, c_iq, idx_heads, st.tables, True, (bf16,))
                ik_f, ik_b = take_cols(z, c_ik_, 1, st.tables, True, (f32, bf16))
                if st.past:
                    s0, sb0 = c_gla, i * DB
                else:
                    s0, sb0 = zero_state, 0
                go, gst = gla(z, s0, sb0, wa2, ba, gn, n_seq=st.n_seq, T=st.T, heads=gh, dk=gdk, dv=gdv,
                              cols=(c_gq, c_gk, c_gv, c_gg, c_misc))
                s_valid = st.past + st.T
                if st.past:
                    tq, kb = st.T, DSA_KB
                    s_pad = -(-s_valid // kb) * kb

                    def cat(cache, new, width):
                        return jnp.concatenate([cache.astype(bf16), new.reshape(st.n_seq, st.T, width),
                                                jnp.zeros((st.n_seq, s_pad - s_valid, width), bf16)], axis=1)

                    k_all, v_all, ik_all = cat(c_dk[i], dk_b, kvw), cat(c_dv[i], dv_b, kvw), cat(c_ik[i], ik_b, IDX_HD)
                else:
                    tq, kb = _pick(st.T, (DSA_TQ,)), _pick(st.T, (DSA_KB, 256, 128))
                    k_all = dk_b.reshape(st.n_seq, st.T, kvw)
                    v_all = dv_b.reshape(st.n_seq, st.T, kvw)
                    ik_all = ik_b.reshape(st.n_seq, st.T, IDX_HD)
                do = dsa(dq_b, iq_b, z, k_all, v_all, ik_all, n_seq=st.n_seq, T=st.T, s_valid=s_valid, pos0=st.pos0,
                         heads=dsa_heads, kv_heads=kvh, idx_heads=idx_heads, misc_col=c_misc, tq=tq, kb=kb)
                st.h = matmul((go, do), w_out, f32, res=st.h, name="out_ab")
                o = outs[si]
                o["gla"].append(gst)
                o["dk"].append(dk_f.reshape(st.n_seq, st.T, kvh, DSA_HD))
                o["dv"].append(dv_f.reshape(st.n_seq, st.T, kvh, DSA_HD))
                o["ik"].append(ik_f.reshape(st.n_seq, st.T, IDX_HD))
        else:
            lam_init = 0.8 - 0.6 * math.exp(-0.3 * l)
            w_in = w_in_c[i].astype(bf16)
            w_out = w_out_c[i].astype(bf16)
            lams = tuple(a[i].reshape(1, DIFF_HD) for a in (lambda_q1, lambda_k1, lambda_q2, lambda_k2))
            gn = diff_norm[i].reshape(1, 2 * DIFF_HD)
            for si, st in enumerate(streams):
                hn = rmsnorm(st.h, norm_mix[l], bf16)
                z = matmul(hn, w_in, f32, name="in_c")
                (q_b,) = take_cols(z, 0, cw_ // LANE, st.tables, True, (bf16,))
                k_f, k_b = take_cols(z, cw_, cw_ // LANE, st.tables, True, (f32, bf16))
                v_f, v_b = take_cols(z, 2 * cw_, cw_ // LANE, st.tables, False, (f32, bf16))
                if st.past:
                    at = diff_sample(q_b, k_b, v_b, c_ck, c_cv, i * DB, lams, gn, n_seq=st.n_seq, T=st.T,
                                     heads=diff_heads, lam_init=lam_init)
                else:
                    at = diff_prompt(q_b, k_b, v_b, lams, gn, n_seq=st.n_seq, T=st.T, heads=diff_heads,
                                     lam_init=lam_init)
                st.h = matmul(at, w_out, f32, res=st.h, name="out_c")
                o = outs[si]
                o["ck"].append(k_f.reshape(st.n_seq, st.T, diff_heads, 2, DIFF_HD))
                o["cv"].append(v_f.reshape(st.n_seq, st.T, diff_heads, 2 * DIFF_HD))

        wd = w_down[l].astype(bf16)
        wu = w_up
        for si, st in enumerate(streams):
            hn = rmsnorm(st.h, norm_ffn[l], bf16)
            prev, pb0 = (c_conv, l * DB) if st.past else (zero_conv, 0)
            g, conv_state, wu_b = ffn_up(hn, wu, prev, pb0, conv_w, conv_b, l, n_seq=st.n_seq, T=st.T, dff=dff)
            if wu_b is not None:
                wu = wu_b
            st.h = matmul(g, wd, f32, res=st.h, name="ffn_down")
            outs[si]["conv"].append(conv_state)

    ys = [rmsnorm(st.h, norm_final, f32).reshape(st.n_seq, st.T, D) for st in streams]
    tail = [jnp.stack(o[k]) for o in outs for k in names]
    return (ys[0], ys[1], *tail)
```

```python
import functools
import math

import numpy as np
import jax
import jax.numpy as jnp
from jax import lax
from jax.experimental import pallas as pl
from jax.experimental.pallas import tpu as pltpu

CHUNK = 64
EPS = 1e-6
ROPE_THETA = 10000.0
GLA_RANK = 16
GLA_TAU = 16.0
DSA_HD = 128
IDX_HD = 128
IDX_TOPK = 256
DIFF_HD = 128
CONV_W = 3

LANE = 128
GLA_SUB = 16
DSA_TQ, DSA_KB = 128, 1024
DSA_KB_SAMPLE = 512
DIFF_TQ, DIFF_TK = 1024, 1024
FFN_TM = 1024
DIFF_SAMPLE_TK = 512
VMEM_LIMIT = 60 * 1024 * 1024
MM_VMEM_BUDGET = 52 * 1024 * 1024
NEG_INIT = -1e30
NEG_MASK = -3e38
INT_MIN = -2147483648
LOG2E = 1.4426950408889634

f32 = jnp.float32
bf16 = jnp.bfloat16
NT = (((1,), (1,)), ((), ()))
TN = (((0,), (0,)), ((), ()))


def _cparams(sem):
    return pltpu.CompilerParams(dimension_semantics=sem, vmem_limit_bytes=VMEM_LIMIT)


def _pick(dim, cands):
    for c in cands:
        if c <= dim and dim % c == 0:
            return c
    return dim


def _rmsnorm_kernel(x_ref, g_ref, o_ref):
    x = x_ref[...]
    y = x * lax.rsqrt(jnp.mean(x * x, axis=-1, keepdims=True) + EPS)
    o_ref[...] = (y * g_ref[...]).astype(o_ref.dtype)


def rmsnorm(x, g, out_dtype):
    M, D = x.shape
    tm = _pick(M, (256, 128, 64, 32, 16, 8))
    return pl.pallas_call(
        _rmsnorm_kernel,
        grid=(M // tm,),
        in_specs=[pl.BlockSpec((tm, D), lambda i: (i, 0)), pl.BlockSpec((1, D), lambda i: (0, 0))],
        out_specs=pl.BlockSpec((tm, D), lambda i: (i, 0)),
        out_shape=jax.ShapeDtypeStruct((M, D), out_dtype),
        compiler_params=_cparams(("parallel",)),
        name="rmsnorm",
    )(x, g.reshape(1, D).astype(f32))


def _mm_kernel(*refs, n_x, has_res):
    x_refs, w_ref = refs[:n_x], refs[n_x]
    o_ref = refs[-1]
    acc = None
    k0 = 0
    for x_ref in x_refs:
        kw = x_ref.shape[1]
        part = jnp.dot(x_ref[...], w_ref[k0:k0 + kw, :], preferred_element_type=f32)
        acc = part if acc is None else acc + part
        k0 += kw
    if has_res:
        acc = refs[n_x + 1][...] + acc
    o_ref[...] = acc.astype(o_ref.dtype)


def matmul(xs, w, out_dtype, res=None, name="matmul", layer=None):
    xs = tuple(xs) if isinstance(xs, (tuple, list)) else (xs,)
    M = xs[0].shape[0]
    K, N = w.shape[-2:]
    ob = jnp.dtype(out_dtype).itemsize
    best = None
    for tm in (1024, 768, 512, 384, 256, 128, 64, 32, 16, 8):
        if M % tm:
            continue
        for tn in (1024, 768, 512, 384, 256, 128):
            if N % tn:
                continue
            vm = 2 * (tm * K * 2 + K * tn * 2 + tm * tn * ob + (tm * tn * 4 if res is not None else 0))
            if vm > MM_VMEM_BUDGET:
                continue
            score = (tm * tn) / (tm + tn)
            if best is None or score > best[0]:
                best = (score, tm, tn)
    _, tm, tn = best
    in_specs = [pl.BlockSpec((tm, x.shape[1]), lambda i, j: (i, 0)) for x in xs]
    if layer is None:
        in_specs.append(pl.BlockSpec((K, tn), lambda i, j: (0, j)))
    else:
        in_specs.append(pl.BlockSpec((None, K, tn), lambda i, j: (layer, 0, j)))
    args = list(xs) + [w]
    if res is not None:
        in_specs.append(pl.BlockSpec((tm, tn), lambda i, j: (i, j)))
        args.append(res)
    return pl.pallas_call(
        functools.partial(_mm_kernel, n_x=len(xs), has_res=res is not None),
        grid=(M // tm, N // tn),
        in_specs=in_specs,
        out_specs=pl.BlockSpec((tm, tn), lambda i, j: (i, j)),
        out_shape=jax.ShapeDtypeStruct((M, N), out_dtype),
        compiler_params=_cparams(("parallel", "arbitrary")),
        name=name,
    )(*args)


def _proj_kernel(x_ref, w_ref, c_ref, s_ref, *o_refs, segs, tn):
    j = pl.program_id(1)
    acc = jnp.dot(x_ref[...], w_ref[...], preferred_element_type=f32)

    @pl.when(j == 0)
    def _():
        for (j0, j1, rotate, out_ids, widths) in segs:
            if j0 > 0:
                for k in jax.tree_util.tree_leaves(out_ids):
                    o_refs[k][...] = jnp.zeros(o_refs[k].shape, o_refs[k].dtype)

    for (j0, j1, rotate, out_ids, widths) in segs:
        @pl.when(jnp.logical_and(j >= j0, j < j1))
        def _(rotate=rotate, out_ids=out_ids, widths=widths):
            cos, sin = c_ref[...], s_ref[...]
            col = 0
            for part, k_ids in enumerate(out_ids if widths else (out_ids,)):
                wcols = widths[part] if widths else tn
                rot = rotate[part] if widths else rotate
                for g in range(wcols // LANE):
                    y = acc[:, col + g * LANE:col + (g + 1) * LANE]
                    if rot:
                        y = y * cos + pltpu.roll(y, LANE // 2, 1) * sin
                    for k in (k_ids if isinstance(k_ids, tuple) else (k_ids,)):
                        o_refs[k][:, g * LANE:(g + 1) * LANE] = y.astype(o_refs[k].dtype)
                col += wcols


def proj(x, w, tables, segs_cols, tn, name):
    M, K = x.shape
    N = w.shape[1]
    tm = _pick(M, (1024, 768, 512, 256, 128, 64, 32, 16, 8))
    segs, out_shape, out_specs = [], [], []
    j0 = 0
    for sc in segs_cols:
        n_cols = sc[0]
        assert n_cols % tn == 0 or (len(sc) == 2 and n_cols <= tn)
        nj = max(n_cols // tn, 1)
        lo, hi = j0, j0 + nj

        def omap(i, j, lo=lo, hi=hi):
            return (i, jnp.clip(j - lo, 0, hi - lo - 1))

        if len(sc) == 3:
            _, rotate, dtypes = sc
            ids = []
            for dt in dtypes:
                ids.append(len(out_shape))
                out_shape.append(jax.ShapeDtypeStruct((M, n_cols), dt))
                out_specs.append(pl.BlockSpec((tm, tn), omap))
            segs.append((lo, hi, rotate, tuple(ids), None))
        else:
            parts = sc[1]
            ids, widths, rots = [], [], []
            for (width, rotate, dtypes) in parts:
                pid = []
                for dt in dtypes:
                    pid.append(len(out_shape))
                    out_shape.append(jax.ShapeDtypeStruct((M, width), dt))
                    out_specs.append(pl.BlockSpec((tm, width), lambda i, j: (i, 0)))
                ids.append(tuple(pid))
                widths.append(width)
                rots.append(rotate)
            segs.append((lo, hi, tuple(rots), tuple(ids), tuple(widths)))
        j0 = hi
    assert j0 * tn == N, (j0, tn, N)
    tab = pl.BlockSpec((tm, LANE), lambda i, j: (i, 0))
    return pl.pallas_call(
        functools.partial(_proj_kernel, segs=tuple(segs), tn=tn),
        grid=(M // tm, N // tn),
        in_specs=[pl.BlockSpec((tm, K), lambda i, j: (i, 0)), pl.BlockSpec((K, tn), lambda i, j: (0, j)), tab, tab],
        out_specs=out_specs,
        out_shape=out_shape,
        compiler_params=_cparams(("parallel", "arbitrary")),
        name=name,
    )(x, w, *tables)


def _prep_w_ab_kernel(a_ref, b_ref, o_ref, *, j_shift, j_misc, shift):
    j = pl.program_id(0)
    lane = lax.broadcasted_iota(jnp.int32, a_ref.shape, 1)

    @pl.when(j < j_shift)
    def _():
        o_ref[...] = a_ref[...].astype(bf16)

    @pl.when(jnp.logical_and(j >= j_shift, j < j_misc))
    def _():
        o_ref[...] = jnp.concatenate([a_ref[:, shift:], b_ref[:, :shift]], axis=1).astype(bf16)

    @pl.when(j == j_misc)
    def _():
        o_ref[...] = jnp.where(lane < shift, a_ref[...], jnp.where(lane < 2 * shift, b_ref[...], 0.0)).astype(bf16)

    @pl.when(j > j_misc)
    def _():
        o_ref[...] = jnp.zeros(o_ref.shape, bf16)


def prep_w_ab(w, layer, *, c_ga, c_end_ik, n_out):
    _, K, n_in = w.shape
    shift = GLA_RANK
    assert c_ga % LANE == 0 and (c_end_ik - shift) % LANE == 0 and n_in - c_end_ik == shift
    j_shift = c_ga // LANE
    j_misc = (c_end_ik - shift) // LANE
    last = (n_in - 1) // LANE

    def amap(j):
        return (layer, 0, jnp.where(j == j_misc, j_shift, jnp.minimum(j, last)))

    def bmap(j):
        return (layer, 0, jnp.where(j == j_misc, j_misc, jnp.minimum(j + 1, last)))

    return pl.pallas_call(
        functools.partial(_prep_w_ab_kernel, j_shift=j_shift, j_misc=j_misc, shift=shift),
        grid=(n_out // LANE,),
        in_specs=[pl.BlockSpec((None, K, LANE), amap), pl.BlockSpec((None, K, LANE), bmap)],
        out_specs=pl.BlockSpec((K, LANE), lambda j: (0, j)),
        out_shape=jax.ShapeDtypeStruct((K, n_out), bf16),
        compiler_params=_cparams(("parallel",)),
        name="prep_w_ab",
    )(w, w)


def rope_tables(pos):
    half = LANE // 2
    inv = 1.0 / (ROPE_THETA ** (jnp.arange(half, dtype=f32) * (2.0 / LANE)))
    ang = pos.astype(f32)[:, None] * inv[None, :]
    cos, sin = jnp.cos(ang), jnp.sin(ang)
    return jnp.concatenate([cos, cos], axis=-1), jnp.concatenate([-sin, sin], axis=-1)


def _gla_kernel(q_ref, k_ref, v_ref, gg_ref, misc_ref, wa2_ref, ba_ref, gn_ref, s0_ref,
                o_ref, sout_ref, st_ref, *, c, n_inner, dk, dv):
    step = pl.program_id(2)

    @pl.when(step == 0)
    def _():
        st_ref[...] = s0_ref[0, 0].T

    nsub = c // GLA_SUB
    row_c = lax.broadcasted_iota(jnp.int32, (c, dk), 0)
    row_a = lax.broadcasted_iota(jnp.int32, (c, c), 0)
    lane_a = lax.broadcasted_iota(jnp.int32, (c, c), 1)
    sub_a = row_a % GLA_SUB

    def chunk(ci, carry):
        r = pl.multiple_of(ci * c, c)
        q = q_ref[pl.ds(r, c), :] * (dk ** -0.5)
        k = k_ref[pl.ds(r, c), :]
        v = v_ref[pl.ds(r, c), :]
        ga = misc_ref[pl.ds(r, c), :][:, :GLA_RANK]
        x = jnp.dot(ga.astype(bf16), wa2_ref[...], preferred_element_type=f32) + ba_ref[...]
        loga = (jnp.minimum(x, 0.0) - jnp.log1p(jnp.exp(-jnp.abs(x)))) / GLA_TAU
        b = loga
        d = 1
        while d < c:
            b = b + jnp.where(row_c >= d, pltpu.roll(b, d, 0), 0.0)
            d *= 2
        st = st_ref[...]
        vb = v.astype(bf16)
        o_inter = lax.dot_general((q * jnp.exp(b)).astype(bf16), st.astype(bf16), NT,
                                  preferred_element_type=f32)
        att = jnp.zeros((c, c), f32)
        for delta in range(GLA_SUB):
            if delta == 0:
                w = q * k
            else:
                w = q * pltpu.roll(k, delta, 0) * jnp.exp(jnp.minimum(b - pltpu.roll(b, delta, 0), 0.0))
            col = jnp.sum(w, axis=-1, keepdims=True)
            att = jnp.where(jnp.logical_and(lane_a == row_a - delta, sub_a >= delta), col, att)
        if nsub > 1:
            offs = [jnp.zeros((GLA_SUB, c), f32)]
            for i in range(1, nsub):
                r0 = i * GLA_SUB
                bref = b[r0 - 1:r0]
                qq = q[r0:r0 + GLA_SUB] * jnp.exp(b[r0:r0 + GLA_SUB] - bref)
                kk = jnp.where(row_c < r0, k * jnp.exp(jnp.minimum(bref - b, 0.0)), 0.0)
                offs.append(lax.dot_general(qq.astype(bf16), kk.astype(bf16), NT, preferred_element_type=f32))
            att = att + jnp.concatenate(offs, axis=0)
        o = o_inter + jnp.dot(att.astype(bf16), vb, preferred_element_type=f32)
        b_last = b[c - 1:c]
        kd = k * jnp.exp(b_last - b)
        st_ref[...] = jnp.exp(b_last) * st + lax.dot_general(vb, kd.astype(bf16), TN, preferred_element_type=f32)
        y = o * lax.rsqrt(jnp.mean(o * o, axis=-1, keepdims=True) + EPS) * gn_ref[...]
        gg = gg_ref[pl.ds(r, c), :]
        y = y * (gg * (1.0 / (1.0 + jnp.exp(-gg))))
        o_ref[pl.ds(r, c), :] = y.astype(o_ref.dtype)
        return carry

    lax.fori_loop(0, n_inner, chunk, 0)

    @pl.when(step == pl.num_programs(2) - 1)
    def _():
        sout_ref[0, 0] = st_ref[...].T


def gla(z, misc, s0, sb0, wa2, ba, gn, *, n_seq, T, heads, dk, dv, cols):
    c = min(CHUNK, T)
    tb = max(_pick(T, (256, 128, 64, 32, 16)), c)
    n_steps = T // tb
    cq, ck, cv, cg = cols
    in_specs = [
        pl.BlockSpec((tb, dk), lambda b, h, s: (b * n_steps + s, cq // dk + h)),
        pl.BlockSpec((tb, dk), lambda b, h, s: (b * n_steps + s, ck // dk + h)),
        pl.BlockSpec((tb, dv), lambda b, h, s: (b * n_steps + s, cv // dv + h)),
        pl.BlockSpec((tb, dv), lambda b, h, s: (b * n_steps + s, cg // dv + h)),
        pl.BlockSpec((tb, LANE), lambda b, h, s: (b * n_steps + s, 0)),
        pl.BlockSpec((GLA_RANK, dk), lambda b, h, s: (0, h)),
        pl.BlockSpec((1, dk), lambda b, h, s: (0, h)),
        pl.BlockSpec((1, dv), lambda b, h, s: (0, 0)),
        pl.BlockSpec((1, 1, dk, dv), lambda b, h, s: (sb0 + b, h, 0, 0)),
    ]
    out_specs = [
        pl.BlockSpec((tb, dv), lambda b, h, s: (b * n_steps + s, h)),
        pl.BlockSpec((1, 1, dk, dv), lambda b, h, s: (b, h, 0, 0)),
    ]
    return pl.pallas_call(
        functools.partial(_gla_kernel, c=c, n_inner=tb // c, dk=dk, dv=dv),
        grid=(n_seq, heads, n_steps),
        in_specs=in_specs,
        out_specs=out_specs,
        out_shape=[jax.ShapeDtypeStruct((n_seq * T, heads * dv), bf16),
                   jax.ShapeDtypeStruct((n_seq, heads, dk, dv), f32)],
        scratch_shapes=[pltpu.VMEM((dv, dk), f32)],
        compiler_params=_cparams(("parallel", "parallel", "arbitrary")),
        name="gla",
    )(z, z, z, z, misc, wa2, ba, gn, s0)


def _dsa_kernel(dq_ref, iq_ref, misc_ref, k_ref, v_ref, ik_ref, o_ref, key_s, q_s, m_s, l_s, acc_s,
                *, tq, kb, s_valid, pos0, topk, heads, kv_heads, idx_heads):
    j = pl.program_id(1)
    qpos0 = pos0 + j * tq
    vis_end = jnp.minimum(((qpos0 + tq - 1) // CHUNK + 1) * CHUNK, s_valid)
    nkb = (vis_end + kb - 1) // kb
    qchunk = (qpos0 + lax.broadcasted_iota(jnp.int32, (tq, 1), 0)) // CHUNK
    w = misc_ref[...][:, GLA_RANK:GLA_RANK + idx_heads]
    group = heads // kv_heads

    def score_body(b, carry):
        c0 = pl.multiple_of(b * kb, kb)
        ikb = ik_ref[0, pl.ds(c0, kb), :]
        acc = jnp.zeros((tq, kb), f32)
        for h in range(idx_heads):
            s = lax.dot_general(iq_ref[:, h * IDX_HD:(h + 1) * IDX_HD], ikb, NT, preferred_element_type=f32)
            acc = acc + w[:, h:h + 1] * jnp.maximum(s, 0.0)
        kpos = c0 + lax.broadcasted_iota(jnp.int32, (1, kb), 1)
        vis = jnp.logical_and(kpos // CHUNK <= qchunk, kpos < s_valid)
        bits = pltpu.bitcast(acc, jnp.int32)
        key = bits ^ ((bits >> 31) & 0x7FFFFFFF)
        key_s[:, pl.ds(c0, kb)] = jnp.where(vis, key, INT_MIN)
        return carry

    lax.fori_loop(0, nkb, score_body, 0)

    one, zero = jnp.int32(1), jnp.int32(0)
    izeros = jnp.zeros((tq, LANE), jnp.int32)
    lane_i = lax.broadcasted_iota(jnp.int32, (1, LANE), 1)

    def count(pred):
        def body(b, cacc):
            c0 = pl.multiple_of(b * kb, kb)
            blk = key_s[:, pl.ds(c0, kb)]
            for t in range(kb // LANE):
                cacc = cacc + jnp.where(pred(blk[:, t * LANE:(t + 1) * LANE], c0 + t * LANE + lane_i), one, zero)
            return cacc
        return jnp.sum(lax.fori_loop(0, nkb, body, izeros), axis=-1, keepdims=True)

    def bit_body(i, carry):
        t_u, c_at = carry
        cand_u = t_u | lax.shift_left(one, 31 - i)
        cand = cand_u ^ INT_MIN
        cnt = count(lambda kt, it: kt >= cand)
        ok = cnt >= topk
        return jnp.where(ok, cand_u, t_u), jnp.where(ok, cnt, c_at)

    t_u, c_at = lax.fori_loop(0, 32, bit_body, (izeros, izeros))
    thr = jnp.maximum(t_u ^ INT_MIN, INT_MIN + 1)

    @pl.when(jnp.max(c_at) > topk)
    def _():
        need = topk - count(lambda kt, it: kt > thr)
        nbits = int(s_valid).bit_length()

        def idx_body(i, lo):
            cand = lo | lax.shift_left(one, nbits - 1 - i)
            c = count(lambda kt, it: jnp.logical_and(kt == thr, it < cand))
            return jnp.where(c < need, cand, lo)

        lo = lax.fori_loop(0, nbits, idx_body, izeros)

        def drop_body(b, carry):
            c0 = pl.multiple_of(b * kb, kb)
            blk = key_s[:, pl.ds(c0, kb)]
            tiles = []
            for t in range(kb // LANE):
                kt = blk[:, t * LANE:(t + 1) * LANE]
                drop = jnp.logical_and(kt == thr, c0 + t * LANE + lane_i > lo)
                tiles.append(jnp.where(drop, INT_MIN, kt))
            key_s[:, pl.ds(c0, kb)] = jnp.concatenate(tiles, axis=1)
            return carry

        lax.fori_loop(0, nkb, drop_body, 0)

    c2 = DSA_HD ** -0.5 * LOG2E
    for g in range(kv_heads):
        for i in range(group):
            hh = g * group + i
            q_s[g, i * tq:(i + 1) * tq, :] = dq_ref[:, hh * DSA_HD:(hh + 1) * DSA_HD]
    m_s[...] = jnp.full(m_s.shape, NEG_INIT, f32)
    l_s[...] = jnp.zeros(l_s.shape, f32)
    acc_s[...] = jnp.zeros(acc_s.shape, f32)

    def kv_body(b, carry):
        c0 = pl.multiple_of(b * kb, kb)
        keys = key_s[:, pl.ds(c0, kb)]
        bias = jnp.concatenate(
            [jnp.where(keys[:, t * LANE:(t + 1) * LANE] >= thr, 0.0, NEG_MASK) for t in range(kb // LANE)],
            axis=1)
        bias = jnp.concatenate([bias] * group, axis=0)
        for g in range(kv_heads):
            kblk = k_ref[0, pl.ds(c0, kb), g * DSA_HD:(g + 1) * DSA_HD]
            vblk = v_ref[0, pl.ds(c0, kb), g * DSA_HD:(g + 1) * DSA_HD]
            s = lax.dot_general(q_s[g], kblk, NT, preferred_element_type=f32) * c2 + bias
            m_old = m_s[g]
            m_new = jnp.maximum(m_old, jnp.max(s, axis=-1, keepdims=True))
            p = jnp.exp2(s - m_new)
            alpha = jnp.exp2(m_old - m_new)
            l_s[g] = alpha * l_s[g] + jnp.sum(p, axis=-1, keepdims=True)
            acc_s[g] = alpha * acc_s[g] + jnp.dot(p.astype(bf16), vblk, preferred_element_type=f32)
            m_s[g] = m_new
        return carry

    lax.fori_loop(0, nkb, kv_body, 0)
    for g in range(kv_heads):
        out = acc_s[g] / l_s[g]
        for i in range(group):
            hh = g * group + i
            o_ref[:, hh * DSA_HD:(hh + 1) * DSA_HD] = out[i * tq:(i + 1) * tq].astype(o_ref.dtype)


def dsa(dq, iq, misc, k_all, v_all, ik_all, *, n_seq, T, s_valid, pos0, heads, kv_heads, idx_heads, tq, kb):
    s_pad = k_all.shape[1]
    nq = T // tq
    topk = min(IDX_TOPK, s_valid // 4)
    group = heads // kv_heads
    kern = functools.partial(_dsa_kernel, tq=tq, kb=kb, s_valid=s_valid, pos0=pos0, topk=topk, heads=heads,
                             kv_heads=kv_heads, idx_heads=idx_heads)
    return pl.pallas_call(
        kern,
        grid=(n_seq, nq),
        in_specs=[
            pl.BlockSpec((tq, heads * DSA_HD), lambda b, j: (b * nq + j, 0)),
            pl.BlockSpec((tq, idx_heads * IDX_HD), lambda b, j: (b * nq + j, 0)),
            pl.BlockSpec((tq, LANE), lambda b, j: (b * nq + j, 0)),
            pl.BlockSpec((1, s_pad, kv_heads * DSA_HD), lambda b, j: (b, 0, 0)),
            pl.BlockSpec((1, s_pad, kv_heads * DSA_HD), lambda b, j: (b, 0, 0)),
            pl.BlockSpec((1, s_pad, IDX_HD), lambda b, j: (b, 0, 0)),
        ],
        out_specs=pl.BlockSpec((tq, heads * DSA_HD), lambda b, j: (b * nq + j, 0)),
        out_shape=jax.ShapeDtypeStruct((n_seq * T, heads * DSA_HD), bf16),
        scratch_shapes=[pltpu.VMEM((tq, s_pad), jnp.int32), pltpu.VMEM((kv_heads, group * tq, DSA_HD), bf16),
                        pltpu.VMEM((kv_heads, group * tq, 1), f32), pltpu.VMEM((kv_heads, group * tq, 1), f32),
                        pltpu.VMEM((kv_heads, group * tq, DSA_HD), f32)],
        compiler_params=_cparams(("parallel", "arbitrary")),
        name="dsa",
    )(dq, iq, misc, k_all, v_all, ik_all)


def _diff_lambda(lq1, lk1, lq2, lk2, lam_init):
    return (jnp.exp(jnp.sum(lq1[...] * lk1[...], axis=-1, keepdims=True))
            - jnp.exp(jnp.sum(lq2[...] * lk2[...], axis=-1, keepdims=True)) + lam_init)


def _diff_finish(o, gn_ref, lam_init):
    y = o * lax.rsqrt(jnp.mean(o * o, axis=-1, keepdims=True) + EPS) * gn_ref[...]
    return y * (1.0 - lam_init)


def _diff_prompt_kernel(qt_ref, kt_ref, q_ref, k_ref, v_ref, lq1, lk1, lq2, lk2, gn_ref, o_ref, m_s, l_s, acc_s,
                        *, tq, tk, lam_init):
    p_id = pl.program_id(2)
    qi = qt_ref[p_id]
    ki = kt_ref[p_id]
    last_k = ((qi + 1) * tq - 1) // tk
    c2 = DIFF_HD ** -0.5 * LOG2E

    @pl.when(ki == 0)
    def _():
        m_s[...] = jnp.full(m_s.shape, NEG_INIT, f32)
        l_s[...] = jnp.zeros(l_s.shape, f32)
        acc_s[...] = jnp.zeros(acc_s.shape, f32)

    def update(masked):
        vb = v_ref[...]
        if masked:
            qc = (qi * tq + lax.broadcasted_iota(jnp.int32, (tq, 1), 0)) // CHUNK
            kc = (ki * tk + lax.broadcasted_iota(jnp.int32, (1, tk), 1)) // CHUNK
            bias = jnp.where(kc <= qc, 0.0, NEG_MASK)
        for c in range(2):
            s = lax.dot_general(q_ref[:, c * DIFF_HD:(c + 1) * DIFF_HD], k_ref[:, c * DIFF_HD:(c + 1) * DIFF_HD],
                                NT, preferred_element_type=f32) * c2
            if masked:
                s = s + bias
            m_old = m_s[c]
            m_new = jnp.maximum(m_old, jnp.max(s, axis=-1, keepdims=True))
            p = jnp.exp2(s - m_new)
            alpha = jnp.exp2(m_old - m_new)
            l_s[c] = alpha * l_s[c] + jnp.sum(p, axis=-1, keepdims=True)
            acc_s[c] = alpha * acc_s[c] + jnp.dot(p.astype(bf16), vb, preferred_element_type=f32)
            m_s[c] = m_new

    needs_mask = (ki + 1) * tk > qi * tq + CHUNK

    @pl.when(needs_mask)
    def _():
        update(True)

    @pl.when(jnp.logical_not(needs_mask))
    def _():
        update(False)

    @pl.when(ki == last_k)
    def _():
        lam = _diff_lambda(lq1, lk1, lq2, lk2, lam_init)
        o = acc_s[0] / l_s[0] - lam * (acc_s[1] / l_s[1])
        o_ref[...] = _diff_finish(o, gn_ref, lam_init).astype(o_ref.dtype)


def diff_prompt(q, k, v, lams, gn, *, n_seq, T, heads, lam_init):
    hw = 2 * DIFF_HD
    tq = _pick(T, (DIFF_TQ, 512, 256, 128))
    tk = _pick(T, (DIFF_TK, 512, 256, 128))
    nq, nk = T // tq, T // tk
    pairs = [(qi, ki) for qi in range(nq) for ki in range(((qi + 1) * tq - 1) // tk + 1)]
    qt = jnp.asarray(np.array([p[0] for p in pairs], np.int32))
    kt = jnp.asarray(np.array([p[1] for p in pairs], np.int32))
    vec = pl.BlockSpec((1, DIFF_HD), lambda b, h, p, qt, kt: (0, 0))
    kv = pl.BlockSpec((tk, hw), lambda b, h, p, qt, kt: (b * nk + kt[p], h))
    qo = pl.BlockSpec((tq, hw), lambda b, h, p, qt, kt: (b * nq + qt[p], h))
    return pl.pallas_call(
        functools.partial(_diff_prompt_kernel, tq=tq, tk=tk, lam_init=lam_init),
        grid_spec=pltpu.PrefetchScalarGridSpec(
            num_scalar_prefetch=2,
            grid=(n_seq, heads, len(pairs)),
            in_specs=[qo, kv, kv, vec, vec, vec, vec, pl.BlockSpec((1, hw), lambda b, h, p, qt, kt: (0, 0))],
            out_specs=qo,
            scratch_shapes=[pltpu.VMEM((2, tq, 1), f32), pltpu.VMEM((2, tq, 1), f32), pltpu.VMEM((2, tq, hw), f32)],
        ),
        out_shape=jax.ShapeDtypeStruct((n_seq * T, heads * hw), bf16),
        compiler_params=_cparams(("parallel", "parallel", "arbitrary")),
        name="diff_prompt",
    )(qt, kt, q, k, v, *lams, gn)


def _diff_sample_kernel(q_ref, kn_ref, vn_ref, k0_ref, k1_ref, vlo_ref, vhi_ref, lq1, lk1, lq2, lk2, gn_ref, o_ref,
                        m_s, l_s, acc_s, flat_s, *, tk, hpb, lam_init):
    kb = pl.program_id(2)
    T = q_ref.shape[0]
    hw = 2 * DIFF_HD
    c2 = DIFF_HD ** -0.5 * LOG2E

    @pl.when(kb == 0)
    def _():
        m_s[...] = jnp.full(m_s.shape, NEG_INIT, f32)
        l_s[...] = jnp.zeros(l_s.shape, f32)
        acc_s[...] = jnp.zeros(acc_s.shape, f32)

    def update(hl, kblks, vblk):
        ps = []
        for c in range(2):
            idx = 2 * hl + c
            s = lax.dot_general(q_ref[:, idx * DIFF_HD:(idx + 1) * DIFF_HD], kblks[c], NT,
                                preferred_element_type=f32) * c2
            m_old = m_s[idx]
            m_new = jnp.maximum(m_old, jnp.max(s, axis=-1, keepdims=True))
            p = jnp.exp2(s - m_new)
            alpha = jnp.exp2(m_old - m_new)
            l_s[idx] = alpha * l_s[idx] + jnp.sum(p, axis=-1, keepdims=True)
            m_s[idx] = m_new
            acc_s[idx] = alpha * acc_s[idx]
            ps.append(p.astype(bf16))
        pv = jnp.dot(jnp.concatenate(ps, axis=0), vblk, preferred_element_type=f32)
        acc_s[2 * hl] = acc_s[2 * hl] + pv[:T]
        acc_s[2 * hl + 1] = acc_s[2 * hl + 1] + pv[T:]

    for n, ref in enumerate((k0_ref, k1_ref, vlo_ref, vhi_ref)):
        flat_s[n] = ref[...].reshape(tk * 8, DIFF_HD)

    def rows(n, r):
        return flat_s[n, pl.ds(r, tk, stride=8), :]

    for hl in range(hpb):
        quad, j0 = hl // 4, (hl % 4) * 2
        kblks = tuple(rows(quad, j0 + c).astype(bf16) for c in range(2))
        vblk = jnp.concatenate([rows(2, hl), rows(3, hl)], axis=1).astype(bf16)
        update(hl, kblks, vblk)

    @pl.when(kb == pl.num_programs(2) - 1)
    def _():
        lam = _diff_lambda(lq1, lk1, lq2, lk2, lam_init)
        for hl in range(hpb):
            kblks = tuple(kn_ref[:, (2 * hl + c) * DIFF_HD:(2 * hl + c + 1) * DIFF_HD] for c in range(2))
            update(hl, kblks, vn_ref[:, hl * hw:(hl + 1) * hw])
            o = acc_s[2 * hl] / l_s[2 * hl] - lam * (acc_s[2 * hl + 1] / l_s[2 * hl + 1])
            o_ref[:, hl * hw:(hl + 1) * hw] = _diff_finish(o, gn_ref, lam_init).astype(o_ref.dtype)


def diff_sample(q, k, v, k_past, v_past, pb0, lams, gn, *, n_seq, T, heads, lam_init):
    hw = 2 * DIFF_HD
    hpb = 8
    past = k_past.shape[1]
    assert (past + T - 1) // CHUNK == past // CHUNK and past % CHUNK == 0
    assert heads % hpb == 0
    tk = _pick(past, (DIFF_SAMPLE_TK, 512, 256, 128))
    new = pl.BlockSpec((T, hpb * hw), lambda b, o, kb: (b, o))
    vec = pl.BlockSpec((1, DIFF_HD), lambda b, o, kb: (0, 0))
    tile = (None, tk, 8, DIFF_HD)
    return pl.pallas_call(
        functools.partial(_diff_sample_kernel, tk=tk, hpb=hpb, lam_init=lam_init),
        grid=(n_seq, heads // hpb, past // tk),
        in_specs=[new, new, new,
                  pl.BlockSpec(tile, lambda b, o, kb: (pb0 + b, kb, 2 * o, 0)),
                  pl.BlockSpec(tile, lambda b, o, kb: (pb0 + b, kb, 2 * o + 1, 0)),
                  pl.BlockSpec(tile, lambda b, o, kb: (pb0 + b, kb, o, 0)),
                  pl.BlockSpec(tile, lambda b, o, kb: (pb0 + b, kb, o, 1)),
                  vec, vec, vec, vec, pl.BlockSpec((1, hw), lambda b, o, kb: (0, 0))],
        out_specs=pl.BlockSpec((T, hpb * hw), lambda b, o, kb: (b, o)),
        out_shape=jax.ShapeDtypeStruct((n_seq * T, heads * hw), bf16),
        scratch_shapes=[pltpu.VMEM((2 * hpb, T, 1), f32), pltpu.VMEM((2 * hpb, T, 1), f32),
                        pltpu.VMEM((2 * hpb, T, hw), f32), pltpu.VMEM((4, tk * 8, DIFF_HD), f32)],
        compiler_params=_cparams(("parallel", "parallel", "arbitrary")),
        name="diff_sample",
    )(q, k, v, k_past, k_past, v_past, v_past, *lams, gn)


def _ffn_up_kernel(*refs, T, tm, tn, tiled_seq, emit_w):
    it = iter(refs)
    x_ref, wa_ref, wg_ref = next(it), next(it), next(it)
    if tiled_seq:
        prev_a, prev_g = (next(it),), (next(it),)
    else:
        prev_a = (next(it), next(it))
        prev_g = (next(it), next(it))
    cwa_ref, cwg_ref, cba_ref, cbg_ref = next(it), next(it), next(it), next(it)
    o_ref, ta_ref, tg_ref = next(it), next(it), next(it)
    wb_refs = (next(it), next(it)) if emit_w else None
    was_s, wgs_s, carry_s = next(it), next(it), next(it)
    i = pl.program_id(1)

    @pl.when(i == 0)
    def _():
        was_s[...] = wa_ref[...].astype(bf16)
        wgs_s[...] = wg_ref[...].astype(bf16)
        if emit_w:
            wb_refs[0][...] = was_s[...]
            wb_refs[1][...] = wgs_s[...]

    x = x_ref[...]
    row = lax.broadcasted_iota(jnp.int32, (tm, tn), 0)

    def half(idx, ws_s, prevs, cw_ref, cb_ref, tail_ref):
        u = jnp.dot(x, ws_s[...], preferred_element_type=f32)
        if tiled_seq:
            first = (i % (T // tm)) == 0
            c0 = jnp.where(first, prevs[0][0, 0:1, :], carry_s[idx, 0:1, :])
            c1 = jnp.where(first, prevs[0][0, 1:2, :], carry_s[idx, 1:2, :])
            u1 = jnp.where(row == 0, c1, pltpu.roll(u, 1, 0))
            u2 = jnp.where(row == 0, c0, jnp.where(row == 1, c1, pltpu.roll(u, 2, 0)))
            carry_s[idx] = u[tm - 2:tm]
            tail_ref[0] = u[tm - 2:tm]
        else:
            rmod = row % T
            p1 = prevs[1][...]
            u1 = jnp.where(rmod == 0, p1, pltpu.roll(u, 1, 0))
            u2 = jnp.where(rmod == 0, prevs[0][...], jnp.where(rmod == 1, pltpu.roll(p1, 1, 0), pltpu.roll(u, 2, 0)))
            tail_ref[...] = u
        return cb_ref[...] + u2 * cw_ref[0:1, :] + u1 * cw_ref[1:2, :] + u * cw_ref[2:3, :]

    a = half(0, was_s, prev_a, cwa_ref, cba_ref, ta_ref)
    g = half(1, wgs_s, prev_g, cwg_ref, cbg_ref, tg_ref)
    o_ref[...] = (g * (1.0 / (1.0 + jnp.exp(-g))) * a).astype(o_ref.dtype)


def ffn_up(x, w, prev, pb0, cw, cb, l, *, n_seq, T, dff):
    M, K = x.shape
    tn = _pick(dff, (256, 128))
    nb = dff // tn
    tm = _pick(M, (FFN_TM, 512, 256, 128, 64, 32, 16))
    tiled_seq = tm <= T
    assert T % tm == 0 if tiled_seq else tm % T == 0
    emit_w = not isinstance(w, tuple)
    if emit_w:
        w_args = [w, w]
        w_specs = [pl.BlockSpec((None, K, tn), lambda j, i: (l, 0, j)),
                   pl.BlockSpec((None, K, tn), lambda j, i: (l, 0, nb + j))]
    else:
        w_args = list(w)
        w_specs = [pl.BlockSpec((K, tn), lambda j, i: (0, j))] * 2
    if tiled_seq:
        spt = T // tm
        p_args = [prev, prev]
        p_specs = [pl.BlockSpec((1, CONV_W - 1, tn), lambda j, i: (pb0 + i // spt, 0, j)),
                   pl.BlockSpec((1, CONV_W - 1, tn), lambda j, i: (pb0 + i // spt, 0, nb + j))]
        tail_shape = jax.ShapeDtypeStruct((n_seq, CONV_W - 1, dff), f32)
        tail_spec = pl.BlockSpec((1, CONV_W - 1, tn), lambda j, i: (i // spt, 0, j))
    else:
        pv = prev[pb0:pb0 + n_seq]
        ex = [jnp.pad(pv[:, k:k + 1], ((0, 0), (0, T - 1), (0, 0))).reshape(M, 2 * dff) for k in range(CONV_W - 1)]
        p_args = [ex[0], ex[1], ex[0], ex[1]]
        p_specs = [pl.BlockSpec((tm, tn), lambda j, i: (i, j)), pl.BlockSpec((tm, tn), lambda j, i: (i, j)),
                   pl.BlockSpec((tm, tn), lambda j, i: (i, nb + j)), pl.BlockSpec((tm, tn), lambda j, i: (i, nb + j))]
        tail_shape = jax.ShapeDtypeStruct((M, dff), f32)
        tail_spec = pl.BlockSpec((tm, tn), lambda j, i: (i, j))
    c_specs = [pl.BlockSpec((None, CONV_W, tn), lambda j, i: (l, 0, j)),
               pl.BlockSpec((None, CONV_W, tn), lambda j, i: (l, 0, nb + j)),
               pl.BlockSpec((None, 1, tn), lambda j, i: (l, 0, j)),
               pl.BlockSpec((None, 1, tn), lambda j, i: (l, 0, nb + j))]
    cb3 = cb.reshape(cb.shape[0], 1, 2 * dff)
    out_shape = [jax.ShapeDtypeStruct((M, dff), bf16), tail_shape, tail_shape]
    out_specs = [pl.BlockSpec((tm, tn), lambda j, i: (i, j)), tail_spec, tail_spec]
    if emit_w:
        out_shape += [jax.ShapeDtypeStruct((K, dff), bf16)] * 2
        out_specs += [pl.BlockSpec((K, tn), lambda j, i: (0, j))] * 2
    outs = pl.pallas_call(
        functools.partial(_ffn_up_kernel, T=T, tm=tm, tn=tn, tiled_seq=tiled_seq, emit_w=emit_w),
        grid=(nb, M // tm),
        in_specs=[pl.BlockSpec((tm, K), lambda j, i: (i, 0))] + w_specs + p_specs + c_specs,
        out_specs=out_specs,
        out_shape=out_shape,
        scratch_shapes=[pltpu.VMEM((K, tn), bf16), pltpu.VMEM((K, tn), bf16), pltpu.VMEM((2, CONV_W - 1, tn), f32)],
        compiler_params=_cparams(("parallel", "arbitrary")),
        name="ffn_up",
    )(x, *w_args, *p_args, cw, cw, cb3, cb3)
    g, ta, tg = outs[:3]
    if tiled_seq:
        state = jnp.concatenate([ta, tg], axis=-1)
    else:
        state = jnp.concatenate([ta.reshape(n_seq, T, dff)[:, T - (CONV_W - 1):],
                                 tg.reshape(n_seq, T, dff)[:, T - (CONV_W - 1):]], axis=-1)
    return g, state, (tuple(outs[3:]) if emit_w else None)


class _Stream:
    def __init__(self, x, pos0, past):
        self.n_seq, self.T, d = x.shape
        self.h = x.reshape(self.n_seq * self.T, d)
        self.pos0 = pos0
        self.past = past
        pos = jnp.tile(pos0 + jnp.arange(self.T, dtype=jnp.int32), self.n_seq)
        self.tables = rope_tables(pos)


def kernel(x_prompt, x_sample, cache_gla_state, cache_dsa_k, cache_dsa_v, cache_idx_k, cache_diff_k, cache_diff_v, state_ffn_conv, norm_mix, norm_ffn, norm_final, w_in_ab, w_gla_a2, b_gla_a, gla_norm, w_out_ab, w_in_c, lambda_q1, lambda_k1, lambda_q2, lambda_k2, diff_norm, w_out_c, w_up, conv_w, conv_b, w_down):
    B, S, D = x_prompt.shape
    DB, DS, _ = x_sample.shape
    depth = norm_mix.shape[0]
    n_ab, _, gh, gdk, gdv = cache_gla_state.shape
    n_c = cache_diff_k.shape[0]
    past = cache_dsa_k.shape[2]
    kvh = cache_dsa_k.shape[3]
    kvw = kvh * DSA_HD
    dh = w_out_ab.shape[1] - gh * gdv
    dsa_heads = dh // DSA_HD
    idx_heads = (w_in_ab.shape[2] - (2 * gh * gdk + 2 * gh * gdv + GLA_RANK + dh + 2 * kvw + IDX_HD)) // (IDX_HD + 1)
    diff_heads = cache_diff_k.shape[3]
    dff = w_down.shape[1]
    cw_ = diff_heads * 2 * DIFF_HD

    streams = (_Stream(x_prompt, 0, 0), _Stream(x_sample, past, past))
    names = ("gla", "dk", "dv", "ik", "ck", "cv", "conv")
    outs = [{k: [] for k in names} for _ in streams]

    c_gla = cache_gla_state.reshape(n_ab * DB, gh, gdk, gdv)
    c_dk = cache_dsa_k.reshape(n_ab, DB, past, kvw)
    c_dv = cache_dsa_v.reshape(n_ab, DB, past, kvw)
    c_ik = cache_idx_k.reshape(n_ab, DB, past, IDX_HD)
    c_ck = cache_diff_k.reshape(n_c * DB, past, 2 * diff_heads, DIFF_HD)
    c_cv = cache_diff_v.reshape(n_c * DB, past, diff_heads, 2 * DIFF_HD)
    c_conv = state_ffn_conv.reshape(depth * DB, CONV_W - 1, 2 * dff)
    zero_state = jnp.zeros((B, gh, gdk, gdv), f32)
    zero_conv = jnp.zeros((B, CONV_W - 1, 2 * dff), f32)
    wd_all = w_down.astype(bf16)

    for l in range(depth):
        i = l // 2
        if l % 2 == 0:
            sizes = (gh * gdk, gh * gdk, gh * gdv, gh * gdv, GLA_RANK, dh, kvw, kvw, idx_heads * IDX_HD, IDX_HD,
                     idx_heads)
            offs = np.concatenate([[0], np.cumsum(sizes)]).tolist()
            tn_ab = kvw
            n_body = offs[10] - GLA_RANK
            n_out = -(-(n_body + LANE) // tn_ab) * tn_ab
            w_perm = prep_w_ab(w_in_ab, i, c_ga=offs[4], c_end_ik=offs[10], n_out=n_out)
            ab_segs = [(offs[4], False, (f32,)), (dh, True, (bf16,)), (kvw, True, (f32, bf16)),
                       (kvw, False, (f32, bf16)), (idx_heads * IDX_HD, True, (bf16,)),
                       (tn_ab, [(IDX_HD, True, (f32, bf16)), (LANE, False, (f32,))])]
            c_gq = 0
            c_gk = c_gq + gh * gdk
            c_gv = c_gk + gh * gdk
            c_gg = c_gv + gh * gdv
            wa2 = w_gla_a2[i].astype(bf16)
            ba = b_gla_a[i].reshape(1, gh * gdk)
            gn = gla_norm[i].reshape(1, gdv)
            w_out = w_out_ab[i].astype(bf16)
            for si, st in enumerate(streams):
                hn = rmsnorm(st.h, norm_mix[l], bf16)
                zg, dq_b, dk_f, dk_b, dv_f, dv_b, iq_b, ik_f, ik_b, misc = proj(hn, w_perm, st.tables, ab_segs, tn_ab,
                                                                                 "in_ab")
                if st.past:
                    s0, sb0 = c_gla, i * DB
                else:
                    s0, sb0 = zero_state, 0
                go, gst = gla(zg, misc, s0, sb0, wa2, ba, gn, n_seq=st.n_seq, T=st.T, heads=gh, dk=gdk, dv=gdv,
                              cols=(c_gq, c_gk, c_gv, c_gg))
                s_valid = st.past + st.T
                if st.past:
                    tq, kb = st.T, DSA_KB_SAMPLE
                    s_pad = -(-s_valid // kb) * kb

                    def cat(cache, new, width):
                        return jnp.concatenate([cache.astype(bf16), new.reshape(st.n_seq, st.T, width),
                                                jnp.zeros((st.n_seq, s_pad - s_valid, width), bf16)], axis=1)

                    k_all, v_all, ik_all = cat(c_dk[i], dk_b, kvw), cat(c_dv[i], dv_b, kvw), cat(c_ik[i], ik_b, IDX_HD)
                else:
                    tq, kb = _pick(st.T, (DSA_TQ,)), _pick(st.T, (DSA_KB, 256, 128))
                    k_all = dk_b.reshape(st.n_seq, st.T, kvw)
                    v_all = dv_b.reshape(st.n_seq, st.T, kvw)
                    ik_all = ik_b.reshape(st.n_seq, st.T, IDX_HD)
                do = dsa(dq_b, iq_b, misc, k_all, v_all, ik_all, n_seq=st.n_seq, T=st.T, s_valid=s_valid, pos0=st.pos0,
                         heads=dsa_heads, kv_heads=kvh, idx_heads=idx_heads, tq=tq, kb=kb)
                st.h = matmul((go, do), w_out, f32, res=st.h, name="out_ab")
                o = outs[si]
                o["gla"].append(gst)
                o["dk"].append(dk_f.reshape(st.n_seq, st.T, kvh, DSA_HD))
                o["dv"].append(dv_f.reshape(st.n_seq, st.T, kvh, DSA_HD))
                o["ik"].append(ik_f.reshape(st.n_seq, st.T, IDX_HD))
        else:
            lam_init = 0.8 - 0.6 * math.exp(-0.3 * l)
            w_in = w_in_c[i].astype(bf16)
            w_out = w_out_c[i].astype(bf16)
            lams = tuple(a[i].reshape(1, DIFF_HD) for a in (lambda_q1, lambda_k1, lambda_q2, lambda_k2))
            gn = diff_norm[i].reshape(1, 2 * DIFF_HD)
            c_segs = [(cw_, True, (bf16,)), (cw_, True, (f32, bf16)), (cw_, False, (f32, bf16))]
            for si, st in enumerate(streams):
                hn = rmsnorm(st.h, norm_mix[l], bf16)
                q_b, k_f, k_b, v_f, v_b = proj(hn, w_in, st.tables, c_segs, min(512, cw_), "in_c")
                if st.past:
                    at = diff_sample(q_b, k_b, v_b, c_ck, c_cv, i * DB, lams, gn, n_seq=st.n_seq, T=st.T,
                                     heads=diff_heads, lam_init=lam_init)
                else:
                    at = diff_prompt(q_b, k_b, v_b, lams, gn, n_seq=st.n_seq, T=st.T, heads=diff_heads,
                                     lam_init=lam_init)
                st.h = matmul(at, w_out, f32, res=st.h, name="out_c")
                o = outs[si]
                o["ck"].append(k_f.reshape(st.n_seq, st.T, diff_heads, 2, DIFF_HD))
                o["cv"].append(v_f.reshape(st.n_seq, st.T, diff_heads, 2 * DIFF_HD))

        wu = w_up
        for si, st in enumerate(streams):
            hn = rmsnorm(st.h, norm_ffn[l], bf16)
            prev, pb0 = (c_conv, l * DB) if st.past else (zero_conv, 0)
            g, conv_state, wu_b = ffn_up(hn, wu, prev, pb0, conv_w, conv_b, l, n_seq=st.n_seq, T=st.T, dff=dff)
            if wu_b is not None:
                wu = wu_b
            st.h = matmul(g, wd_all, f32, res=st.h, name="ffn_down", layer=l)
            outs[si]["conv"].append(conv_state)

    ys = [rmsnorm(st.h, norm_final, f32).reshape(st.n_seq, st.T, D) for st in streams]
    tail = [jnp.stack(o[k]) for o in outs for k in names]
    return (ys[0], ys[1], *tail)
```

```python
import functools
import math

import numpy as np
import jax
import jax.numpy as jnp
from jax import lax
from jax.experimental import pallas as pl
from jax.experimental.pallas import tpu as pltpu

CHUNK = 64
EPS = 1e-6
ROPE_THETA = 10000.0
GLA_RANK = 16
GLA_TAU = 16.0
DSA_HD = 128
IDX_HD = 128
IDX_TOPK = 256
DIFF_HD = 128
CONV_W = 3

LANE = 128
GLA_SUB = 16
DSA_TQ, DSA_KB = 128, 1024
DSA_KB_SAMPLE = 512
DIFF_TQ, DIFF_TK = 1024, 1024
FFN_TM = 1024
DIFF_SAMPLE_TK = 512
VMEM_LIMIT = 60 * 1024 * 1024
MM_VMEM_BUDGET = 52 * 1024 * 1024
NEG_INIT = -1e30
NEG_MASK = -3e38
INT_MIN = -2147483648
LOG2E = 1.4426950408889634

f32 = jnp.float32
bf16 = jnp.bfloat16
NT = (((1,), (1,)), ((), ()))
TN = (((0,), (0,)), ((), ()))


def _cparams(sem):
    return pltpu.CompilerParams(dimension_semantics=sem, vmem_limit_bytes=VMEM_LIMIT)


def _pick(dim, cands):
    for c in cands:
        if c <= dim and dim % c == 0:
            return c
    return dim


def _rmsnorm_kernel(x_ref, g_ref, o_ref):
    x = x_ref[...]
    y = x * lax.rsqrt(jnp.mean(x * x, axis=-1, keepdims=True) + EPS)
    o_ref[...] = (y * g_ref[...]).astype(o_ref.dtype)


def rmsnorm(x, g, out_dtype):
    M, D = x.shape
    tm = _pick(M, (256, 128, 64, 32, 16, 8))
    return pl.pallas_call(
        _rmsnorm_kernel,
        grid=(M // tm,),
        in_specs=[pl.BlockSpec((tm, D), lambda i: (i, 0)), pl.BlockSpec((1, D), lambda i: (0, 0))],
        out_specs=pl.BlockSpec((tm, D), lambda i: (i, 0)),
        out_shape=jax.ShapeDtypeStruct((M, D), out_dtype),
        compiler_params=_cparams(("parallel",)),
        name="rmsnorm",
    )(x, g.reshape(1, D).astype(f32))


def _mm_kernel(*refs, n_x, has_res):
    x_refs, w_ref = refs[:n_x], refs[n_x]
    o_ref = refs[-1]
    acc = None
    k0 = 0
    for x_ref in x_refs:
        kw = x_ref.shape[1]
        part = jnp.dot(x_ref[...], w_ref[k0:k0 + kw, :], preferred_element_type=f32)
        acc = part if acc is None else acc + part
        k0 += kw
    if has_res:
        acc = refs[n_x + 1][...] + acc
    o_ref[...] = acc.astype(o_ref.dtype)


def matmul(xs, w, out_dtype, res=None, name="matmul", layer=None):
    xs = tuple(xs) if isinstance(xs, (tuple, list)) else (xs,)
    M = xs[0].shape[0]
    K, N = w.shape[-2:]
    ob = jnp.dtype(out_dtype).itemsize
    best = None
    for tm in (1024, 768, 512, 384, 256, 128, 64, 32, 16, 8):
        if M % tm:
            continue
        for tn in (1024, 768, 512, 384, 256, 128):
            if N % tn:
                continue
            vm = 2 * (tm * K * 2 + K * tn * 2 + tm * tn * ob + (tm * tn * 4 if res is not None else 0))
            if vm > MM_VMEM_BUDGET:
                continue
            score = (tm * tn) / (tm + tn)
            if best is None or score > best[0]:
                best = (score, tm, tn)
    _, tm, tn = best
    in_specs = [pl.BlockSpec((tm, x.shape[1]), lambda i, j: (i, 0)) for x in xs]
    if layer is None:
        in_specs.append(pl.BlockSpec((K, tn), lambda i, j: (0, j)))
    else:
        in_specs.append(pl.BlockSpec((None, K, tn), lambda i, j: (layer, 0, j)))
    args = list(xs) + [w]
    if res is not None:
        in_specs.append(pl.BlockSpec((tm, tn), lambda i, j: (i, j)))
        args.append(res)
    return pl.pallas_call(
        functools.partial(_mm_kernel, n_x=len(xs), has_res=res is not None),
        grid=(M // tm, N // tn),
        in_specs=in_specs,
        out_specs=pl.BlockSpec((tm, tn), lambda i, j: (i, j)),
        out_shape=jax.ShapeDtypeStruct((M, N), out_dtype),
        compiler_params=_cparams(("parallel", "arbitrary")),
        name=name,
    )(*args)


def _proj_kernel(x_ref, w_ref, c_ref, s_ref, *o_refs, segs, tn):
    j = pl.program_id(1)
    acc = jnp.dot(x_ref[...], w_ref[...], preferred_element_type=f32)

    @pl.when(j == 0)
    def _():
        for (j0, j1, rotate, out_ids, widths) in segs:
            if j0 > 0:
                for k in jax.tree_util.tree_leaves(out_ids):
                    o_refs[k][...] = jnp.zeros(o_refs[k].shape, o_refs[k].dtype)

    for (j0, j1, rotate, out_ids, widths) in segs:
        @pl.when(jnp.logical_and(j >= j0, j < j1))
        def _(rotate=rotate, out_ids=out_ids, widths=widths):
            cos, sin = c_ref[...], s_ref[...]
            col = 0
            for part, k_ids in enumerate(out_ids if widths else (out_ids,)):
                wcols = widths[part] if widths else tn
                rot = rotate[part] if widths else rotate
                for g in range(wcols // LANE):
                    y = acc[:, col + g * LANE:col + (g + 1) * LANE]
                    if rot:
                        y = y * cos + pltpu.roll(y, LANE // 2, 1) * sin
                    for k in (k_ids if isinstance(k_ids, tuple) else (k_ids,)):
                        o_refs[k][:, g * LANE:(g + 1) * LANE] = y.astype(o_refs[k].dtype)
                col += wcols


def proj(x, w, tables, segs_cols, tn, name):
    M, K = x.shape
    N = w.shape[1]
    tm = _pick(M, (1024, 768, 512, 256, 128, 64, 32, 16, 8))
    segs, out_shape, out_specs = [], [], []
    j0 = 0
    for sc in segs_cols:
        n_cols = sc[0]
        assert n_cols % tn == 0 or (len(sc) == 2 and n_cols <= tn)
        nj = max(n_cols // tn, 1)
        lo, hi = j0, j0 + nj

        def omap(i, j, lo=lo, hi=hi):
            return (i, jnp.clip(j - lo, 0, hi - lo - 1))

        if len(sc) == 3:
            _, rotate, dtypes = sc
            ids = []
            for dt in dtypes:
                ids.append(len(out_shape))
                out_shape.append(jax.ShapeDtypeStruct((M, n_cols), dt))
                out_specs.append(pl.BlockSpec((tm, tn), omap))
            segs.append((lo, hi, rotate, tuple(ids), None))
        else:
            parts = sc[1]
            ids, widths, rots = [], [], []
            for (width, rotate, dtypes) in parts:
                pid = []
                for dt in dtypes:
                    pid.append(len(out_shape))
                    out_shape.append(jax.ShapeDtypeStruct((M, width), dt))
                    out_specs.append(pl.BlockSpec((tm, width), lambda i, j: (i, 0)))
                ids.append(tuple(pid))
                widths.append(width)
                rots.append(rotate)
            segs.append((lo, hi, tuple(rots), tuple(ids), tuple(widths)))
        j0 = hi
    assert j0 * tn == N, (j0, tn, N)
    tab = pl.BlockSpec((tm, LANE), lambda i, j: (i, 0))
    return pl.pallas_call(
        functools.partial(_proj_kernel, segs=tuple(segs), tn=tn),
        grid=(M // tm, N // tn),
        in_specs=[pl.BlockSpec((tm, K), lambda i, j: (i, 0)), pl.BlockSpec((K, tn), lambda i, j: (0, j)), tab, tab],
        out_specs=out_specs,
        out_shape=out_shape,
        compiler_params=_cparams(("parallel", "arbitrary")),
        name=name,
    )(x, w, *tables)


def _prep_w_ab_kernel(a_ref, b_ref, o_ref, *, j_shift, j_misc, shift):
    j = pl.program_id(0)
    row = lax.broadcasted_iota(jnp.int32, a_ref.shape, 0)

    @pl.when(j < j_shift)
    def _():
        o_ref[...] = a_ref[...].T.astype(bf16)

    @pl.when(jnp.logical_and(j >= j_shift, j < j_misc))
    def _():
        o_ref[...] = jnp.concatenate([a_ref[shift:, :], b_ref[:shift, :]], axis=0).T.astype(bf16)

    @pl.when(j == j_misc)
    def _():
        y = jnp.where(row < shift, a_ref[...], jnp.where(row < 2 * shift, b_ref[...], 0.0))
        o_ref[...] = y.T.astype(bf16)

    @pl.when(j > j_misc)
    def _():
        o_ref[...] = jnp.zeros(o_ref.shape, bf16)


def prep_w_ab(w, layer, *, c_ga, c_end_ik, n_out):
    _, K, n_in = w.shape
    shift = GLA_RANK
    assert c_ga % LANE == 0 and (c_end_ik - shift) % LANE == 0 and n_in - c_end_ik == shift
    j_shift = c_ga // LANE
    j_misc = (c_end_ik - shift) // LANE
    last = (n_in - 1) // LANE

    def amap(j):
        return (layer, jnp.where(j == j_misc, j_shift, jnp.minimum(j, last)), 0)

    def bmap(j):
        return (layer, jnp.where(j == j_misc, j_misc, jnp.minimum(j + 1, last)), 0)

    wt = jnp.swapaxes(w, 1, 2)
    return pl.pallas_call(
        functools.partial(_prep_w_ab_kernel, j_shift=j_shift, j_misc=j_misc, shift=shift),
        grid=(n_out // LANE,),
        in_specs=[pl.BlockSpec((None, LANE, K), amap), pl.BlockSpec((None, LANE, K), bmap)],
        out_specs=pl.BlockSpec((K, LANE), lambda j: (0, j)),
        out_shape=jax.ShapeDtypeStruct((K, n_out), bf16),
        compiler_params=_cparams(("parallel",)),
        name="prep_w_ab",
    )(wt, wt)


def rope_tables(pos):
    half = LANE // 2
    inv = 1.0 / (ROPE_THETA ** (jnp.arange(half, dtype=f32) * (2.0 / LANE)))
    ang = pos.astype(f32)[:, None] * inv[None, :]
    cos, sin = jnp.cos(ang), jnp.sin(ang)
    return jnp.concatenate([cos, cos], axis=-1), jnp.concatenate([-sin, sin], axis=-1)


def _gla_kernel(q_ref, k_ref, v_ref, gg_ref, misc_ref, wa2_ref, ba_ref, gn_ref, s0_ref,
                o_ref, sout_ref, st_ref, *, c, n_inner, dk, dv):
    step = pl.program_id(2)

    @pl.when(step == 0)
    def _():
        st_ref[...] = s0_ref[0, 0].T

    nsub = c // GLA_SUB
    row_c = lax.broadcasted_iota(jnp.int32, (c, dk), 0)
    row_a = lax.broadcasted_iota(jnp.int32, (c, c), 0)
    lane_a = lax.broadcasted_iota(jnp.int32, (c, c), 1)
    sub_a = row_a % GLA_SUB

    def chunk(ci, carry):
        r = pl.multiple_of(ci * c, c)
        q = q_ref[pl.ds(r, c), :] * (dk ** -0.5)
        k = k_ref[pl.ds(r, c), :]
        v = v_ref[pl.ds(r, c), :]
        ga = misc_ref[pl.ds(r, c), :][:, :GLA_RANK]
        x = jnp.dot(ga.astype(bf16), wa2_ref[...], preferred_element_type=f32) + ba_ref[...]
        loga = (jnp.minimum(x, 0.0) - jnp.log1p(jnp.exp(-jnp.abs(x)))) / GLA_TAU
        b = loga
        d = 1
        while d < c:
            b = b + jnp.where(row_c >= d, pltpu.roll(b, d, 0), 0.0)
            d *= 2
        st = st_ref[...]
        vb = v.astype(bf16)
        o_inter = lax.dot_general((q * jnp.exp(b)).astype(bf16), st.astype(bf16), NT,
                                  preferred_element_type=f32)
        att = jnp.zeros((c, c), f32)
        for delta in range(GLA_SUB):
            if delta == 0:
                w = q * k
            else:
                w = q * pltpu.roll(k, delta, 0) * jnp.exp(jnp.minimum(b - pltpu.roll(b, delta, 0), 0.0))
            col = jnp.sum(w, axis=-1, keepdims=True)
            att = jnp.where(jnp.logical_and(lane_a == row_a - delta, sub_a >= delta), col, att)
        if nsub > 1:
            offs = [jnp.zeros((GLA_SUB, c), f32)]
            for i in range(1, nsub):
                r0 = i * GLA_SUB
                bref = b[r0 - 1:r0]
                qq = q[r0:r0 + GLA_SUB] * jnp.exp(b[r0:r0 + GLA_SUB] - bref)
                kk = jnp.where(row_c < r0, k * jnp.exp(jnp.minimum(bref - b, 0.0)), 0.0)
                offs.append(lax.dot_general(qq.astype(bf16), kk.astype(bf16), NT, preferred_element_type=f32))
            att = att + jnp.concatenate(offs, axis=0)
        o = o_inter + jnp.dot(att.astype(bf16), vb, preferred_element_type=f32)
        b_last = b[c - 1:c]
        kd = k * jnp.exp(b_last - b)
        st_ref[...] = jnp.exp(b_last) * st + lax.dot_general(vb, kd.astype(bf16), TN, preferred_element_type=f32)
        y = o * lax.rsqrt(jnp.mean(o * o, axis=-1, keepdims=True) + EPS) * gn_ref[...]
        gg = gg_ref[pl.ds(r, c), :]
        y = y * (gg * (1.0 / (1.0 + jnp.exp(-gg))))
        o_ref[pl.ds(r, c), :] = y.astype(o_ref.dtype)
        return carry

    lax.fori_loop(0, n_inner, chunk, 0)

    @pl.when(step == pl.num_programs(2) - 1)
    def _():
        sout_ref[0, 0] = st_ref[...].T


def gla(z, misc, s0, sb0, wa2, ba, gn, *, n_seq, T, heads, dk, dv, cols):
    c = min(CHUNK, T)
    tb = max(_pick(T, (256, 128, 64, 32, 16)), c)
    n_steps = T // tb
    cq, ck, cv, cg = cols
    in_specs = [
        pl.BlockSpec((tb, dk), lambda b, h, s: (b * n_steps + s, cq // dk + h)),
        pl.BlockSpec((tb, dk), lambda b, h, s: (b * n_steps + s, ck // dk + h)),
        pl.BlockSpec((tb, dv), lambda b, h, s: (b * n_steps + s, cv // dv + h)),
        pl.BlockSpec((tb, dv), lambda b, h, s: (b * n_steps + s, cg // dv + h)),
        pl.BlockSpec((tb, LANE), lambda b, h, s: (b * n_steps + s, 0)),
        pl.BlockSpec((GLA_RANK, dk), lambda b, h, s: (0, h)),
        pl.BlockSpec((1, dk), lambda b, h, s: (0, h)),
        pl.BlockSpec((1, dv), lambda b, h, s: (0, 0)),
        pl.BlockSpec((1, 1, dk, dv), lambda b, h, s: (sb0 + b, h, 0, 0)),
    ]
    out_specs = [
        pl.BlockSpec((tb, dv), lambda b, h, s: (b * n_steps + s, h)),
        pl.BlockSpec((1, 1, dk, dv), lambda b, h, s: (b, h, 0, 0)),
    ]
    return pl.pallas_call(
        functools.partial(_gla_kernel, c=c, n_inner=tb // c, dk=dk, dv=dv),
        grid=(n_seq, heads, n_steps),
        in_specs=in_specs,
        out_specs=out_specs,
        out_shape=[jax.ShapeDtypeStruct((n_seq * T, heads * dv), bf16),
                   jax.ShapeDtypeStruct((n_seq, heads, dk, dv), f32)],
        scratch_shapes=[pltpu.VMEM((dv, dk), f32)],
        compiler_params=_cparams(("parallel", "parallel", "arbitrary")),
        name="gla",
    )(z, z, z, z, misc, wa2, ba, gn, s0)


def _dsa_kernel(*refs, tq, kb, s_valid, pos0, past, topk, heads, kv_heads, idx_heads):
    if past:
        (dq_ref, iq_ref, misc_ref, k_ref, v_ref, ik_ref, kc_ref, vc_ref, ikc_ref, o_ref,
         key_s, q_s, m_s, l_s, acc_s, ikt_s, kt_s, vt_s) = refs
    else:
        dq_ref, iq_ref, misc_ref, k_ref, v_ref, ik_ref, o_ref, key_s, q_s, m_s, l_s, acc_s = refs
    j = pl.program_id(1)
    qpos0 = pos0 + j * tq
    qchunk = (qpos0 + lax.broadcasted_iota(jnp.int32, (tq, 1), 0)) // CHUNK
    w = misc_ref[...][:, GLA_RANK:GLA_RANK + idx_heads]
    group = heads // kv_heads
    T = dq_ref.shape[0]

    if past:
        n_cache = past // kb
        nkb = n_cache + 1
        ikt_s[...] = jnp.zeros(ikt_s.shape, bf16)
        ikt_s[0:T, :] = ik_ref[...]
        kt_s[...] = jnp.zeros(kt_s.shape, bf16)
        vt_s[...] = jnp.zeros(vt_s.shape, bf16)
        for g in range(kv_heads):
            kt_s[g, 0:T, :] = k_ref[:, g * DSA_HD:(g + 1) * DSA_HD]
            vt_s[g, 0:T, :] = v_ref[:, g * DSA_HD:(g + 1) * DSA_HD]

        def over_blocks(cache_fn, tail_fn):
            lax.fori_loop(0, n_cache, lambda b, c: (cache_fn(b), c)[1], 0)
            tail_fn()

        def cached(ref, b, g):
            return ref[pl.ds(b * (kb * kv_heads) + g, kb, stride=kv_heads), :].astype(bf16)
    else:
        vis_end = jnp.minimum(((qpos0 + tq - 1) // CHUNK + 1) * CHUNK, s_valid)
        nkb = (vis_end + kb - 1) // kb

    def score_block(c0, ikb):
        acc = jnp.zeros((tq, kb), f32)
        for h in range(idx_heads):
            s = lax.dot_general(iq_ref[:, h * IDX_HD:(h + 1) * IDX_HD], ikb, NT, preferred_element_type=f32)
            acc = acc + w[:, h:h + 1] * jnp.maximum(s, 0.0)
        kpos = c0 + lax.broadcasted_iota(jnp.int32, (1, kb), 1)
        vis = jnp.logical_and(kpos // CHUNK <= qchunk, kpos < s_valid)
        bits = pltpu.bitcast(acc, jnp.int32)
        key = bits ^ ((bits >> 31) & 0x7FFFFFFF)
        key_s[:, pl.ds(c0, kb)] = jnp.where(vis, key, INT_MIN)

    if past:
        over_blocks(lambda b: score_block(pl.multiple_of(b * kb, kb),
                                          ikc_ref[pl.ds(pl.multiple_of(b * kb, kb), kb), :].astype(bf16)),
                    lambda: score_block(n_cache * kb, ikt_s[...]))
    else:
        def score_body(b, carry):
            c0 = pl.multiple_of(b * kb, kb)
            score_block(c0, ik_ref[0, pl.ds(c0, kb), :])
            return carry

        lax.fori_loop(0, nkb, score_body, 0)

    one, zero = jnp.int32(1), jnp.int32(0)
    izeros = jnp.zeros((tq, LANE), jnp.int32)
    lane_i = lax.broadcasted_iota(jnp.int32, (1, LANE), 1)

    def count(pred):
        def body(b, cacc):
            c0 = pl.multiple_of(b * kb, kb)
            blk = key_s[:, pl.ds(c0, kb)]
            for t in range(kb // LANE):
                cacc = cacc + jnp.where(pred(blk[:, t * LANE:(t + 1) * LANE], c0 + t * LANE + lane_i), one, zero)
            return cacc
        return jnp.sum(lax.fori_loop(0, nkb, body, izeros), axis=-1, keepdims=True)

    def bit_body(i, carry):
        t_u, c_at = carry
        cand_u = t_u | lax.shift_left(one, 31 - i)
        cand = cand_u ^ INT_MIN
        cnt = count(lambda kt, it: kt >= cand)
        ok = cnt >= topk
        return jnp.where(ok, cand_u, t_u), jnp.where(ok, cnt, c_at)

    t_u, c_at = lax.fori_loop(0, 32, bit_body, (izeros, izeros))
    thr = jnp.maximum(t_u ^ INT_MIN, INT_MIN + 1)

    @pl.when(jnp.max(c_at) > topk)
    def _():
        need = topk - count(lambda kt, it: kt > thr)
        nbits = int(s_valid).bit_length()

        def idx_body(i, lo):
            cand = lo | lax.shift_left(one, nbits - 1 - i)
            c = count(lambda kt, it: jnp.logical_and(kt == thr, it < cand))
            return jnp.where(c < need, cand, lo)

        lo = lax.fori_loop(0, nbits, idx_body, izeros)

        def drop_body(b, carry):
            c0 = pl.multiple_of(b * kb, kb)
            blk = key_s[:, pl.ds(c0, kb)]
            tiles = []
            for t in range(kb // LANE):
                kt = blk[:, t * LANE:(t + 1) * LANE]
                drop = jnp.logical_and(kt == thr, c0 + t * LANE + lane_i > lo)
                tiles.append(jnp.where(drop, INT_MIN, kt))
            key_s[:, pl.ds(c0, kb)] = jnp.concatenate(tiles, axis=1)
            return carry

        lax.fori_loop(0, nkb, drop_body, 0)

    c2 = DSA_HD ** -0.5 * LOG2E
    for g in range(kv_heads):
        for i in range(group):
            hh = g * group + i
            q_s[g, i * tq:(i + 1) * tq, :] = dq_ref[:, hh * DSA_HD:(hh + 1) * DSA_HD]
    m_s[...] = jnp.full(m_s.shape, NEG_INIT, f32)
    l_s[...] = jnp.zeros(l_s.shape, f32)
    acc_s[...] = jnp.zeros(acc_s.shape, f32)

    def attend_block(c0, kv_fn):
        keys = key_s[:, pl.ds(c0, kb)]
        bias = jnp.concatenate(
            [jnp.where(keys[:, t * LANE:(t + 1) * LANE] >= thr, 0.0, NEG_MASK) for t in range(kb // LANE)],
            axis=1)
        bias = jnp.concatenate([bias] * group, axis=0)
        for g in range(kv_heads):
            kblk, vblk = kv_fn(g)
            s = lax.dot_general(q_s[g], kblk, NT, preferred_element_type=f32) * c2 + bias
            m_old = m_s[g]
            m_new = jnp.maximum(m_old, jnp.max(s, axis=-1, keepdims=True))
            p = jnp.exp2(s - m_new)
            alpha = jnp.exp2(m_old - m_new)
            l_s[g] = alpha * l_s[g] + jnp.sum(p, axis=-1, keepdims=True)
            acc_s[g] = alpha * acc_s[g] + jnp.dot(p.astype(bf16), vblk, preferred_element_type=f32)
            m_s[g] = m_new

    if past:
        over_blocks(lambda b: attend_block(pl.multiple_of(b * kb, kb),
                                           lambda g: (cached(kc_ref, b, g), cached(vc_ref, b, g))),
                    lambda: attend_block(n_cache * kb, lambda g: (kt_s[g], vt_s[g])))
    else:
        def kv_body(b, carry):
            c0 = pl.multiple_of(b * kb, kb)
            attend_block(c0, lambda g: (k_ref[0, pl.ds(c0, kb), g * DSA_HD:(g + 1) * DSA_HD],
                                        v_ref[0, pl.ds(c0, kb), g * DSA_HD:(g + 1) * DSA_HD]))
            return carry

        lax.fori_loop(0, nkb, kv_body, 0)
    for g in range(kv_heads):
        out = acc_s[g] / l_s[g]
        for i in range(group):
            hh = g * group + i
            o_ref[:, hh * DSA_HD:(hh + 1) * DSA_HD] = out[i * tq:(i + 1) * tq].astype(o_ref.dtype)


def dsa(dq, iq, misc, k_new, v_new, ik_new, caches, *, n_seq, T, pos0, heads, kv_heads, idx_heads, tq, kb):
    nq = T // tq
    group = heads // kv_heads
    kvw = kv_heads * DSA_HD
    past = 0 if caches is None else caches[2].shape[1]
    s_valid = past + T
    topk = min(IDX_TOPK, s_valid // 4)
    in_specs = [
        pl.BlockSpec((tq, heads * DSA_HD), lambda b, j: (b * nq + j, 0)),
        pl.BlockSpec((tq, idx_heads * IDX_HD), lambda b, j: (b * nq + j, 0)),
        pl.BlockSpec((tq, LANE), lambda b, j: (b * nq + j, 0)),
    ]
    scratch = [pltpu.VMEM((kv_heads, group * tq, DSA_HD), bf16),
               pltpu.VMEM((kv_heads, group * tq, 1), f32), pltpu.VMEM((kv_heads, group * tq, 1), f32),
               pltpu.VMEM((kv_heads, group * tq, DSA_HD), f32)]
    if past:
        assert nq == 1 and past % kb == 0 and T <= kb
        kc, vc, ikc, pb0 = caches
        s_pad = past + kb
        in_specs += [pl.BlockSpec((T, kvw), lambda b, j: (b, 0)), pl.BlockSpec((T, kvw), lambda b, j: (b, 0)),
                     pl.BlockSpec((T, IDX_HD), lambda b, j: (b, 0)),
                     pl.BlockSpec((None, past * kv_heads, DSA_HD), lambda b, j: (pb0 + b, 0, 0)),
                     pl.BlockSpec((None, past * kv_heads, DSA_HD), lambda b, j: (pb0 + b, 0, 0)),
                     pl.BlockSpec((None, past, IDX_HD), lambda b, j: (pb0 + b, 0, 0))]
        args = (dq, iq, misc, k_new, v_new, ik_new, kc, vc, ikc)
        scratch += [pltpu.VMEM((kb, IDX_HD), bf16), pltpu.VMEM((kv_heads, kb, DSA_HD), bf16),
                    pltpu.VMEM((kv_heads, kb, DSA_HD), bf16)]
    else:
        s_pad = T
        in_specs += [pl.BlockSpec((1, T, kvw), lambda b, j: (b, 0, 0)), pl.BlockSpec((1, T, kvw), lambda b, j: (b, 0, 0)),
                     pl.BlockSpec((1, T, IDX_HD), lambda b, j: (b, 0, 0))]
        args = (dq, iq, misc, k_new.reshape(n_seq, T, kvw), v_new.reshape(n_seq, T, kvw),
                ik_new.reshape(n_seq, T, IDX_HD))
    kern = functools.partial(_dsa_kernel, tq=tq, kb=kb, s_valid=s_valid, pos0=pos0, past=past, topk=topk,
                             heads=heads, kv_heads=kv_heads, idx_heads=idx_heads)
    return pl.pallas_call(
        kern,
        grid=(n_seq, nq),
        in_specs=in_specs,
        out_specs=pl.BlockSpec((tq, heads * DSA_HD), lambda b, j: (b * nq + j, 0)),
        out_shape=jax.ShapeDtypeStruct((n_seq * T, heads * DSA_HD), bf16),
        scratch_shapes=[pltpu.VMEM((tq, s_pad), jnp.int32)] + scratch,
        compiler_params=_cparams(("parallel", "arbitrary")),
        name="dsa",
    )(*args)


def _diff_lambda(lq1, lk1, lq2, lk2, lam_init):
    return (jnp.exp(jnp.sum(lq1[...] * lk1[...], axis=-1, keepdims=True))
            - jnp.exp(jnp.sum(lq2[...] * lk2[...], axis=-1, keepdims=True)) + lam_init)


def _diff_finish(o, gn_ref, lam_init):
    y = o * lax.rsqrt(jnp.mean(o * o, axis=-1, keepdims=True) + EPS) * gn_ref[...]
    return y * (1.0 - lam_init)


def _diff_prompt_kernel(qt_ref, kt_ref, q_ref, k_ref, v_ref, lq1, lk1, lq2, lk2, gn_ref, o_ref, m_s, l_s, acc_s,
                        *, tq, tk, lam_init):
    p_id = pl.program_id(2)
    qi = qt_ref[p_id]
    ki = kt_ref[p_id]
    last_k = ((qi + 1) * tq - 1) // tk
    c2 = DIFF_HD ** -0.5 * LOG2E

    @pl.when(ki == 0)
    def _():
        m_s[...] = jnp.full(m_s.shape, NEG_INIT, f32)
        l_s[...] = jnp.zeros(l_s.shape, f32)
        acc_s[...] = jnp.zeros(acc_s.shape, f32)

    def update(masked):
        vb = v_ref[...]
        if masked:
            qc = (qi * tq + lax.broadcasted_iota(jnp.int32, (tq, 1), 0)) // CHUNK
            kc = (ki * tk + lax.broadcasted_iota(jnp.int32, (1, tk), 1)) // CHUNK
            bias = jnp.where(kc <= qc, 0.0, NEG_MASK)
        for c in range(2):
            s = lax.dot_general(q_ref[:, c * DIFF_HD:(c + 1) * DIFF_HD], k_ref[:, c * DIFF_HD:(c + 1) * DIFF_HD],
                                NT, preferred_element_type=f32) * c2
            if masked:
                s = s + bias
            m_old = m_s[c]
            m_new = jnp.maximum(m_old, jnp.max(s, axis=-1, keepdims=True))
            p = jnp.exp2(s - m_new)
            alpha = jnp.exp2(m_old - m_new)
            l_s[c] = alpha * l_s[c] + jnp.sum(p, axis=-1, keepdims=True)
            acc_s[c] = alpha * acc_s[c] + jnp.dot(p.astype(bf16), vb, preferred_element_type=f32)
            m_s[c] = m_new

    needs_mask = (ki + 1) * tk > qi * tq + CHUNK

    @pl.when(needs_mask)
    def _():
        update(True)

    @pl.when(jnp.logical_not(needs_mask))
    def _():
        update(False)

    @pl.when(ki == last_k)
    def _():
        lam = _diff_lambda(lq1, lk1, lq2, lk2, lam_init)
        o = acc_s[0] / l_s[0] - lam * (acc_s[1] / l_s[1])
        o_ref[...] = _diff_finish(o, gn_ref, lam_init).astype(o_ref.dtype)


def diff_prompt(q, k, v, lams, gn, *, n_seq, T, heads, lam_init):
    hw = 2 * DIFF_HD
    tq = _pick(T, (DIFF_TQ, 512, 256, 128))
    tk = _pick(T, (DIFF_TK, 512, 256, 128))
    nq, nk = T // tq, T // tk
    pairs = [(qi, ki) for qi in range(nq) for ki in range(((qi + 1) * tq - 1) // tk + 1)]
    qt = jnp.asarray(np.array([p[0] for p in pairs], np.int32))
    kt = jnp.asarray(np.array([p[1] for p in pairs], np.int32))
    vec = pl.BlockSpec((1, DIFF_HD), lambda b, h, p, qt, kt: (0, 0))
    kv = pl.BlockSpec((tk, hw), lambda b, h, p, qt, kt: (b * nk + kt[p], h))
    qo = pl.BlockSpec((tq, hw), lambda b, h, p, qt, kt: (b * nq + qt[p], h))
    return pl.pallas_call(
        functools.partial(_diff_prompt_kernel, tq=tq, tk=tk, lam_init=lam_init),
        grid_spec=pltpu.PrefetchScalarGridSpec(
            num_scalar_prefetch=2,
            grid=(n_seq, heads, len(pairs)),
            in_specs=[qo, kv, kv, vec, vec, vec, vec, pl.BlockSpec((1, hw), lambda b, h, p, qt, kt: (0, 0))],
            out_specs=qo,
            scratch_shapes=[pltpu.VMEM((2, tq, 1), f32), pltpu.VMEM((2, tq, 1), f32), pltpu.VMEM((2, tq, hw), f32)],
        ),
        out_shape=jax.ShapeDtypeStruct((n_seq * T, heads * hw), bf16),
        compiler_params=_cparams(("parallel", "parallel", "arbitrary")),
        name="diff_prompt",
    )(qt, kt, q, k, v, *lams, gn)


def _diff_sample_kernel(q_ref, kn_ref, vn_ref, k0_ref, k1_ref, vlo_ref, vhi_ref, lq1, lk1, lq2, lk2, gn_ref, o_ref,
                        m_s, l_s, acc_s, flat_s, *, tk, hpb, lam_init):
    kb = pl.program_id(2)
    T = q_ref.shape[0]
    hw = 2 * DIFF_HD
    c2 = DIFF_HD ** -0.5 * LOG2E

    @pl.when(kb == 0)
    def _():
        m_s[...] = jnp.full(m_s.shape, NEG_INIT, f32)
        l_s[...] = jnp.zeros(l_s.shape, f32)
        acc_s[...] = jnp.zeros(acc_s.shape, f32)

    def update(hl, kblks, vblk):
        ps = []
        for c in range(2):
            idx = 2 * hl + c
            s = lax.dot_general(q_ref[:, idx * DIFF_HD:(idx + 1) * DIFF_HD], kblks[c], NT,
                                preferred_element_type=f32) * c2
            m_old = m_s[idx]
            m_new = jnp.maximum(m_old, jnp.max(s, axis=-1, keepdims=True))
            p = jnp.exp2(s - m_new)
            alpha = jnp.exp2(m_old - m_new)
            l_s[idx] = alpha * l_s[idx] + jnp.sum(p, axis=-1, keepdims=True)
            m_s[idx] = m_new
            acc_s[idx] = alpha * acc_s[idx]
            ps.append(p.astype(bf16))
        pv = jnp.dot(jnp.concatenate(ps, axis=0), vblk, preferred_element_type=f32)
        acc_s[2 * hl] = acc_s[2 * hl] + pv[:T]
        acc_s[2 * hl + 1] = acc_s[2 * hl + 1] + pv[T:]

    for n, ref in enumerate((k0_ref, k1_ref, vlo_ref, vhi_ref)):
        flat_s[n] = ref[...].reshape(tk * 8, DIFF_HD)

    def rows(n, r):
        return flat_s[n, pl.ds(r, tk, stride=8), :]

    for hl in range(hpb):
        quad, j0 = hl // 4, (hl % 4) * 2
        kblks = tuple(rows(quad, j0 + c).astype(bf16) for c in range(2))
        vblk = jnp.concatenate([rows(2, hl), rows(3, hl)], axis=1).astype(bf16)
        update(hl, kblks, vblk)

    @pl.when(kb == pl.num_programs(2) - 1)
    def _():
        lam = _diff_lambda(lq1, lk1, lq2, lk2, lam_init)
        for hl in range(hpb):
            kblks = tuple(kn_ref[:, (2 * hl + c) * DIFF_HD:(2 * hl + c + 1) * DIFF_HD] for c in range(2))
            update(hl, kblks, vn_ref[:, hl * hw:(hl + 1) * hw])
            o = acc_s[2 * hl] / l_s[2 * hl] - lam * (acc_s[2 * hl + 1] / l_s[2 * hl + 1])
            o_ref[:, hl * hw:(hl + 1) * hw] = _diff_finish(o, gn_ref, lam_init).astype(o_ref.dtype)


def diff_sample(q, k, v, k_past, v_past, pb0, lams, gn, *, n_seq, T, heads, lam_init):
    hw = 2 * DIFF_HD
    hpb = 8
    past = k_past.shape[1]
    assert (past + T - 1) // CHUNK == past // CHUNK and past % CHUNK == 0
    assert heads % hpb == 0
    tk = _pick(past, (DIFF_SAMPLE_TK, 512, 256, 128))
    new = pl.BlockSpec((T, hpb * hw), lambda b, o, kb: (b, o))
    vec = pl.BlockSpec((1, DIFF_HD), lambda b, o, kb: (0, 0))
    tile = (None, tk, 8, DIFF_HD)
    return pl.pallas_call(
        functools.partial(_diff_sample_kernel, tk=tk, hpb=hpb, lam_init=lam_init),
        grid=(n_seq, heads // hpb, past // tk),
        in_specs=[new, new, new,
                  pl.BlockSpec(tile, lambda b, o, kb: (pb0 + b, kb, 2 * o, 0)),
                  pl.BlockSpec(tile, lambda b, o, kb: (pb0 + b, kb, 2 * o + 1, 0)),
                  pl.BlockSpec(tile, lambda b, o, kb: (pb0 + b, kb, o, 0)),
                  pl.BlockSpec(tile, lambda b, o, kb: (pb0 + b, kb, o, 1)),
                  vec, vec, vec, vec, pl.BlockSpec((1, hw), lambda b, o, kb: (0, 0))],
        out_specs=pl.BlockSpec((T, hpb * hw), lambda b, o, kb: (b, o)),
        out_shape=jax.ShapeDtypeStruct((n_seq * T, heads * hw), bf16),
        scratch_shapes=[pltpu.VMEM((2 * hpb, T, 1), f32), pltpu.VMEM((2 * hpb, T, 1), f32),
                        pltpu.VMEM((2 * hpb, T, hw), f32), pltpu.VMEM((4, tk * 8, DIFF_HD), f32)],
        compiler_params=_cparams(("parallel", "parallel", "arbitrary")),
        name="diff_sample",
    )(q, k, v, k_past, k_past, v_past, v_past, *lams, gn)


def _ffn_up_kernel(*refs, T, tm, tn, tiled_seq, emit_w):
    it = iter(refs)
    x_ref, wa_ref, wg_ref = next(it), next(it), next(it)
    if tiled_seq:
        prev_a, prev_g = (next(it),), (next(it),)
    else:
        prev_a = (next(it), next(it))
        prev_g = (next(it), next(it))
    cwa_ref, cwg_ref, cba_ref, cbg_ref = next(it), next(it), next(it), next(it)
    o_ref, ta_ref, tg_ref = next(it), next(it), next(it)
    wb_refs = (next(it), next(it)) if emit_w else None
    was_s, wgs_s, carry_s = next(it), next(it), next(it)
    i = pl.program_id(1)

    @pl.when(i == 0)
    def _():
        was_s[...] = wa_ref[...].astype(bf16)
        wgs_s[...] = wg_ref[...].astype(bf16)
        if emit_w:
            wb_refs[0][...] = was_s[...]
            wb_refs[1][...] = wgs_s[...]

    x = x_ref[...]
    row = lax.broadcasted_iota(jnp.int32, (tm, tn), 0)

    def half(idx, ws_s, prevs, cw_ref, cb_ref, tail_ref):
        u = jnp.dot(x, ws_s[...], preferred_element_type=f32)
        if tiled_seq:
            first = (i % (T // tm)) == 0
            c0 = jnp.where(first, prevs[0][0, 0:1, :], carry_s[idx, 0:1, :])
            c1 = jnp.where(first, prevs[0][0, 1:2, :], carry_s[idx, 1:2, :])
            u1 = jnp.where(row == 0, c1, pltpu.roll(u, 1, 0))
            u2 = jnp.where(row == 0, c0, jnp.where(row == 1, c1, pltpu.roll(u, 2, 0)))
            carry_s[idx] = u[tm - 2:tm]
            tail_ref[0] = u[tm - 2:tm]
        else:
            rmod = row % T
            p1 = prevs[1][...]
            u1 = jnp.where(rmod == 0, p1, pltpu.roll(u, 1, 0))
            u2 = jnp.where(rmod == 0, prevs[0][...], jnp.where(rmod == 1, pltpu.roll(p1, 1, 0), pltpu.roll(u, 2, 0)))
            tail_ref[...] = u
        return cb_ref[...] + u2 * cw_ref[0:1, :] + u1 * cw_ref[1:2, :] + u * cw_ref[2:3, :]

    a = half(0, was_s, prev_a, cwa_ref, cba_ref, ta_ref)
    g = half(1, wgs_s, prev_g, cwg_ref, cbg_ref, tg_ref)
    o_ref[...] = (g * (1.0 / (1.0 + jnp.exp(-g))) * a).astype(o_ref.dtype)


def ffn_up(x, w, prev, pb0, cw, cb, l, *, n_seq, T, dff):
    M, K = x.shape
    tn = _pick(dff, (256, 128))
    nb = dff // tn
    tm = _pick(M, (FFN_TM, 512, 256, 128, 64, 32, 16))
    tiled_seq = tm <= T
    assert T % tm == 0 if tiled_seq else tm % T == 0
    emit_w = not isinstance(w, tuple)
    if emit_w:
        w_args = [w, w]
        w_specs = [pl.BlockSpec((None, K, tn), lambda j, i: (l, 0, j)),
                   pl.BlockSpec((None, K, tn), lambda j, i: (l, 0, nb + j))]
    else:
        w_args = list(w)
        w_specs = [pl.BlockSpec((K, tn), lambda j, i: (0, j))] * 2
    if tiled_seq:
        spt = T // tm
        p_args = [prev, prev]
        p_specs = [pl.BlockSpec((1, CONV_W - 1, tn), lambda j, i: (pb0 + i // spt, 0, j)),
                   pl.BlockSpec((1, CONV_W - 1, tn), lambda j, i: (pb0 + i // spt, 0, nb + j))]
        tail_shape = jax.ShapeDtypeStruct((n_seq, CONV_W - 1, dff), f32)
        tail_spec = pl.BlockSpec((1, CONV_W - 1, tn), lambda j, i: (i // spt, 0, j))
    else:
        pv = prev[pb0:pb0 + n_seq]
        ex = [jnp.pad(pv[:, k:k + 1], ((0, 0), (0, T - 1), (0, 0))).reshape(M, 2 * dff) for k in range(CONV_W - 1)]
        p_args = [ex[0], ex[1], ex[0], ex[1]]
        p_specs = [pl.BlockSpec((tm, tn), lambda j, i: (i, j)), pl.BlockSpec((tm, tn), lambda j, i: (i, j)),
                   pl.BlockSpec((tm, tn), lambda j, i: (i, nb + j)), pl.BlockSpec((tm, tn), lambda j, i: (i, nb + j))]
        tail_shape = jax.ShapeDtypeStruct((M, dff), f32)
        tail_spec = pl.BlockSpec((tm, tn), lambda j, i: (i, j))
    c_specs = [pl.BlockSpec((None, CONV_W, tn), lambda j, i: (l, 0, j)),
               pl.BlockSpec((None, CONV_W, tn), lambda j, i: (l, 0, nb + j)),
               pl.BlockSpec((None, 1, tn), lambda j, i: (l, 0, j)),
               pl.BlockSpec((None, 1, tn), lambda j, i: (l, 0, nb + j))]
    cb3 = cb.reshape(cb.shape[0], 1, 2 * dff)
    out_shape = [jax.ShapeDtypeStruct((M, dff), bf16), tail_shape, tail_shape]
    out_specs = [pl.BlockSpec((tm, tn), lambda j, i: (i, j)), tail_spec, tail_spec]
    if emit_w:
        out_shape += [jax.ShapeDtypeStruct((K, dff), bf16)] * 2
        out_specs += [pl.BlockSpec((K, tn), lambda j, i: (0, j))] * 2
    outs = pl.pallas_call(
        functools.partial(_ffn_up_kernel, T=T, tm=tm, tn=tn, tiled_seq=tiled_seq, emit_w=emit_w),
        grid=(nb, M // tm),
        in_specs=[pl.BlockSpec((tm, K), lambda j, i: (i, 0))] + w_specs + p_specs + c_specs,
        out_specs=out_specs,
        out_shape=out_shape,
        scratch_shapes=[pltpu.VMEM((K, tn), bf16), pltpu.VMEM((K, tn), bf16), pltpu.VMEM((2, CONV_W - 1, tn), f32)],
        compiler_params=_cparams(("parallel", "arbitrary")),
        name="ffn_up",
    )(x, *w_args, *p_args, cw, cw, cb3, cb3)
    g, ta, tg = outs[:3]
    if tiled_seq:
        state = jnp.concatenate([ta, tg], axis=-1)
    else:
        state = jnp.concatenate([ta.reshape(n_seq, T, dff)[:, T - (CONV_W - 1):],
                                 tg.reshape(n_seq, T, dff)[:, T - (CONV_W - 1):]], axis=-1)
    return g, state, (tuple(outs[3:]) if emit_w else None)


class _Stream:
    def __init__(self, x, pos0, past):
        self.n_seq, self.T, d = x.shape
        self.h = x.reshape(self.n_seq * self.T, d)
        self.pos0 = pos0
        self.past = past
        pos = jnp.tile(pos0 + jnp.arange(self.T, dtype=jnp.int32), self.n_seq)
        self.tables = rope_tables(pos)


def kernel(x_prompt, x_sample, cache_gla_state, cache_dsa_k, cache_dsa_v, cache_idx_k, cache_diff_k, cache_diff_v, state_ffn_conv, norm_mix, norm_ffn, norm_final, w_in_ab, w_gla_a2, b_gla_a, gla_norm, w_out_ab, w_in_c, lambda_q1, lambda_k1, lambda_q2, lambda_k2, diff_norm, w_out_c, w_up, conv_w, conv_b, w_down):
    B, S, D = x_prompt.shape
    DB, DS, _ = x_sample.shape
    depth = norm_mix.shape[0]
    n_ab, _, gh, gdk, gdv = cache_gla_state.shape
    n_c = cache_diff_k.shape[0]
    past = cache_dsa_k.shape[2]
    kvh = cache_dsa_k.shape[3]
    kvw = kvh * DSA_HD
    dh = w_out_ab.shape[1] - gh * gdv
    dsa_heads = dh // DSA_HD
    idx_heads = (w_in_ab.shape[2] - (2 * gh * gdk + 2 * gh * gdv + GLA_RANK + dh + 2 * kvw + IDX_HD)) // (IDX_HD + 1)
    diff_heads = cache_diff_k.shape[3]
    dff = w_down.shape[1]
    cw_ = diff_heads * 2 * DIFF_HD

    streams = (_Stream(x_prompt, 0, 0), _Stream(x_sample, past, past))
    names = ("gla", "dk", "dv", "ik", "ck", "cv", "conv")
    outs = [{k: [] for k in names} for _ in streams]

    c_gla = cache_gla_state.reshape(n_ab * DB, gh, gdk, gdv)
    c_dk = cache_dsa_k.reshape(n_ab * DB, past * kvh, DSA_HD)
    c_dv = cache_dsa_v.reshape(n_ab * DB, past * kvh, DSA_HD)
    c_ik = cache_idx_k.reshape(n_ab * DB, past, IDX_HD)
    c_ck = cache_diff_k.reshape(n_c * DB, past, 2 * diff_heads, DIFF_HD)
    c_cv = cache_diff_v.reshape(n_c * DB, past, diff_heads, 2 * DIFF_HD)
    c_conv = state_ffn_conv.reshape(depth * DB, CONV_W - 1, 2 * dff)
    zero_state = jnp.zeros((B, gh, gdk, gdv), f32)
    zero_conv = jnp.zeros((B, CONV_W - 1, 2 * dff), f32)
    wd_all = w_down.astype(bf16)

    for l in range(depth):
        i = l // 2
        if l % 2 == 0:
            sizes = (gh * gdk, gh * gdk, gh * gdv, gh * gdv, GLA_RANK, dh, kvw, kvw, idx_heads * IDX_HD, IDX_HD,
                     idx_heads)
            offs = np.concatenate([[0], np.cumsum(sizes)]).tolist()
            tn_ab = kvw
            n_body = offs[10] - GLA_RANK
            n_out = -(-(n_body + LANE) // tn_ab) * tn_ab
            w_perm = prep_w_ab(w_in_ab, i, c_ga=offs[4], c_end_ik=offs[10], n_out=n_out)
            ab_segs = [(offs[4], False, (f32,)), (dh, True, (bf16,)), (kvw, True, (f32, bf16)),
                       (kvw, False, (f32, bf16)), (idx_heads * IDX_HD, True, (bf16,)),
                       (tn_ab, [(IDX_HD, True, (f32, bf16)), (LANE, False, (f32,))])]
            c_gq = 0
            c_gk = c_gq + gh * gdk
            c_gv = c_gk + gh * gdk
            c_gg = c_gv + gh * gdv
            wa2 = w_gla_a2[i].astype(bf16)
            ba = b_gla_a[i].reshape(1, gh * gdk)
            gn = gla_norm[i].reshape(1, gdv)
            w_out = w_out_ab[i].astype(bf16)
            for si, st in enumerate(streams):
                hn = rmsnorm(st.h, norm_mix[l], bf16)
                zg, dq_b, dk_f, dk_b, dv_f, dv_b, iq_b, ik_f, ik_b, misc = proj(hn, w_perm, st.tables, ab_segs, tn_ab,
                                                                                 "in_ab")
                if st.past:
                    s0, sb0 = c_gla, i * DB
                else:
                    s0, sb0 = zero_state, 0
                go, gst = gla(zg, misc, s0, sb0, wa2, ba, gn, n_seq=st.n_seq, T=st.T, heads=gh, dk=gdk, dv=gdv,
                              cols=(c_gq, c_gk, c_gv, c_gg))
                if st.past:
                    tq, kb, caches = st.T, DSA_KB_SAMPLE, (c_dk, c_dv, c_ik, i * DB)
                else:
                    tq, kb, caches = _pick(st.T, (DSA_TQ,)), _pick(st.T, (DSA_KB, 512, 256, 128)), None
                do = dsa(dq_b, iq_b, misc, dk_b, dv_b, ik_b, caches, n_seq=st.n_seq, T=st.T, pos0=st.pos0,
                         heads=dsa_heads, kv_heads=kvh, idx_heads=idx_heads, tq=tq, kb=kb)
                st.h = matmul((go, do), w_out, f32, res=st.h, name="out_ab")
                o = outs[si]
                o["gla"].append(gst)
                o["dk"].append(dk_f.reshape(st.n_seq, st.T, kvh, DSA_HD))
                o["dv"].append(dv_f.reshape(st.n_seq, st.T, kvh, DSA_HD))
                o["ik"].append(ik_f.reshape(st.n_seq, st.T, IDX_HD))
        else:
            lam_init = 0.8 - 0.6 * math.exp(-0.3 * l)
            w_in = w_in_c[i].astype(bf16)
            w_out = w_out_c[i].astype(bf16)
            lams = tuple(a[i].reshape(1, DIFF_HD) for a in (lambda_q1, lambda_k1, lambda_q2, lambda_k2))
            gn = diff_norm[i].reshape(1, 2 * DIFF_HD)
            c_segs = [(cw_, True, (bf16,)), (cw_, True, (f32, bf16)), (cw_, False, (f32, bf16))]
            for si, st in enumerate(streams):
                hn = rmsnorm(st.h, norm_mix[l], bf16)
                q_b, k_f, k_b, v_f, v_b = proj(hn, w_in, st.tables, c_segs, min(512, cw_), "in_c")
                if st.past:
                    at = diff_sample(q_b, k_b, v_b, c_ck, c_cv, i * DB, lams, gn, n_seq=st.n_seq, T=st.T,
                                     heads=diff_heads, lam_init=lam_init)
                else:
                    at = diff_prompt(q_b, k_b, v_b, lams, gn, n_seq=st.n_seq, T=st.T, heads=diff_heads,
                                     lam_init=lam_init)
                st.h = matmul(at, w_out, f32, res=st.h, name="out_c")
                o = outs[si]
                o["ck"].append(k_f.reshape(st.n_seq, st.T, diff_heads, 2, DIFF_HD))
                o["cv"].append(v_f.reshape(st.n_seq, st.T, diff_heads, 2 * DIFF_HD))

        wu = w_up
        for si, st in enumerate(streams):
            hn = rmsnorm(st.h, norm_ffn[l], bf16)
            prev, pb0 = (c_conv, l * DB) if st.past else (zero_conv, 0)
            g, conv_state, wu_b = ffn_up(hn, wu, prev, pb0, conv_w, conv_b, l, n_seq=st.n_seq, T=st.T, dff=dff)
            if wu_b is not None:
                wu = wu_b
            st.h = matmul(g, wd_all, f32, res=st.h, name="ffn_down", layer=l)
            outs[si]["conv"].append(conv_state)

    ys = [rmsnorm(st.h, norm_final, f32).reshape(st.n_seq, st.T, D) for st in streams]
    tail = [jnp.stack(o[k]) for o in outs for k in names]
    return (ys[0], ys[1], *tail)
```

```python
import functools
import math

import numpy as np
import jax
import jax.numpy as jnp
from jax import lax
from jax.experimental import pallas as pl
from jax.experimental.pallas import tpu as pltpu

CHUNK = 64
EPS = 1e-6
ROPE_THETA = 10000.0
GLA_RANK = 16
GLA_TAU = 16.0
DSA_HD = 128
IDX_HD = 128
IDX_TOPK = 256
DIFF_HD = 128
CONV_W = 3

LANE = 128
GLA_SUB = 16
DSA_TQ, DSA_KB = 128, 1024
DSA_KB_SAMPLE = 1024
DIFF_TQ, DIFF_TK = 1024, 1024
FFN_TM = 1024
DIFF_SAMPLE_TK = 512
VMEM_LIMIT = 60 * 1024 * 1024
MM_VMEM_BUDGET = 52 * 1024 * 1024
NEG_INIT = -1e30
NEG_MASK = -3e38
INT_MIN = -2147483648
LOG2E = 1.4426950408889634

f32 = jnp.float32
bf16 = jnp.bfloat16
NT = (((1,), (1,)), ((), ()))
TN = (((0,), (0,)), ((), ()))


def _cparams(sem):
    return pltpu.CompilerParams(dimension_semantics=sem, vmem_limit_bytes=VMEM_LIMIT)


def _pick(dim, cands):
    for c in cands:
        if c <= dim and dim % c == 0:
            return c
    return dim


def _rmsnorm_kernel(x_ref, g_ref, o_ref):
    x = x_ref[...]
    y = x * lax.rsqrt(jnp.mean(x * x, axis=-1, keepdims=True) + EPS)
    o_ref[...] = (y * g_ref[...]).astype(o_ref.dtype)


def rmsnorm(x, g, out_dtype):
    M, D = x.shape
    tm = _pick(M, (256, 128, 64, 32, 16, 8))
    return pl.pallas_call(
        _rmsnorm_kernel,
        grid=(M // tm,),
        in_specs=[pl.BlockSpec((tm, D), lambda i: (i, 0)), pl.BlockSpec((1, D), lambda i: (0, 0))],
        out_specs=pl.BlockSpec((tm, D), lambda i: (i, 0)),
        out_shape=jax.ShapeDtypeStruct((M, D), out_dtype),
        compiler_params=_cparams(("parallel",)),
        name="rmsnorm",
    )(x, g.reshape(1, D).astype(f32))


def _mm_kernel(*refs, n_x, has_res):
    x_refs, w_ref = refs[:n_x], refs[n_x]
    o_ref = refs[-1]
    acc = None
    k0 = 0
    for x_ref in x_refs:
        kw = x_ref.shape[1]
        part = jnp.dot(x_ref[...], w_ref[k0:k0 + kw, :], preferred_element_type=f32)
        acc = part if acc is None else acc + part
        k0 += kw
    if has_res:
        acc = refs[n_x + 1][...] + acc
    o_ref[...] = acc.astype(o_ref.dtype)


def matmul(xs, w, out_dtype, res=None, name="matmul", layer=None):
    xs = tuple(xs) if isinstance(xs, (tuple, list)) else (xs,)
    M = xs[0].shape[0]
    K, N = w.shape[-2:]
    ob = jnp.dtype(out_dtype).itemsize
    best = None
    for tm in (1024, 768, 512, 384, 256, 128, 64, 32, 16, 8):
        if M % tm:
            continue
        for tn in (1024, 768, 512, 384, 256, 128):
            if N % tn:
                continue
            vm = 2 * (tm * K * 2 + K * tn * 2 + tm * tn * ob + (tm * tn * 4 if res is not None else 0))
            if vm > MM_VMEM_BUDGET:
                continue
            score = (tm * tn) / (tm + tn)
            if best is None or score > best[0]:
                best = (score, tm, tn)
    _, tm, tn = best
    in_specs = [pl.BlockSpec((tm, x.shape[1]), lambda i, j: (i, 0)) for x in xs]
    if layer is None:
        in_specs.append(pl.BlockSpec((K, tn), lambda i, j: (0, j)))
    else:
        in_specs.append(pl.BlockSpec((None, K, tn), lambda i, j: (layer, 0, j)))
    args = list(xs) + [w]
    if res is not None:
        in_specs.append(pl.BlockSpec((tm, tn), lambda i, j: (i, j)))
        args.append(res)
    return pl.pallas_call(
        functools.partial(_mm_kernel, n_x=len(xs), has_res=res is not None),
        grid=(M // tm, N // tn),
        in_specs=in_specs,
        out_specs=pl.BlockSpec((tm, tn), lambda i, j: (i, j)),
        out_shape=jax.ShapeDtypeStruct((M, N), out_dtype),
        compiler_params=_cparams(("parallel", "arbitrary")),
        name=name,
    )(*args)


def _proj_kernel(x_ref, w_ref, c_ref, s_ref, *o_refs, segs, tn):
    j = pl.program_id(1)
    acc = jnp.dot(x_ref[...], w_ref[...], preferred_element_type=f32)

    @pl.when(j == 0)
    def _():
        for (j0, j1, rotate, out_ids, widths) in segs:
            if j0 > 0:
                for k in jax.tree_util.tree_leaves(out_ids):
                    o_refs[k][...] = jnp.zeros(o_refs[k].shape, o_refs[k].dtype)

    for (j0, j1, rotate, out_ids, widths) in segs:
        @pl.when(jnp.logical_and(j >= j0, j < j1))
        def _(rotate=rotate, out_ids=out_ids, widths=widths):
            cos, sin = c_ref[...], s_ref[...]
            col = 0
            for part, k_ids in enumerate(out_ids if widths else (out_ids,)):
                wcols = widths[part] if widths else tn
                rot = rotate[part] if widths else rotate
                for g in range(wcols // LANE):
                    y = acc[:, col + g * LANE:col + (g + 1) * LANE]
                    if rot:
                        y = y * cos + pltpu.roll(y, LANE // 2, 1) * sin
                    for k in (k_ids if isinstance(k_ids, tuple) else (k_ids,)):
                        o_refs[k][:, g * LANE:(g + 1) * LANE] = y.astype(o_refs[k].dtype)
                col += wcols


def proj(x, w, tables, segs_cols, tn, name):
    M, K = x.shape
    N = w.shape[1]
    tm = _pick(M, (1024, 768, 512, 256, 128, 64, 32, 16, 8))
    segs, out_shape, out_specs = [], [], []
    j0 = 0
    for sc in segs_cols:
        n_cols = sc[0]
        assert n_cols % tn == 0 or (len(sc) == 2 and n_cols <= tn)
        nj = max(n_cols // tn, 1)
        lo, hi = j0, j0 + nj

        def omap(i, j, lo=lo, hi=hi):
            return (i, jnp.clip(j - lo, 0, hi - lo - 1))

        if len(sc) == 3:
            _, rotate, dtypes = sc
            ids = []
            for dt in dtypes:
                ids.append(len(out_shape))
                out_shape.append(jax.ShapeDtypeStruct((M, n_cols), dt))
                out_specs.append(pl.BlockSpec((tm, tn), omap))
            segs.append((lo, hi, rotate, tuple(ids), None))
        else:
            parts = sc[1]
            ids, widths, rots = [], [], []
            for (width, rotate, dtypes) in parts:
                pid = []
                for dt in dtypes:
                    pid.append(len(out_shape))
                    out_shape.append(jax.ShapeDtypeStruct((M, width), dt))
                    out_specs.append(pl.BlockSpec((tm, width), lambda i, j: (i, 0)))
                ids.append(tuple(pid))
                widths.append(width)
                rots.append(rotate)
            segs.append((lo, hi, tuple(rots), tuple(ids), tuple(widths)))
        j0 = hi
    assert j0 * tn == N, (j0, tn, N)
    tab = pl.BlockSpec((tm, LANE), lambda i, j: (i, 0))
    return pl.pallas_call(
        functools.partial(_proj_kernel, segs=tuple(segs), tn=tn),
        grid=(M // tm, N // tn),
        in_specs=[pl.BlockSpec((tm, K), lambda i, j: (i, 0)), pl.BlockSpec((K, tn), lambda i, j: (0, j)), tab, tab],
        out_specs=out_specs,
        out_shape=out_shape,
        compiler_params=_cparams(("parallel", "arbitrary")),
        name=name,
    )(x, w, *tables)


def _prep_w_ab_kernel(a_ref, b_ref, o_ref, *, j_shift, j_misc, shift):
    j = pl.program_id(0)
    row = lax.broadcasted_iota(jnp.int32, a_ref.shape, 0)

    @pl.when(j < j_shift)
    def _():
        o_ref[...] = a_ref[...].T.astype(bf16)

    @pl.when(jnp.logical_and(j >= j_shift, j < j_misc))
    def _():
        o_ref[...] = jnp.concatenate([a_ref[shift:, :], b_ref[:shift, :]], axis=0).T.astype(bf16)

    @pl.when(j == j_misc)
    def _():
        y = jnp.where(row < shift, a_ref[...], jnp.where(row < 2 * shift, b_ref[...], 0.0))
        o_ref[...] = y.T.astype(bf16)

    @pl.when(j > j_misc)
    def _():
        o_ref[...] = jnp.zeros(o_ref.shape, bf16)


def prep_w_ab(w, layer, *, c_ga, c_end_ik, n_out):
    _, K, n_in = w.shape
    shift = GLA_RANK
    assert c_ga % LANE == 0 and (c_end_ik - shift) % LANE == 0 and n_in - c_end_ik == shift
    j_shift = c_ga // LANE
    j_misc = (c_end_ik - shift) // LANE
    last = (n_in - 1) // LANE

    def amap(j):
        return (layer, jnp.where(j == j_misc, j_shift, jnp.minimum(j, last)), 0)

    def bmap(j):
        return (layer, jnp.where(j == j_misc, j_misc, jnp.minimum(j + 1, last)), 0)

    wt = jnp.swapaxes(w, 1, 2)
    return pl.pallas_call(
        functools.partial(_prep_w_ab_kernel, j_shift=j_shift, j_misc=j_misc, shift=shift),
        grid=(n_out // LANE,),
        in_specs=[pl.BlockSpec((None, LANE, K), amap), pl.BlockSpec((None, LANE, K), bmap)],
        out_specs=pl.BlockSpec((K, LANE), lambda j: (0, j)),
        out_shape=jax.ShapeDtypeStruct((K, n_out), bf16),
        compiler_params=_cparams(("parallel",)),
        name="prep_w_ab",
    )(wt, wt)


def rope_tables(pos):
    half = LANE // 2
    inv = 1.0 / (ROPE_THETA ** (jnp.arange(half, dtype=f32) * (2.0 / LANE)))
    ang = pos.astype(f32)[:, None] * inv[None, :]
    cos, sin = jnp.cos(ang), jnp.sin(ang)
    return jnp.concatenate([cos, cos], axis=-1), jnp.concatenate([-sin, sin], axis=-1)


def _gla_kernel(q_ref, k_ref, v_ref, gg_ref, misc_ref, wa2_ref, ba_ref, gn_ref, s0_ref,
                o_ref, sout_ref, st_ref, *, c, n_inner, dk, dv):
    step = pl.program_id(2)

    @pl.when(step == 0)
    def _():
        st_ref[...] = s0_ref[0, 0].T

    nsub = c // GLA_SUB
    row_c = lax.broadcasted_iota(jnp.int32, (c, dk), 0)
    row_a = lax.broadcasted_iota(jnp.int32, (c, c), 0)
    lane_a = lax.broadcasted_iota(jnp.int32, (c, c), 1)
    sub_a = row_a % GLA_SUB

    def chunk(ci, carry):
        r = pl.multiple_of(ci * c, c)
        q = q_ref[pl.ds(r, c), :] * (dk ** -0.5)
        k = k_ref[pl.ds(r, c), :]
        v = v_ref[pl.ds(r, c), :]
        ga = misc_ref[pl.ds(r, c), :][:, :GLA_RANK]
        x = jnp.dot(ga.astype(bf16), wa2_ref[...], preferred_element_type=f32) + ba_ref[...]
        loga = (jnp.minimum(x, 0.0) - jnp.log1p(jnp.exp(-jnp.abs(x)))) / GLA_TAU
        b = loga
        d = 1
        while d < c:
            b = b + jnp.where(row_c >= d, pltpu.roll(b, d, 0), 0.0)
            d *= 2
        st = st_ref[...]
        vb = v.astype(bf16)
        o_inter = lax.dot_general((q * jnp.exp(b)).astype(bf16), st.astype(bf16), NT,
                                  preferred_element_type=f32)
        att = jnp.zeros((c, c), f32)
        for delta in range(GLA_SUB):
            if delta == 0:
                w = q * k
            else:
                w = q * pltpu.roll(k, delta, 0) * jnp.exp(jnp.minimum(b - pltpu.roll(b, delta, 0), 0.0))
            col = jnp.sum(w, axis=-1, keepdims=True)
            att = jnp.where(jnp.logical_and(lane_a == row_a - delta, sub_a >= delta), col, att)
        if nsub > 1:
            offs = [jnp.zeros((GLA_SUB, c), f32)]
            for i in range(1, nsub):
                r0 = i * GLA_SUB
                bref = b[r0 - 1:r0]
                qq = q[r0:r0 + GLA_SUB] * jnp.exp(b[r0:r0 + GLA_SUB] - bref)
                kk = jnp.where(row_c < r0, k * jnp.exp(jnp.minimum(bref - b, 0.0)), 0.0)
                offs.append(lax.dot_general(qq.astype(bf16), kk.astype(bf16), NT, preferred_element_type=f32))
            att = att + jnp.concatenate(offs, axis=0)
        o = o_inter + jnp.dot(att.astype(bf16), vb, preferred_element_type=f32)
        b_last = b[c - 1:c]
        kd = k * jnp.exp(b_last - b)
        st_ref[...] = jnp.exp(b_last) * st + lax.dot_general(vb, kd.astype(bf16), TN, preferred_element_type=f32)
        y = o * lax.rsqrt(jnp.mean(o * o, axis=-1, keepdims=True) + EPS) * gn_ref[...]
        gg = gg_ref[pl.ds(r, c), :]
        y = y * (gg * (1.0 / (1.0 + jnp.exp(-gg))))
        o_ref[pl.ds(r, c), :] = y.astype(o_ref.dtype)
        return carry

    lax.fori_loop(0, n_inner, chunk, 0)

    @pl.when(step == pl.num_programs(2) - 1)
    def _():
        sout_ref[0, 0] = st_ref[...].T


def gla(z, misc, s0, sb0, wa2, ba, gn, *, n_seq, T, heads, dk, dv, cols):
    c = min(CHUNK, T)
    tb = max(_pick(T, (256, 128, 64, 32, 16)), c)
    n_steps = T // tb
    cq, ck, cv, cg = cols
    in_specs = [
        pl.BlockSpec((tb, dk), lambda b, h, s: (b * n_steps + s, cq // dk + h)),
        pl.BlockSpec((tb, dk), lambda b, h, s: (b * n_steps + s, ck // dk + h)),
        pl.BlockSpec((tb, dv), lambda b, h, s: (b * n_steps + s, cv // dv + h)),
        pl.BlockSpec((tb, dv), lambda b, h, s: (b * n_steps + s, cg // dv + h)),
        pl.BlockSpec((tb, LANE), lambda b, h, s: (b * n_steps + s, 0)),
        pl.BlockSpec((GLA_RANK, dk), lambda b, h, s: (0, h)),
        pl.BlockSpec((1, dk), lambda b, h, s: (0, h)),
        pl.BlockSpec((1, dv), lambda b, h, s: (0, 0)),
        pl.BlockSpec((1, 1, dk, dv), lambda b, h, s: (sb0 + b, h, 0, 0)),
    ]
    out_specs = [
        pl.BlockSpec((tb, dv), lambda b, h, s: (b * n_steps + s, h)),
        pl.BlockSpec((1, 1, dk, dv), lambda b, h, s: (b, h, 0, 0)),
    ]
    return pl.pallas_call(
        functools.partial(_gla_kernel, c=c, n_inner=tb // c, dk=dk, dv=dv),
        grid=(n_seq, heads, n_steps),
        in_specs=in_specs,
        out_specs=out_specs,
        out_shape=[jax.ShapeDtypeStruct((n_seq * T, heads * dv), bf16),
                   jax.ShapeDtypeStruct((n_seq, heads, dk, dv), f32)],
        scratch_shapes=[pltpu.VMEM((dv, dk), f32)],
        compiler_params=_cparams(("parallel", "parallel", "arbitrary")),
        name="gla",
    )(z, z, z, z, misc, wa2, ba, gn, s0)


def _dsa_kernel(*refs, tq, kb, s_valid, pos0, past, topk, heads, kv_heads, idx_heads):
    if past:
        (dq_ref, iq_ref, misc_ref, k_ref, v_ref, ik_ref, kc_ref, vc_ref, ikc_ref, o_ref,
         key_s, q_s, m_s, acc_s, ikt_s, kt_s, vt_s) = refs
    else:
        dq_ref, iq_ref, misc_ref, k_ref, v_ref, ik_ref, o_ref, key_s, q_s, m_s, acc_s = refs
    j = pl.program_id(1)
    qpos0 = pos0 + j * tq
    qchunk = (qpos0 + lax.broadcasted_iota(jnp.int32, (tq, 1), 0)) // CHUNK
    w = misc_ref[...][:, GLA_RANK:GLA_RANK + idx_heads]
    group = heads // kv_heads
    T = dq_ref.shape[0]

    if past:
        n_cache = past // kb
        nkb = n_cache + 1
        ikt_s[...] = jnp.zeros(ikt_s.shape, bf16)
        ikt_s[0:T, :] = ik_ref[...]
        kt_s[...] = jnp.zeros(kt_s.shape, bf16)
        vt_s[...] = jnp.zeros(vt_s.shape, bf16)
        for g in range(kv_heads):
            kt_s[g, 0:T, :] = k_ref[:, g * DSA_HD:(g + 1) * DSA_HD]
            vt_s[g, 0:T, :] = v_ref[:, g * DSA_HD:(g + 1) * DSA_HD]

        def over_blocks(cache_fn, tail_fn):
            lax.fori_loop(0, n_cache, lambda b, c: (cache_fn(b), c)[1], 0)
            tail_fn()

        def cached(ref, b, g):
            return ref[pl.ds(b * (kb * kv_heads) + g, kb, stride=kv_heads), :].astype(bf16)
    else:
        vis_end = jnp.minimum(((qpos0 + tq - 1) // CHUNK + 1) * CHUNK, s_valid)
        nkb = (vis_end + kb - 1) // kb

    def score_block(c0, ikb):
        acc = jnp.zeros((tq, kb), f32)
        for h in range(idx_heads):
            s = lax.dot_general(iq_ref[:, h * IDX_HD:(h + 1) * IDX_HD], ikb, NT, preferred_element_type=f32)
            acc = acc + w[:, h:h + 1] * jnp.maximum(s, 0.0)
        kpos = c0 + lax.broadcasted_iota(jnp.int32, (1, kb), 1)
        vis = jnp.logical_and(kpos // CHUNK <= qchunk, kpos < s_valid)
        bits = pltpu.bitcast(acc, jnp.int32)
        key = bits ^ ((bits >> 31) & 0x7FFFFFFF)
        key_s[:, pl.ds(c0, kb)] = jnp.where(vis, key, INT_MIN)

    if past:
        over_blocks(lambda b: score_block(pl.multiple_of(b * kb, kb),
                                          ikc_ref[pl.ds(pl.multiple_of(b * kb, kb), kb), :].astype(bf16)),
                    lambda: score_block(n_cache * kb, ikt_s[...]))
    else:
        def score_body(b, carry):
            c0 = pl.multiple_of(b * kb, kb)
            score_block(c0, ik_ref[0, pl.ds(c0, kb), :])
            return carry

        lax.fori_loop(0, nkb, score_body, 0)

    one, zero = jnp.int32(1), jnp.int32(0)
    izeros = jnp.zeros((tq, LANE), jnp.int32)
    lane_i = lax.broadcasted_iota(jnp.int32, (1, LANE), 1)

    def count(pred):
        def body(b, cacc):
            c0 = pl.multiple_of(b * kb, kb)
            blk = key_s[:, pl.ds(c0, kb)]
            for t in range(kb // LANE):
                cacc = cacc + jnp.where(pred(blk[:, t * LANE:(t + 1) * LANE], c0 + t * LANE + lane_i), one, zero)
            return cacc
        return jnp.sum(lax.fori_loop(0, nkb, body, izeros), axis=-1, keepdims=True)

    def bit_body(i, carry):
        t_u, c_at = carry
        cand_u = t_u | lax.shift_left(one, 31 - i)
        cand = cand_u ^ INT_MIN
        cnt = count(lambda kt, it: kt >= cand)
        ok = cnt >= topk
        return jnp.where(ok, cand_u, t_u), jnp.where(ok, cnt, c_at)

    t_u, c_at = lax.fori_loop(0, 32, bit_body, (izeros, izeros))
    thr = jnp.maximum(t_u ^ INT_MIN, INT_MIN + 1)

    @pl.when(jnp.max(c_at) > topk)
    def _():
        need = topk - count(lambda kt, it: kt > thr)
        nbits = int(s_valid).bit_length()

        def idx_body(i, lo):
            cand = lo | lax.shift_left(one, nbits - 1 - i)
            c = count(lambda kt, it: jnp.logical_and(kt == thr, it < cand))
            return jnp.where(c < need, cand, lo)

        lo = lax.fori_loop(0, nbits, idx_body, izeros)

        def drop_body(b, carry):
            c0 = pl.multiple_of(b * kb, kb)
            blk = key_s[:, pl.ds(c0, kb)]
            tiles = []
            for t in range(kb // LANE):
                kt = blk[:, t * LANE:(t + 1) * LANE]
                drop = jnp.logical_and(kt == thr, c0 + t * LANE + lane_i > lo)
                tiles.append(jnp.where(drop, INT_MIN, kt))
            key_s[:, pl.ds(c0, kb)] = jnp.concatenate(tiles, axis=1)
            return carry

        lax.fori_loop(0, nkb, drop_body, 0)

    c2 = DSA_HD ** -0.5 * LOG2E
    for g in range(kv_heads):
        for i in range(group):
            hh = g * group + i
            q_s[g, i * tq:(i + 1) * tq, :] = dq_ref[:, hh * DSA_HD:(hh + 1) * DSA_HD]
    m_s[...] = jnp.full(m_s.shape, NEG_INIT, f32)
    acc_s[...] = jnp.zeros(acc_s.shape, f32)

    def attend_block(c0, kv_fn):
        keys = key_s[:, pl.ds(c0, kb)]
        bias = jnp.concatenate(
            [jnp.where(keys[:, t * LANE:(t + 1) * LANE] >= thr, 0.0, NEG_MASK) for t in range(kb // LANE)],
            axis=1)
        bias = jnp.concatenate([bias] * group, axis=0)
        ones = jnp.ones((kb, DSA_HD), bf16)
        for g in range(kv_heads):
            kblk, vblk = kv_fn(g)
            s = lax.dot_general(q_s[g], kblk, NT, preferred_element_type=f32) * c2 + bias
            m_old = m_s[g]
            m_new = jnp.maximum(m_old, jnp.max(s, axis=-1, keepdims=True))
            p = jnp.exp2((s - m_new).astype(bf16))
            alpha = jnp.exp2(m_old - m_new)
            acc_s[g] = alpha * acc_s[g] + jnp.dot(p, jnp.concatenate([vblk, ones], axis=1),
                                                  preferred_element_type=f32)
            m_s[g] = m_new

    if past:
        over_blocks(lambda b: attend_block(pl.multiple_of(b * kb, kb),
                                           lambda g: (cached(kc_ref, b, g), cached(vc_ref, b, g))),
                    lambda: attend_block(n_cache * kb, lambda g: (kt_s[g], vt_s[g])))
    else:
        def kv_body(b, carry):
            c0 = pl.multiple_of(b * kb, kb)
            attend_block(c0, lambda g: (k_ref[0, pl.ds(c0, kb), g * DSA_HD:(g + 1) * DSA_HD],
                                        v_ref[0, pl.ds(c0, kb), g * DSA_HD:(g + 1) * DSA_HD]))
            return carry

        lax.fori_loop(0, nkb, kv_body, 0)
    for g in range(kv_heads):
        out = acc_s[g][:, :DSA_HD] / acc_s[g][:, DSA_HD:DSA_HD + 1]
        for i in range(group):
            hh = g * group + i
            o_ref[:, hh * DSA_HD:(hh + 1) * DSA_HD] = out[i * tq:(i + 1) * tq].astype(o_ref.dtype)


def dsa(dq, iq, misc, k_new, v_new, ik_new, caches, *, n_seq, T, pos0, heads, kv_heads, idx_heads, tq, kb):
    nq = T // tq
    group = heads // kv_heads
    kvw = kv_heads * DSA_HD
    past = 0 if caches is None else caches[2].shape[1]
    s_valid = past + T
    topk = min(IDX_TOPK, s_valid // 4)
    in_specs = [
        pl.BlockSpec((tq, heads * DSA_HD), lambda b, j: (b * nq + j, 0)),
        pl.BlockSpec((tq, idx_heads * IDX_HD), lambda b, j: (b * nq + j, 0)),
        pl.BlockSpec((tq, LANE), lambda b, j: (b * nq + j, 0)),
    ]
    scratch = [pltpu.VMEM((kv_heads, group * tq, DSA_HD), bf16), pltpu.VMEM((kv_heads, group * tq, 1), f32),
               pltpu.VMEM((kv_heads, group * tq, 2 * DSA_HD), f32)]
    if past:
        assert nq == 1 and past % kb == 0 and T <= kb
        kc, vc, ikc, pb0 = caches
        s_pad = past + kb
        in_specs += [pl.BlockSpec((T, kvw), lambda b, j: (b, 0)), pl.BlockSpec((T, kvw), lambda b, j: (b, 0)),
                     pl.BlockSpec((T, IDX_HD), lambda b, j: (b, 0)),
                     pl.BlockSpec((None, past * kv_heads, DSA_HD), lambda b, j: (pb0 + b, 0, 0)),
                     pl.BlockSpec((None, past * kv_heads, DSA_HD), lambda b, j: (pb0 + b, 0, 0)),
                     pl.BlockSpec((None, past, IDX_HD), lambda b, j: (pb0 + b, 0, 0))]
        args = (dq, iq, misc, k_new, v_new, ik_new, kc, vc, ikc)
        scratch += [pltpu.VMEM((kb, IDX_HD), bf16), pltpu.VMEM((kv_heads, kb, DSA_HD), bf16),
                    pltpu.VMEM((kv_heads, kb, DSA_HD), bf16)]
    else:
        s_pad = T
        in_specs += [pl.BlockSpec((1, T, kvw), lambda b, j: (b, 0, 0)), pl.BlockSpec((1, T, kvw), lambda b, j: (b, 0, 0)),
                     pl.BlockSpec((1, T, IDX_HD), lambda b, j: (b, 0, 0))]
        args = (dq, iq, misc, k_new.reshape(n_seq, T, kvw), v_new.reshape(n_seq, T, kvw),
                ik_new.reshape(n_seq, T, IDX_HD))
    kern = functools.partial(_dsa_kernel, tq=tq, kb=kb, s_valid=s_valid, pos0=pos0, past=past, topk=topk,
                             heads=heads, kv_heads=kv_heads, idx_heads=idx_heads)
    return pl.pallas_call(
        kern,
        grid=(n_seq, nq),
        in_specs=in_specs,
        out_specs=pl.BlockSpec((tq, heads * DSA_HD), lambda b, j: (b * nq + j, 0)),
        out_shape=jax.ShapeDtypeStruct((n_seq * T, heads * DSA_HD), bf16),
        scratch_shapes=[pltpu.VMEM((tq, s_pad), jnp.int32)] + scratch,
        compiler_params=_cparams(("parallel", "arbitrary")),
        name="dsa",
    )(*args)


def _diff_lambda(lq1, lk1, lq2, lk2, lam_init):
    return (jnp.exp(jnp.sum(lq1[...] * lk1[...], axis=-1, keepdims=True))
            - jnp.exp(jnp.sum(lq2[...] * lk2[...], axis=-1, keepdims=True)) + lam_init)


def _diff_finish(o, gn_ref, lam_init):
    y = o * lax.rsqrt(jnp.mean(o * o, axis=-1, keepdims=True) + EPS) * gn_ref[...]
    return y * (1.0 - lam_init)


def _diff_prompt_kernel(qt_ref, kt_ref, q_ref, k_ref, v_ref, lq1, lk1, lq2, lk2, gn_ref, o_ref, m_s, l_s, acc_s,
                        *, tq, tk, lam_init):
    p_id = pl.program_id(2)
    qi = qt_ref[p_id]
    ki = kt_ref[p_id]
    last_k = ((qi + 1) * tq - 1) // tk
    c2 = DIFF_HD ** -0.5 * LOG2E

    @pl.when(ki == 0)
    def _():
        m_s[...] = jnp.full(m_s.shape, NEG_INIT, f32)
        l_s[...] = jnp.zeros(l_s.shape, f32)
        acc_s[...] = jnp.zeros(acc_s.shape, f32)

    def update(masked):
        vb = v_ref[...]
        if masked:
            qc = (qi * tq + lax.broadcasted_iota(jnp.int32, (tq, 1), 0)) // CHUNK
            kc = (ki * tk + lax.broadcasted_iota(jnp.int32, (1, tk), 1)) // CHUNK
            bias = jnp.where(kc <= qc, 0.0, NEG_MASK)
        for c in range(2):
            s = lax.dot_general(q_ref[:, c * DIFF_HD:(c + 1) * DIFF_HD], k_ref[:, c * DIFF_HD:(c + 1) * DIFF_HD],
                                NT, preferred_element_type=f32) * c2
            if masked:
                s = s + bias
            m_old = m_s[c]
            m_new = jnp.maximum(m_old, jnp.max(s, axis=-1, keepdims=True))
            p = jnp.exp2(s - m_new)
            alpha = jnp.exp2(m_old - m_new)
            l_s[c] = alpha * l_s[c] + jnp.sum(p, axis=-1, keepdims=True)
            acc_s[c] = alpha * acc_s[c] + jnp.dot(p.astype(bf16), vb, preferred_element_type=f32)
            m_s[c] = m_new

    needs_mask = (ki + 1) * tk > qi * tq + CHUNK

    @pl.when(needs_mask)
    def _():
        update(True)

    @pl.when(jnp.logical_not(needs_mask))
    def _():
        update(False)

    @pl.when(ki == last_k)
    def _():
        lam = _diff_lambda(lq1, lk1, lq2, lk2, lam_init)
        o = acc_s[0] / l_s[0] - lam * (acc_s[1] / l_s[1])
        o_ref[...] = _diff_finish(o, gn_ref, lam_init).astype(o_ref.dtype)


def diff_prompt(q, k, v, lams, gn, *, n_seq, T, heads, lam_init):
    hw = 2 * DIFF_HD
    tq = _pick(T, (DIFF_TQ, 512, 256, 128))
    tk = _pick(T, (DIFF_TK, 512, 256, 128))
    nq, nk = T // tq, T // tk
    pairs = [(qi, ki) for qi in range(nq) for ki in range(((qi + 1) * tq - 1) // tk + 1)]
    qt = jnp.asarray(np.array([p[0] for p in pairs], np.int32))
    kt = jnp.asarray(np.array([p[1] for p in pairs], np.int32))
    vec = pl.BlockSpec((1, DIFF_HD), lambda b, h, p, qt, kt: (0, 0))
    kv = pl.BlockSpec((tk, hw), lambda b, h, p, qt, kt: (b * nk + kt[p], h))
    qo = pl.BlockSpec((tq, hw), lambda b, h, p, qt, kt: (b * nq + qt[p], h))
    return pl.pallas_call(
        functools.partial(_diff_prompt_kernel, tq=tq, tk=tk, lam_init=lam_init),
        grid_spec=pltpu.PrefetchScalarGridSpec(
            num_scalar_prefetch=2,
            grid=(n_seq, heads, len(pairs)),
            in_specs=[qo, kv, kv, vec, vec, vec, vec, pl.BlockSpec((1, hw), lambda b, h, p, qt, kt: (0, 0))],
            out_specs=qo,
            scratch_shapes=[pltpu.VMEM((2, tq, 1), f32), pltpu.VMEM((2, tq, 1), f32), pltpu.VMEM((2, tq, hw), f32)],
        ),
        out_shape=jax.ShapeDtypeStruct((n_seq * T, heads * hw), bf16),
        compiler_params=_cparams(("parallel", "parallel", "arbitrary")),
        name="diff_prompt",
    )(qt, kt, q, k, v, *lams, gn)


def _diff_sample_kernel(q_ref, kn_ref, vn_ref, k0_ref, k1_ref, vlo_ref, vhi_ref, lq1, lk1, lq2, lk2, gn_ref, o_ref,
                        m_s, l_s, acc_s, flat_s, *, tk, hpb, lam_init):
    kb = pl.program_id(2)
    T = q_ref.shape[0]
    hw = 2 * DIFF_HD
    c2 = DIFF_HD ** -0.5 * LOG2E

    @pl.when(kb == 0)
    def _():
        m_s[...] = jnp.full(m_s.shape, NEG_INIT, f32)
        l_s[...] = jnp.zeros(l_s.shape, f32)
        acc_s[...] = jnp.zeros(acc_s.shape, f32)

    def update(hl, kblks, vblk):
        ps = []
        for c in range(2):
            idx = 2 * hl + c
            s = lax.dot_general(q_ref[:, idx * DIFF_HD:(idx + 1) * DIFF_HD], kblks[c], NT,
                                preferred_element_type=f32) * c2
            m_old = m_s[idx]
            m_new = jnp.maximum(m_old, jnp.max(s, axis=-1, keepdims=True))
            p = jnp.exp2(s - m_new)
            alpha = jnp.exp2(m_old - m_new)
            l_s[idx] = alpha * l_s[idx] + jnp.sum(p, axis=-1, keepdims=True)
            m_s[idx] = m_new
            acc_s[idx] = alpha * acc_s[idx]
            ps.append(p.astype(bf16))
        pv = jnp.dot(jnp.concatenate(ps, axis=0), vblk, preferred_element_type=f32)
        acc_s[2 * hl] = acc_s[2 * hl] + pv[:T]
        acc_s[2 * hl + 1] = acc_s[2 * hl + 1] + pv[T:]

    for n, ref in enumerate((k0_ref, k1_ref, vlo_ref, vhi_ref)):
        flat_s[n] = ref[...].reshape(tk * 8, DIFF_HD)

    def rows(n, r):
        return flat_s[n, pl.ds(r, tk, stride=8), :]

    for hl in range(hpb):
        quad, j0 = hl // 4, (hl % 4) * 2
        kblks = tuple(rows(quad, j0 + c).astype(bf16) for c in range(2))
        vblk = jnp.concatenate([rows(2, hl), rows(3, hl)], axis=1).astype(bf16)
        update(hl, kblks, vblk)

    @pl.when(kb == pl.num_programs(2) - 1)
    def _():
        lam = _diff_lambda(lq1, lk1, lq2, lk2, lam_init)
        for hl in range(hpb):
            kblks = tuple(kn_ref[:, (2 * hl + c) * DIFF_HD:(2 * hl + c + 1) * DIFF_HD] for c in range(2))
            update(hl, kblks, vn_ref[:, hl * hw:(hl + 1) * hw])
            o = acc_s[2 * hl] / l_s[2 * hl] - lam * (acc_s[2 * hl + 1] / l_s[2 * hl + 1])
            o_ref[:, hl * hw:(hl + 1) * hw] = _diff_finish(o, gn_ref, lam_init).astype(o_ref.dtype)


def diff_sample(q, k, v, k_past, v_past, pb0, lams, gn, *, n_seq, T, heads, lam_init):
    hw = 2 * DIFF_HD
    hpb = 8
    past = k_past.shape[1]
    assert (past + T - 1) // CHUNK == past // CHUNK and past % CHUNK == 0
    assert heads % hpb == 0
    tk = _pick(past, (DIFF_SAMPLE_TK, 512, 256, 128))
    new = pl.BlockSpec((T, hpb * hw), lambda b, o, kb: (b, o))
    vec = pl.BlockSpec((1, DIFF_HD), lambda b, o, kb: (0, 0))
    tile = (None, tk, 8, DIFF_HD)
    return pl.pallas_call(
        functools.partial(_diff_sample_kernel, tk=tk, hpb=hpb, lam_init=lam_init),
        grid=(n_seq, heads // hpb, past // tk),
        in_specs=[new, new, new,
                  pl.BlockSpec(tile, lambda b, o, kb: (pb0 + b, kb, 2 * o, 0)),
                  pl.BlockSpec(tile, lambda b, o, kb: (pb0 + b, kb, 2 * o + 1, 0)),
                  pl.BlockSpec(tile, lambda b, o, kb: (pb0 + b, kb, o, 0)),
                  pl.BlockSpec(tile, lambda b, o, kb: (pb0 + b, kb, o, 1)),
                  vec, vec, vec, vec, pl.BlockSpec((1, hw), lambda b, o, kb: (0, 0))],
        out_specs=pl.BlockSpec((T, hpb * hw), lambda b, o, kb: (b, o)),
        out_shape=jax.ShapeDtypeStruct((n_seq * T, heads * hw), bf16),
        scratch_shapes=[pltpu.VMEM((2 * hpb, T, 1), f32), pltpu.VMEM((2 * hpb, T, 1), f32),
                        pltpu.VMEM((2 * hpb, T, hw), f32), pltpu.VMEM((4, tk * 8, DIFF_HD), f32)],
        compiler_params=_cparams(("parallel", "parallel", "arbitrary")),
        name="diff_sample",
    )(q, k, v, k_past, k_past, v_past, v_past, *lams, gn)


def _ffn_up_kernel(*refs, T, tm, tn, tiled_seq, emit_w):
    it = iter(refs)
    x_ref, wa_ref, wg_ref = next(it), next(it), next(it)
    if tiled_seq:
        prev_a, prev_g = (next(it),), (next(it),)
    else:
        prev_a = (next(it), next(it))
        prev_g = (next(it), next(it))
    cwa_ref, cwg_ref, cba_ref, cbg_ref = next(it), next(it), next(it), next(it)
    o_ref, ta_ref, tg_ref = next(it), next(it), next(it)
    wb_refs = (next(it), next(it)) if emit_w else None
    was_s, wgs_s, carry_s = next(it), next(it), next(it)
    i = pl.program_id(1)

    @pl.when(i == 0)
    def _():
        was_s[...] = wa_ref[...].astype(bf16)
        wgs_s[...] = wg_ref[...].astype(bf16)
        if emit_w:
            wb_refs[0][...] = was_s[...]
            wb_refs[1][...] = wgs_s[...]

    x = x_ref[...]
    row = lax.broadcasted_iota(jnp.int32, (tm, tn), 0)

    def half(idx, ws_s, prevs, cw_ref, cb_ref, tail_ref):
        u = jnp.dot(x, ws_s[...], preferred_element_type=f32)
        if tiled_seq:
            first = (i % (T // tm)) == 0
            c0 = jnp.where(first, prevs[0][0, 0:1, :], carry_s[idx, 0:1, :])
            c1 = jnp.where(first, prevs[0][0, 1:2, :], carry_s[idx, 1:2, :])
            u1 = jnp.where(row == 0, c1, pltpu.roll(u, 1, 0))
            u2 = jnp.where(row == 0, c0, jnp.where(row == 1, c1, pltpu.roll(u, 2, 0)))
            carry_s[idx] = u[tm - 2:tm]
            tail_ref[0] = u[tm - 2:tm]
        else:
            rmod = row % T
            p1 = prevs[1][...]
            u1 = jnp.where(rmod == 0, p1, pltpu.roll(u, 1, 0))
            u2 = jnp.where(rmod == 0, prevs[0][...], jnp.where(rmod == 1, pltpu.roll(p1, 1, 0), pltpu.roll(u, 2, 0)))
            tail_ref[...] = u
        return cb_ref[...] + u2 * cw_ref[0:1, :] + u1 * cw_ref[1:2, :] + u * cw_ref[2:3, :]

    a = half(0, was_s, prev_a, cwa_ref, cba_ref, ta_ref)
    g = half(1, wgs_s, prev_g, cwg_ref, cbg_ref, tg_ref)
    o_ref[...] = (g * (1.0 / (1.0 + jnp.exp(-g))) * a).astype(o_ref.dtype)


def ffn_up(x, w, prev, pb0, cw, cb, l, *, n_seq, T, dff):
    M, K = x.shape
    tn = _pick(dff, (256, 128))
    nb = dff // tn
    tm = _pick(M, (FFN_TM, 512, 256, 128, 64, 32, 16))
    tiled_seq = tm <= T
    assert T % tm == 0 if tiled_seq else tm % T == 0
    emit_w = not isinstance(w, tuple)
    if emit_w:
        w_args = [w, w]
        w_specs = [pl.BlockSpec((None, K, tn), lambda j, i: (l, 0, j)),
                   pl.BlockSpec((None, K, tn), lambda j, i: (l, 0, nb + j))]
    else:
        w_args = list(w)
        w_specs = [pl.BlockSpec((K, tn), lambda j, i: (0, j))] * 2
    if tiled_seq:
        spt = T // tm
        p_args = [prev, prev]
        p_specs = [pl.BlockSpec((1, CONV_W - 1, tn), lambda j, i: (pb0 + i // spt, 0, j)),
                   pl.BlockSpec((1, CONV_W - 1, tn), lambda j, i: (pb0 + i // spt, 0, nb + j))]
        tail_shape = jax.ShapeDtypeStruct((n_seq, CONV_W - 1, dff), f32)
        tail_spec = pl.BlockSpec((1, CONV_W - 1, tn), lambda j, i: (i // spt, 0, j))
    else:
        pv = prev[pb0:pb0 + n_seq]
        ex = [jnp.pad(pv[:, k:k + 1], ((0, 0), (0, T - 1), (0, 0))).reshape(M, 2 * dff) for k in range(CONV_W - 1)]
        p_args = [ex[0], ex[1], ex[0], ex[1]]
        p_specs = [pl.BlockSpec((tm, tn), lambda j, i: (i, j)), pl.BlockSpec((tm, tn), lambda j, i: (i, j)),
                   pl.BlockSpec((tm, tn), lambda j, i: (i, nb + j)), pl.BlockSpec((tm, tn), lambda j, i: (i, nb + j))]
        tail_shape = jax.ShapeDtypeStruct((M, dff), f32)
        tail_spec = pl.BlockSpec((tm, tn), lambda j, i: (i, j))
    c_specs = [pl.BlockSpec((None, CONV_W, tn), lambda j, i: (l, 0, j)),
               pl.BlockSpec((None, CONV_W, tn), lambda j, i: (l, 0, nb + j)),
               pl.BlockSpec((None, 1, tn), lambda j, i: (l, 0, j)),
               pl.BlockSpec((None, 1, tn), lambda j, i: (l, 0, nb + j))]
    cb3 = cb.reshape(cb.shape[0], 1, 2 * dff)
    out_shape = [jax.ShapeDtypeStruct((M, dff), bf16), tail_shape, tail_shape]
    out_specs = [pl.BlockSpec((tm, tn), lambda j, i: (i, j)), tail_spec, tail_spec]
    if emit_w:
        out_shape += [jax.ShapeDtypeStruct((K, dff), bf16)] * 2
        out_specs += [pl.BlockSpec((K, tn), lambda j, i: (0, j))] * 2
    outs = pl.pallas_call(
        functools.partial(_ffn_up_kernel, T=T, tm=tm, tn=tn, tiled_seq=tiled_seq, emit_w=emit_w),
        grid=(nb, M // tm),
        in_specs=[pl.BlockSpec((tm, K), lambda j, i: (i, 0))] + w_specs + p_specs + c_specs,
        out_specs=out_specs,
        out_shape=out_shape,
        scratch_shapes=[pltpu.VMEM((K, tn), bf16), pltpu.VMEM((K, tn), bf16), pltpu.VMEM((2, CONV_W - 1, tn), f32)],
        compiler_params=_cparams(("parallel", "arbitrary")),
        name="ffn_up",
    )(x, *w_args, *p_args, cw, cw, cb3, cb3)
    g, ta, tg = outs[:3]
    if tiled_seq:
        state = jnp.concatenate([ta, tg], axis=-1)
    else:
        state = jnp.concatenate([ta.reshape(n_seq, T, dff)[:, T - (CONV_W - 1):],
                                 tg.reshape(n_seq, T, dff)[:, T - (CONV_W - 1):]], axis=-1)
    return g, state, (tuple(outs[3:]) if emit_w else None)


class _Stream:
    def __init__(self, x, pos0, past):
        self.n_seq, self.T, d = x.shape
        self.h = x.reshape(self.n_seq * self.T, d)
        self.pos0 = pos0
        self.past = past
        pos = jnp.tile(pos0 + jnp.arange(self.T, dtype=jnp.int32), self.n_seq)
        self.tables = rope_tables(pos)


def kernel(x_prompt, x_sample, cache_gla_state, cache_dsa_k, cache_dsa_v, cache_idx_k, cache_diff_k, cache_diff_v, state_ffn_conv, norm_mix, norm_ffn, norm_final, w_in_ab, w_gla_a2, b_gla_a, gla_norm, w_out_ab, w_in_c, lambda_q1, lambda_k1, lambda_q2, lambda_k2, diff_norm, w_out_c, w_up, conv_w, conv_b, w_down):
    B, S, D = x_prompt.shape
    DB, DS, _ = x_sample.shape
    depth = norm_mix.shape[0]
    n_ab, _, gh, gdk, gdv = cache_gla_state.shape
    n_c = cache_diff_k.shape[0]
    past = cache_dsa_k.shape[2]
    kvh = cache_dsa_k.shape[3]
    kvw = kvh * DSA_HD
    dh = w_out_ab.shape[1] - gh * gdv
    dsa_heads = dh // DSA_HD
    idx_heads = (w_in_ab.shape[2] - (2 * gh * gdk + 2 * gh * gdv + GLA_RANK + dh + 2 * kvw + IDX_HD)) // (IDX_HD + 1)
    diff_heads = cache_diff_k.shape[3]
    dff = w_down.shape[1]
    cw_ = diff_heads * 2 * DIFF_HD

    streams = (_Stream(x_prompt, 0, 0), _Stream(x_sample, past, past))
    names = ("gla", "dk", "dv", "ik", "ck", "cv", "conv")
    outs = [{k: [] for k in names} for _ in streams]

    c_gla = cache_gla_state.reshape(n_ab * DB, gh, gdk, gdv)
    c_dk = cache_dsa_k.reshape(n_ab * DB, past * kvh, DSA_HD)
    c_dv = cache_dsa_v.reshape(n_ab * DB, past * kvh, DSA_HD)
    c_ik = cache_idx_k.reshape(n_ab * DB, past, IDX_HD)
    c_ck = cache_diff_k.reshape(n_c * DB, past, 2 * diff_heads, DIFF_HD)
    c_cv = cache_diff_v.reshape(n_c * DB, past, diff_heads, 2 * DIFF_HD)
    c_conv = state_ffn_conv.reshape(depth * DB, CONV_W - 1, 2 * dff)
    zero_state = jnp.zeros((B, gh, gdk, gdv), f32)
    zero_conv = jnp.zeros((B, CONV_W - 1, 2 * dff), f32)
    wd_all = w_down.astype(bf16)

    for l in range(depth):
        i = l // 2
        if l % 2 == 0:
            sizes = (gh * gdk, gh * gdk, gh * gdv, gh * gdv, GLA_RANK, dh, kvw, kvw, idx_heads * IDX_HD, IDX_HD,
                     idx_heads)
            offs = np.concatenate([[0], np.cumsum(sizes)]).tolist()
            tn_ab = kvw
            n_body = offs[10] - GLA_RANK
            n_out = -(-(n_body + LANE) // tn_ab) * tn_ab
            w_perm = prep_w_ab(w_in_ab, i, c_ga=offs[4], c_end_ik=offs[10], n_out=n_out)
            ab_segs = [(offs[4], False, (f32,)), (dh, True, (bf16,)), (kvw, True, (f32, bf16)),
                       (kvw, False, (f32, bf16)), (idx_heads * IDX_HD, True, (bf16,)),
                       (tn_ab, [(IDX_HD, True, (f32, bf16)), (LANE, False, (f32,))])]
            c_gq = 0
            c_gk = c_gq + gh * gdk
            c_gv = c_gk + gh * gdk
            c_gg = c_gv + gh * gdv
            wa2 = w_gla_a2[i].astype(bf16)
            ba = b_gla_a[i].reshape(1, gh * gdk)
            gn = gla_norm[i].reshape(1, gdv)
            w_out = w_out_ab[i].astype(bf16)
            for si, st in enumerate(streams):
                hn = rmsnorm(st.h, norm_mix[l], bf16)
                zg, dq_b, dk_f, dk_b, dv_f, dv_b, iq_b, ik_f, ik_b, misc = proj(hn, w_perm, st.tables, ab_segs, tn_ab,
                                                                                 "in_ab")
                if st.past:
                    s0, sb0 = c_gla, i * DB
                else:
                    s0, sb0 = zero_state, 0
                go, gst = gla(zg, misc, s0, sb0, wa2, ba, gn, n_seq=st.n_seq, T=st.T, heads=gh, dk=gdk, dv=gdv,
                              cols=(c_gq, c_gk, c_gv, c_gg))
                if st.past:
                    tq, kb, caches = st.T, DSA_KB_SAMPLE, (c_dk, c_dv, c_ik, i * DB)
                else:
                    tq, kb, caches = _pick(st.T, (DSA_TQ,)), _pick(st.T, (DSA_KB, 512, 256, 128)), None
                do = dsa(dq_b, iq_b, misc, dk_b, dv_b, ik_b, caches, n_seq=st.n_seq, T=st.T, pos0=st.pos0,
                         heads=dsa_heads, kv_heads=kvh, idx_heads=idx_heads, tq=tq, kb=kb)
                st.h = matmul((go, do), w_out, f32, res=st.h, name="out_ab")
                o = outs[si]
                o["gla"].append(gst)
                o["dk"].append(dk_f.reshape(st.n_seq, st.T, kvh, DSA_HD))
                o["dv"].append(dv_f.reshape(st.n_seq, st.T, kvh, DSA_HD))
                o["ik"].append(ik_f.reshape(st.n_seq, st.T, IDX_HD))
        else:
            lam_init = 0.8 - 0.6 * math.exp(-0.3 * l)
            w_in = w_in_c[i].astype(bf16)
            w_out = w_out_c[i].astype(bf16)
            lams = tuple(a[i].reshape(1, DIFF_HD) for a in (lambda_q1, lambda_k1, lambda_q2, lambda_k2))
            gn = diff_norm[i].reshape(1, 2 * DIFF_HD)
            c_segs = [(cw_, True, (bf16,)), (cw_, True, (f32, bf16)), (cw_, False, (f32, bf16))]
            for si, st in enumerate(streams):
                hn = rmsnorm(st.h, norm_mix[l], bf16)
                q_b, k_f, k_b, v_f, v_b = proj(hn, w_in, st.tables, c_segs, min(512, cw_), "in_c")
                if st.past:
                    at = diff_sample(q_b, k_b, v_b, c_ck, c_cv, i * DB, lams, gn, n_seq=st.n_seq, T=st.T,
                                     heads=diff_heads, lam_init=lam_init)
                else:
                    at = diff_prompt(q_b, k_b, v_b, lams, gn, n_seq=st.n_seq, T=st.T, heads=diff_heads,
                                     lam_init=lam_init)
                st.h = matmul(at, w_out, f32, res=st.h, name="out_c")
                o = outs[si]
                o["ck"].append(k_f.reshape(st.n_seq, st.T, diff_heads, 2, DIFF_HD))
                o["cv"].append(v_f.reshape(st.n_seq, st.T, diff_heads, 2 * DIFF_HD))

        wu = w_up
        for si, st in enumerate(streams):
            hn = rmsnorm(st.h, norm_ffn[l], bf16)
            prev, pb0 = (c_conv, l * DB) if st.past else (zero_conv, 0)
            g, conv_state, wu_b = ffn_up(hn, wu, prev, pb0, conv_w, conv_b, l, n_seq=st.n_seq, T=st.T, dff=dff)
            if wu_b is not None:
                wu = wu_b
            st.h = matmul(g, wd_all, f32, res=st.h, name="ffn_down", layer=l)
            outs[si]["conv"].append(conv_state)

    ys = [rmsnorm(st.h, norm_final, f32).reshape(st.n_seq, st.T, D) for st in streams]
    tail = [jnp.stack(o[k]) for o in outs for k in names]
    return (ys[0], ys[1], *tail)
```

```python
import functools
import math

import numpy as np
import jax
import jax.numpy as jnp
from jax import lax
from jax.experimental import pallas as pl
from jax.experimental.pallas import tpu as pltpu

CHUNK = 64
EPS = 1e-6
ROPE_THETA = 10000.0
GLA_RANK = 16
GLA_TAU = 16.0
DSA_HD = 128
IDX_HD = 128
IDX_TOPK = 256
DIFF_HD = 128
CONV_W = 3

LANE = 128
SUBLANE = 8
PROJ_TN = 512
GLA_SUB = 16
DSA_TQ, DSA_KB = 128, 1024
DSA_KB_SAMPLE = 1024
DIFF_TQ, DIFF_TK = 1024, 1024
FFN_TM = 1024
DIFF_SAMPLE_TK = 512
VMEM_LIMIT = 60 * 1024 * 1024
MM_VMEM_BUDGET = 52 * 1024 * 1024
NEG_INIT = -1e30
NEG_MASK = -3e38
INT_MIN = -2147483648
LOG2E = 1.4426950408889634

f32 = jnp.float32
bf16 = jnp.bfloat16
NT = (((1,), (1,)), ((), ()))
TN = (((0,), (0,)), ((), ()))


def _cparams(sem):
    return pltpu.CompilerParams(dimension_semantics=sem, vmem_limit_bytes=VMEM_LIMIT)


def _pick(dim, cands):
    for c in cands:
        if c <= dim and dim % c == 0:
            return c
    return dim


def _rmsnorm_kernel(x_ref, g_ref, o_ref):
    x = x_ref[...]
    y = x * lax.rsqrt(jnp.mean(x * x, axis=-1, keepdims=True) + EPS)
    o_ref[...] = (y * g_ref[...]).astype(o_ref.dtype)


def rmsnorm(x, g, out_dtype):
    M, D = x.shape
    tm = _pick(M, (256, 128, 64, 32, 16, 8))
    return pl.pallas_call(
        _rmsnorm_kernel,
        grid=(M // tm,),
        in_specs=[pl.BlockSpec((tm, D), lambda i: (i, 0)), pl.BlockSpec((1, D), lambda i: (0, 0))],
        out_specs=pl.BlockSpec((tm, D), lambda i: (i, 0)),
        out_shape=jax.ShapeDtypeStruct((M, D), out_dtype),
        compiler_params=_cparams(("parallel",)),
        name="rmsnorm",
    )(x, g.reshape(1, D).astype(f32))


def _mm_kernel(*refs, n_x, has_res):
    x_refs, w_ref = refs[:n_x], refs[n_x]
    o_ref = refs[-1]
    acc = None
    k0 = 0
    for x_ref in x_refs:
        kw = x_ref.shape[1]
        part = jnp.dot(x_ref[...], w_ref[k0:k0 + kw, :], preferred_element_type=f32)
        acc = part if acc is None else acc + part
        k0 += kw
    if has_res:
        acc = refs[n_x + 1][...] + acc
    o_ref[...] = acc.astype(o_ref.dtype)


def matmul(xs, w, out_dtype, res=None, name="matmul", layer=None):
    xs = tuple(xs) if isinstance(xs, (tuple, list)) else (xs,)
    M = xs[0].shape[0]
    K, N = w.shape[-2:]
    ob = jnp.dtype(out_dtype).itemsize
    best = None
    for tm in (1024, 768, 512, 384, 256, 128, 64, 32, 16, 8):
        if M % tm:
            continue
        for tn in (1024, 768, 512, 384, 256, 128):
            if N % tn:
                continue
            vm = 2 * (tm * K * 2 + K * tn * 2 + tm * tn * ob + (tm * tn * 4 if res is not None else 0))
            if vm > MM_VMEM_BUDGET:
                continue
            score = (tm * tn) / (tm + tn)
            if best is None or score > best[0]:
                best = (score, tm, tn)
    _, tm, tn = best
    in_specs = [pl.BlockSpec((tm, x.shape[1]), lambda i, j: (i, 0)) for x in xs]
    if layer is None:
        in_specs.append(pl.BlockSpec((K, tn), lambda i, j: (0, j)))
    else:
        in_specs.append(pl.BlockSpec((None, K, tn), lambda i, j: (layer, 0, j)))
    args = list(xs) + [w]
    if res is not None:
        in_specs.append(pl.BlockSpec((tm, tn), lambda i, j: (i, j)))
        args.append(res)
    return pl.pallas_call(
        functools.partial(_mm_kernel, n_x=len(xs), has_res=res is not None),
        grid=(M // tm, N // tn),
        in_specs=in_specs,
        out_specs=pl.BlockSpec((tm, tn), lambda i, j: (i, j)),
        out_shape=jax.ShapeDtypeStruct((M, N), out_dtype),
        compiler_params=_cparams(("parallel", "arbitrary")),
        name=name,
    )(*args)


def _proj_kernel(x_ref, w_ref, c_ref, s_ref, *o_refs, segs, tn):
    j = pl.program_id(1)

    @pl.when(j == 0)
    def _():
        for (j0, j1, rotate, out_ids, widths) in segs:
            if j0 > 0:
                for k in jax.tree_util.tree_leaves(out_ids):
                    o_refs[k][...] = jnp.zeros(o_refs[k].shape, o_refs[k].dtype)

    for (j0, j1, rotate, out_ids, widths) in segs:
        @pl.when(jnp.logical_and(j >= j0, j < j1))
        def _(rotate=rotate, out_ids=out_ids, widths=widths):
            acc = jnp.dot(x_ref[...], w_ref[...], preferred_element_type=f32)
            cos, sin = c_ref[...], s_ref[...]
            col = 0
            for part, k_ids in enumerate(out_ids if widths else (out_ids,)):
                wcols = widths[part] if widths else tn
                rot = rotate[part] if widths else rotate
                for g in range(wcols // LANE):
                    y = acc[:, col + g * LANE:col + (g + 1) * LANE]
                    if rot:
                        y = y * cos + pltpu.roll(y, LANE // 2, 1) * sin
                    for k in (k_ids if isinstance(k_ids, tuple) else (k_ids,)):
                        o_refs[k][:, g * LANE:(g + 1) * LANE] = y.astype(o_refs[k].dtype)
                col += wcols


def proj(x, w, tables, segs_cols, tn, name):
    M, K = x.shape
    N = w.shape[1]
    tm = _pick(M, (1024, 768, 512, 256, 128, 64, 32, 16, 8))
    segs, out_shape, out_specs = [], [], []
    j0 = 0
    for sc in segs_cols:
        n_cols = sc[0]
        assert n_cols % tn == 0 or (len(sc) == 2 and n_cols <= tn)
        nj = max(n_cols // tn, 1)
        lo, hi = j0, j0 + nj

        def omap(i, j, lo=lo, hi=hi):
            return (i, jnp.clip(j - lo, 0, hi - lo - 1))

        if len(sc) == 3:
            _, rotate, dtypes = sc
            ids = []
            for dt in dtypes:
                ids.append(len(out_shape))
                out_shape.append(jax.ShapeDtypeStruct((M, n_cols), dt))
                out_specs.append(pl.BlockSpec((tm, tn), omap))
            segs.append((lo, hi, rotate, tuple(ids), None))
        else:
            parts = sc[1]
            ids, widths, rots = [], [], []
            for (width, rotate, dtypes) in parts:
                pid = []
                for dt in dtypes:
                    pid.append(len(out_shape))
                    out_shape.append(jax.ShapeDtypeStruct((M, width), dt))
                    out_specs.append(pl.BlockSpec((tm, width), lambda i, j: (i, 0)))
                ids.append(tuple(pid))
                widths.append(width)
                rots.append(rotate)
            segs.append((lo, hi, tuple(rots), tuple(ids), tuple(widths)))
        j0 = hi
    assert j0 * tn == N, (j0, tn, N)
    tab = pl.BlockSpec((tm, LANE), lambda i, j: (i, 0))
    return pl.pallas_call(
        functools.partial(_proj_kernel, segs=tuple(segs), tn=tn),
        grid=(M // tm, N // tn),
        in_specs=[pl.BlockSpec((tm, K), lambda i, j: (i, 0)), pl.BlockSpec((K, tn), lambda i, j: (0, j)), tab, tab],
        out_specs=out_specs,
        out_shape=out_shape,
        compiler_params=_cparams(("parallel", "arbitrary")),
        name=name,
    )(x, w, *tables)


def _prep_w_ab_kernel(a_ref, b_ref, o_ref, *, j_shift, j_misc, shift):
    j = pl.program_id(0)

    @pl.when(j < j_shift)
    def _():
        o_ref[...] = a_ref[...].T.astype(bf16)

    @pl.when(jnp.logical_and(j >= j_shift, j < j_misc))
    def _():
        o_ref[...] = jnp.concatenate([a_ref[shift:, :], b_ref[...]], axis=0).T.astype(bf16)

    @pl.when(j == j_misc)
    def _():
        pad = jnp.zeros((LANE - 2 * shift, a_ref.shape[1]), f32)
        o_ref[...] = jnp.concatenate([a_ref[:shift, :], b_ref[...], pad], axis=0).T.astype(bf16)

    @pl.when(j > j_misc)
    def _():
        o_ref[...] = jnp.zeros(o_ref.shape, bf16)


def prep_w_ab(w, layer, *, c_ga, c_end_ik, n_out):
    _, K, n_in = w.shape
    shift = GLA_RANK
    assert c_ga % LANE == 0 and (c_end_ik - shift) % LANE == 0 and n_in - c_end_ik == shift
    j_shift = c_ga // LANE
    j_misc = (c_end_ik - shift) // LANE
    last = (n_in - 1) // LANE

    def amap(j):
        return (layer, jnp.where(j == j_misc, j_shift, jnp.minimum(j, last)), 0)

    per = LANE // shift
    assert LANE % shift == 0 and shift % SUBLANE == 0

    def bmap(j):
        return (layer, jnp.where(j == j_misc, j_misc * per + 1, jnp.minimum((j + 1) * per, (n_in - 1) // shift)), 0)

    wt = jnp.swapaxes(w, 1, 2)
    return pl.pallas_call(
        functools.partial(_prep_w_ab_kernel, j_shift=j_shift, j_misc=j_misc, shift=shift),
        grid=(n_out // LANE,),
        in_specs=[pl.BlockSpec((None, LANE, K), amap), pl.BlockSpec((None, shift, K), bmap)],
        out_specs=pl.BlockSpec((K, LANE), lambda j: (0, j)),
        out_shape=jax.ShapeDtypeStruct((K, n_out), bf16),
        compiler_params=_cparams(("parallel",)),
        name="prep_w_ab",
    )(wt, wt)


def rope_tables(pos):
    half = LANE // 2
    inv = 1.0 / (ROPE_THETA ** (jnp.arange(half, dtype=f32) * (2.0 / LANE)))
    ang = pos.astype(f32)[:, None] * inv[None, :]
    cos, sin = jnp.cos(ang), jnp.sin(ang)
    return jnp.concatenate([cos, cos], axis=-1), jnp.concatenate([-sin, sin], axis=-1)


def _gla_kernel(q_ref, k_ref, v_ref, gg_ref, misc_ref, wa2_ref, ba_ref, gn_ref, s0_ref,
                o_ref, sout_ref, st_ref, *, c, n_inner, dk, dv):
    step = pl.program_id(2)

    @pl.when(step == 0)
    def _():
        st_ref[...] = s0_ref[0, 0].T

    nsub = c // GLA_SUB
    row_c = lax.broadcasted_iota(jnp.int32, (c, dk), 0)
    row_a = lax.broadcasted_iota(jnp.int32, (c, c), 0)
    lane_a = lax.broadcasted_iota(jnp.int32, (c, c), 1)
    sub_a = row_a % GLA_SUB

    def chunk(ci, carry):
        r = pl.multiple_of(ci * c, c)
        q = q_ref[pl.ds(r, c), :] * (dk ** -0.5)
        k = k_ref[pl.ds(r, c), :]
        v = v_ref[pl.ds(r, c), :]
        ga = misc_ref[pl.ds(r, c), :][:, :GLA_RANK]
        x = jnp.dot(ga.astype(bf16), wa2_ref[...], preferred_element_type=f32) + ba_ref[...]
        loga = (jnp.minimum(x, 0.0) - jnp.log1p(jnp.exp(-jnp.abs(x)))) / GLA_TAU
        b = loga
        d = 1
        while d < c:
            b = b + jnp.where(row_c >= d, pltpu.roll(b, d, 0), 0.0)
            d *= 2
        st = st_ref[...]
        vb = v.astype(bf16)
        o_inter = lax.dot_general((q * jnp.exp(b)).astype(bf16), st.astype(bf16), NT,
                                  preferred_element_type=f32)
        att = jnp.zeros((c, c), f32)
        for delta in range(GLA_SUB):
            if delta == 0:
                w = q * k
            else:
                w = q * pltpu.roll(k, delta, 0) * jnp.exp(jnp.minimum(b - pltpu.roll(b, delta, 0), 0.0))
            col = jnp.sum(w, axis=-1, keepdims=True)
            att = jnp.where(jnp.logical_and(lane_a == row_a - delta, sub_a >= delta), col, att)
        if nsub > 1:
            offs = [jnp.zeros((GLA_SUB, c), f32)]
            for i in range(1, nsub):
                r0 = i * GLA_SUB
                bref = b[r0 - 1:r0]
                qq = q[r0:r0 + GLA_SUB] * jnp.exp(b[r0:r0 + GLA_SUB] - bref)
                kk = jnp.where(row_c < r0, k * jnp.exp(jnp.minimum(bref - b, 0.0)), 0.0)
                offs.append(lax.dot_general(qq.astype(bf16), kk.astype(bf16), NT, preferred_element_type=f32))
            att = att + jnp.concatenate(offs, axis=0)
        o = o_inter + jnp.dot(att.astype(bf16), vb, preferred_element_type=f32)
        b_last = b[c - 1:c]
        kd = k * jnp.exp(b_last - b)
        st_ref[...] = jnp.exp(b_last) * st + lax.dot_general(vb, kd.astype(bf16), TN, preferred_element_type=f32)
        y = o * lax.rsqrt(jnp.mean(o * o, axis=-1, keepdims=True) + EPS) * gn_ref[...]
        gg = gg_ref[pl.ds(r, c), :]
        y = y * (gg * (1.0 / (1.0 + jnp.exp(-gg))))
        o_ref[pl.ds(r, c), :] = y.astype(o_ref.dtype)
        return carry

    lax.fori_loop(0, n_inner, chunk, 0)

    @pl.when(step == pl.num_programs(2) - 1)
    def _():
        sout_ref[0, 0] = st_ref[...].T


def gla(z, misc, s0, sb0, wa2, ba, gn, *, n_seq, T, heads, dk, dv, cols):
    c = min(CHUNK, T)
    tb = max(_pick(T, (256, 128, 64, 32, 16)), c)
    n_steps = T // tb
    cq, ck, cv, cg = cols
    in_specs = [
        pl.BlockSpec((tb, dk), lambda b, h, s: (b * n_steps + s, cq // dk + h)),
        pl.BlockSpec((tb, dk), lambda b, h, s: (b * n_steps + s, ck // dk + h)),
        pl.BlockSpec((tb, dv), lambda b, h, s: (b * n_steps + s, cv // dv + h)),
        pl.BlockSpec((tb, dv), lambda b, h, s: (b * n_steps + s, cg // dv + h)),
        pl.BlockSpec((tb, LANE), lambda b, h, s: (b * n_steps + s, 0)),
        pl.BlockSpec((GLA_RANK, dk), lambda b, h, s: (0, h)),
        pl.BlockSpec((1, dk), lambda b, h, s: (0, h)),
        pl.BlockSpec((1, dv), lambda b, h, s: (0, 0)),
        pl.BlockSpec((1, 1, dk, dv), lambda b, h, s: (sb0 + b, h, 0, 0)),
    ]
    out_specs = [
        pl.BlockSpec((tb, dv), lambda b, h, s: (b * n_steps + s, h)),
        pl.BlockSpec((1, 1, dk, dv), lambda b, h, s: (b, h, 0, 0)),
    ]
    return pl.pallas_call(
        functools.partial(_gla_kernel, c=c, n_inner=tb // c, dk=dk, dv=dv),
        grid=(n_seq, heads, n_steps),
        in_specs=in_specs,
        out_specs=out_specs,
        out_shape=[jax.ShapeDtypeStruct((n_seq * T, heads * dv), bf16),
                   jax.ShapeDtypeStruct((n_seq, heads, dk, dv), f32)],
        scratch_shapes=[pltpu.VMEM((dv, dk), f32)],
        compiler_params=_cparams(("parallel", "parallel", "arbitrary")),
        name="gla",
    )(z, z, z, z, misc, wa2, ba, gn, s0)


def _dsa_kernel(*refs, tq, kb, s_valid, pos0, past, topk, heads, kv_heads, idx_heads):
    if past:
        (dq_ref, iq_ref, misc_ref, k_ref, v_ref, ik_ref, kc_ref, vc_ref, ikc_ref, o_ref,
         key_s, q_s, m_s, acc_s, ikt_s, kt_s, vt_s) = refs
    else:
        dq_ref, iq_ref, misc_ref, k_ref, v_ref, ik_ref, o_ref, key_s, q_s, m_s, acc_s = refs
    j = pl.program_id(1)
    qpos0 = pos0 + j * tq
    qchunk = (qpos0 + lax.broadcasted_iota(jnp.int32, (tq, 1), 0)) // CHUNK
    w = misc_ref[...][:, GLA_RANK:GLA_RANK + idx_heads]
    group = heads // kv_heads
    T = dq_ref.shape[0]

    if past:
        n_cache = past // kb
        nkb = n_cache + 1
        ikt_s[...] = jnp.zeros(ikt_s.shape, bf16)
        ikt_s[0:T, :] = ik_ref[...]
        kt_s[...] = jnp.zeros(kt_s.shape, bf16)
        vt_s[...] = jnp.zeros(vt_s.shape, bf16)
        for g in range(kv_heads):
            kt_s[g, 0:T, :] = k_ref[:, g * DSA_HD:(g + 1) * DSA_HD]
            vt_s[g, 0:T, :] = v_ref[:, g * DSA_HD:(g + 1) * DSA_HD]

        def over_blocks(cache_fn, tail_fn):
            lax.fori_loop(0, n_cache, lambda b, c: (cache_fn(b), c)[1], 0)
            tail_fn()

        def cached(ref, b, g):
            return ref[pl.ds(b * (kb * kv_heads) + g, kb, stride=kv_heads), :].astype(bf16)
    else:
        vis_end = jnp.minimum(((qpos0 + tq - 1) // CHUNK + 1) * CHUNK, s_valid)
        nkb = (vis_end + kb - 1) // kb

    def score_block(c0, ikb):
        acc = jnp.zeros((tq, kb), f32)
        for h in range(idx_heads):
            s = lax.dot_general(iq_ref[:, h * IDX_HD:(h + 1) * IDX_HD], ikb, NT, preferred_element_type=f32)
            acc = acc + w[:, h:h + 1] * jnp.maximum(s, 0.0)
        kpos = c0 + lax.broadcasted_iota(jnp.int32, (1, kb), 1)
        vis = jnp.logical_and(kpos // CHUNK <= qchunk, kpos < s_valid)
        bits = pltpu.bitcast(acc, jnp.int32)
        key = bits ^ ((bits >> 31) & 0x7FFFFFFF)
        key_s[:, pl.ds(c0, kb)] = jnp.where(vis, key, INT_MIN)

    if past:
        over_blocks(lambda b: score_block(pl.multiple_of(b * kb, kb),
                                          ikc_ref[pl.ds(pl.multiple_of(b * kb, kb), kb), :].astype(bf16)),
                    lambda: score_block(n_cache * kb, ikt_s[...]))
    else:
        def score_body(b, carry):
            c0 = pl.multiple_of(b * kb, kb)
            score_block(c0, ik_ref[0, pl.ds(c0, kb), :])
            return carry

        lax.fori_loop(0, nkb, score_body, 0)

    one, zero = jnp.int32(1), jnp.int32(0)
    izeros = jnp.zeros((tq, LANE), jnp.int32)
    lane_i = lax.broadcasted_iota(jnp.int32, (1, LANE), 1)

    def count(pred):
        def body(b, cacc):
            c0 = pl.multiple_of(b * kb, kb)
            blk = key_s[:, pl.ds(c0, kb)]
            for t in range(kb // LANE):
                cacc = cacc + jnp.where(pred(blk[:, t * LANE:(t + 1) * LANE], c0 + t * LANE + lane_i), one, zero)
            return cacc
        return jnp.sum(lax.fori_loop(0, nkb, body, izeros), axis=-1, keepdims=True)

    def bit_body(i, carry):
        t_u, c_at = carry
        cand_u = t_u | lax.shift_left(one, 31 - i)
        cand = cand_u ^ INT_MIN
        cnt = count(lambda kt, it: kt >= cand)
        ok = cnt >= topk
        return jnp.where(ok, cand_u, t_u), jnp.where(ok, cnt, c_at)

    t_u, c_at = lax.fori_loop(0, 32, bit_body, (izeros, izeros))
    thr = jnp.maximum(t_u ^ INT_MIN, INT_MIN + 1)

    @pl.when(jnp.max(c_at) > topk)
    def _():
        need = topk - count(lambda kt, it: kt > thr)
        nbits = int(s_valid).bit_length()

        def idx_body(i, lo):
            cand = lo | lax.shift_left(one, nbits - 1 - i)
            c = count(lambda kt, it: jnp.logical_and(kt == thr, it < cand))
            return jnp.where(c < need, cand, lo)

        lo = lax.fori_loop(0, nbits, idx_body, izeros)

        def drop_body(b, carry):
            c0 = pl.multiple_of(b * kb, kb)
            blk = key_s[:, pl.ds(c0, kb)]
            tiles = []
            for t in range(kb // LANE):
                kt = blk[:, t * LANE:(t + 1) * LANE]
                drop = jnp.logical_and(kt == thr, c0 + t * LANE + lane_i > lo)
                tiles.append(jnp.where(drop, INT_MIN, kt))
            key_s[:, pl.ds(c0, kb)] = jnp.concatenate(tiles, axis=1)
            return carry

        lax.fori_loop(0, nkb, drop_body, 0)

    c2 = DSA_HD ** -0.5 * LOG2E
    for g in range(kv_heads):
        for i in range(group):
            hh = g * group + i
            q_s[g, i * tq:(i + 1) * tq, :] = dq_ref[:, hh * DSA_HD:(hh + 1) * DSA_HD]
    m_s[...] = jnp.full(m_s.shape, NEG_INIT, f32)
    acc_s[...] = jnp.zeros(acc_s.shape, f32)

    def attend_block(c0, kv_fn):
        keys = key_s[:, pl.ds(c0, kb)]
        bias = jnp.concatenate(
            [jnp.where(keys[:, t * LANE:(t + 1) * LANE] >= thr, 0.0, NEG_MASK) for t in range(kb // LANE)],
            axis=1)
        bias = jnp.concatenate([bias] * group, axis=0)
        ones = jnp.ones((kb, DSA_HD), bf16)
        for g in range(kv_heads):
            kblk, vblk = kv_fn(g)
            s = lax.dot_general(q_s[g], kblk, NT, preferred_element_type=f32) * c2 + bias
            m_old = m_s[g]
            m_new = jnp.maximum(m_old, jnp.max(s, axis=-1, keepdims=True))
            p = jnp.exp2((s - m_new).astype(bf16))
            alpha = jnp.exp2(m_old - m_new)
            acc_s[g] = alpha * acc_s[g] + jnp.dot(p, jnp.concatenate([vblk, ones], axis=1),
                                                  preferred_element_type=f32)
            m_s[g] = m_new

    if past:
        over_blocks(lambda b: attend_block(pl.multiple_of(b * kb, kb),
                                           lambda g: (cached(kc_ref, b, g), cached(vc_ref, b, g))),
                    lambda: attend_block(n_cache * kb, lambda g: (kt_s[g], vt_s[g])))
    else:
        def kv_body(b, carry):
            c0 = pl.multiple_of(b * kb, kb)
            attend_block(c0, lambda g: (k_ref[0, pl.ds(c0, kb), g * DSA_HD:(g + 1) * DSA_HD],
                                        v_ref[0, pl.ds(c0, kb), g * DSA_HD:(g + 1) * DSA_HD]))
            return carry

        lax.fori_loop(0, nkb, kv_body, 0)
    for g in range(kv_heads):
        out = acc_s[g][:, :DSA_HD] / acc_s[g][:, DSA_HD:DSA_HD + 1]
        for i in range(group):
            hh = g * group + i
            o_ref[:, hh * DSA_HD:(hh + 1) * DSA_HD] = out[i * tq:(i + 1) * tq].astype(o_ref.dtype)


def dsa(dq, iq, misc, k_new, v_new, ik_new, caches, *, n_seq, T, pos0, heads, kv_heads, idx_heads, tq, kb):
    nq = T // tq
    group = heads // kv_heads
    kvw = kv_heads * DSA_HD
    past = 0 if caches is None else caches[2].shape[1]
    s_valid = past + T
    topk = min(IDX_TOPK, s_valid // 4)
    in_specs = [
        pl.BlockSpec((tq, heads * DSA_HD), lambda b, j: (b * nq + j, 0)),
        pl.BlockSpec((tq, idx_heads * IDX_HD), lambda b, j: (b * nq + j, 0)),
        pl.BlockSpec((tq, LANE), lambda b, j: (b * nq + j, 0)),
    ]
    scratch = [pltpu.VMEM((kv_heads, group * tq, DSA_HD), bf16), pltpu.VMEM((kv_heads, group * tq, 1), f32),
               pltpu.VMEM((kv_heads, group * tq, 2 * DSA_HD), f32)]
    if past:
        assert nq == 1 and past % kb == 0 and T <= kb
        kc, vc, ikc, pb0 = caches
        s_pad = past + kb
        in_specs += [pl.BlockSpec((T, kvw), lambda b, j: (b, 0)), pl.BlockSpec((T, kvw), lambda b, j: (b, 0)),
                     pl.BlockSpec((T, IDX_HD), lambda b, j: (b, 0)),
                     pl.BlockSpec((None, past * kv_heads, DSA_HD), lambda b, j: (pb0 + b, 0, 0)),
                     pl.BlockSpec((None, past * kv_heads, DSA_HD), lambda b, j: (pb0 + b, 0, 0)),
                     pl.BlockSpec((None, past, IDX_HD), lambda b, j: (pb0 + b, 0, 0))]
        args = (dq, iq, misc, k_new, v_new, ik_new, kc, vc, ikc)
        scratch += [pltpu.VMEM((kb, IDX_HD), bf16), pltpu.VMEM((kv_heads, kb, DSA_HD), bf16),
                    pltpu.VMEM((kv_heads, kb, DSA_HD), bf16)]
    else:
        s_pad = T
        in_specs += [pl.BlockSpec((1, T, kvw), lambda b, j: (b, 0, 0)), pl.BlockSpec((1, T, kvw), lambda b, j: (b, 0, 0)),
                     pl.BlockSpec((1, T, IDX_HD), lambda b, j: (b, 0, 0))]
        args = (dq, iq, misc, k_new.reshape(n_seq, T, kvw), v_new.reshape(n_seq, T, kvw),
                ik_new.reshape(n_seq, T, IDX_HD))
    kern = functools.partial(_dsa_kernel, tq=tq, kb=kb, s_valid=s_valid, pos0=pos0, past=past, topk=topk,
                             heads=heads, kv_heads=kv_heads, idx_heads=idx_heads)
    return pl.pallas_call(
        kern,
        grid=(n_seq, nq),
        in_specs=in_specs,
        out_specs=pl.BlockSpec((tq, heads * DSA_HD), lambda b, j: (b * nq + j, 0)),
        out_shape=jax.ShapeDtypeStruct((n_seq * T, heads * DSA_HD), bf16),
        scratch_shapes=[pltpu.VMEM((tq, s_pad), jnp.int32)] + scratch,
        compiler_params=_cparams(("parallel", "arbitrary")),
        name="dsa",
    )(*args)


def _diff_lambda(lq1, lk1, lq2, lk2, lam_init):
    return (jnp.exp(jnp.sum(lq1[...] * lk1[...], axis=-1, keepdims=True))
            - jnp.exp(jnp.sum(lq2[...] * lk2[...], axis=-1, keepdims=True)) + lam_init)


def _diff_finish(o, gn_ref, lam_init):
    y = o * lax.rsqrt(jnp.mean(o * o, axis=-1, keepdims=True) + EPS) * gn_ref[...]
    return y * (1.0 - lam_init)


def _diff_prompt_kernel(qt_ref, kt_ref, q_ref, k_ref, v_ref, lq1, lk1, lq2, lk2, gn_ref, o_ref, m_s, l_s, acc_s,
                        *, tq, tk, lam_init):
    p_id = pl.program_id(2)
    qi = qt_ref[p_id]
    ki = kt_ref[p_id]
    last_k = ((qi + 1) * tq - 1) // tk
    c2 = DIFF_HD ** -0.5 * LOG2E

    @pl.when(ki == 0)
    def _():
        m_s[...] = jnp.full(m_s.shape, NEG_INIT, f32)
        l_s[...] = jnp.zeros(l_s.shape, f32)
        acc_s[...] = jnp.zeros(acc_s.shape, f32)

    def update(masked):
        vb = v_ref[...]
        if masked:
            qc = (qi * tq + lax.broadcasted_iota(jnp.int32, (tq, 1), 0)) // CHUNK
            kc = (ki * tk + lax.broadcasted_iota(jnp.int32, (1, tk), 1)) // CHUNK
            bias = jnp.where(kc <= qc, 0.0, NEG_MASK)
        for c in range(2):
            s = lax.dot_general(q_ref[:, c * DIFF_HD:(c + 1) * DIFF_HD], k_ref[:, c * DIFF_HD:(c + 1) * DIFF_HD],
                                NT, preferred_element_type=f32) * c2
            if masked:
                s = s + bias
            m_old = m_s[c]
            m_new = jnp.maximum(m_old, jnp.max(s, axis=-1, keepdims=True))
            p = jnp.exp2(s - m_new)
            alpha = jnp.exp2(m_old - m_new)
            l_s[c] = alpha * l_s[c] + jnp.sum(p, axis=-1, keepdims=True)
            acc_s[c] = alpha * acc_s[c] + jnp.dot(p.astype(bf16), vb, preferred_element_type=f32)
            m_s[c] = m_new

    needs_mask = (ki + 1) * tk > qi * tq + CHUNK

    @pl.when(needs_mask)
    def _():
        update(True)

    @pl.when(jnp.logical_not(needs_mask))
    def _():
        update(False)

    @pl.when(ki == last_k)
    def _():
        lam = _diff_lambda(lq1, lk1, lq2, lk2, lam_init)
        o = acc_s[0] / l_s[0] - lam * (acc_s[1] / l_s[1])
        o_ref[...] = _diff_finish(o, gn_ref, lam_init).astype(o_ref.dtype)


def diff_prompt(q, k, v, lams, gn, *, n_seq, T, heads, lam_init):
    hw = 2 * DIFF_HD
    tq = _pick(T, (DIFF_TQ, 512, 256, 128))
    tk = _pick(T, (DIFF_TK, 512, 256, 128))
    nq, nk = T // tq, T // tk
    pairs = [(qi, ki) for qi in range(nq) for ki in range(((qi + 1) * tq - 1) // tk + 1)]
    qt = jnp.asarray(np.array([p[0] for p in pairs], np.int32))
    kt = jnp.asarray(np.array([p[1] for p in pairs], np.int32))
    vec = pl.BlockSpec((1, DIFF_HD), lambda b, h, p, qt, kt: (0, 0))
    kv = pl.BlockSpec((tk, hw), lambda b, h, p, qt, kt: (b * nk + kt[p], h))
    qo = pl.BlockSpec((tq, hw), lambda b, h, p, qt, kt: (b * nq + qt[p], h))
    return pl.pallas_call(
        functools.partial(_diff_prompt_kernel, tq=tq, tk=tk, lam_init=lam_init),
        grid_spec=pltpu.PrefetchScalarGridSpec(
            num_scalar_prefetch=2,
            grid=(n_seq, heads, len(pairs)),
            in_specs=[qo, kv, kv, vec, vec, vec, vec, pl.BlockSpec((1, hw), lambda b, h, p, qt, kt: (0, 0))],
            out_specs=qo,
            scratch_shapes=[pltpu.VMEM((2, tq, 1), f32), pltpu.VMEM((2, tq, 1), f32), pltpu.VMEM((2, tq, hw), f32)],
        ),
        out_shape=jax.ShapeDtypeStruct((n_seq * T, heads * hw), bf16),
        compiler_params=_cparams(("parallel", "parallel", "arbitrary")),
        name="diff_prompt",
    )(qt, kt, q, k, v, *lams, gn)


def _diff_sample_kernel(q_ref, kn_ref, vn_ref, k0_ref, k1_ref, vlo_ref, vhi_ref, lq1, lk1, lq2, lk2, gn_ref, o_ref,
                        m_s, l_s, acc_s, flat_s, *, tk, hpb, lam_init):
    kb = pl.program_id(2)
    T = q_ref.shape[0]
    hw = 2 * DIFF_HD
    c2 = DIFF_HD ** -0.5 * LOG2E

    @pl.when(kb == 0)
    def _():
        m_s[...] = jnp.full(m_s.shape, NEG_INIT, f32)
        l_s[...] = jnp.zeros(l_s.shape, f32)
        acc_s[...] = jnp.zeros(acc_s.shape, f32)

    def update(hl, kblks, vblk):
        ps = []
        for c in range(2):
            idx = 2 * hl + c
            s = lax.dot_general(q_ref[:, idx * DIFF_HD:(idx + 1) * DIFF_HD], kblks[c], NT,
                                preferred_element_type=f32) * c2
            m_old = m_s[idx]
            m_new = jnp.maximum(m_old, jnp.max(s, axis=-1, keepdims=True))
            p = jnp.exp2(s - m_new)
            alpha = jnp.exp2(m_old - m_new)
            l_s[idx] = alpha * l_s[idx] + jnp.sum(p, axis=-1, keepdims=True)
            m_s[idx] = m_new
            acc_s[idx] = alpha * acc_s[idx]
            ps.append(p.astype(bf16))
        pv = jnp.dot(jnp.concatenate(ps, axis=0), vblk, preferred_element_type=f32)
        acc_s[2 * hl] = acc_s[2 * hl] + pv[:T]
        acc_s[2 * hl + 1] = acc_s[2 * hl + 1] + pv[T:]

    for n, ref in enumerate((k0_ref, k1_ref, vlo_ref, vhi_ref)):
        flat_s[n] = ref[...].reshape(tk * SUBLANE, DIFF_HD)

    def rows(n, r):
        return flat_s[n, pl.ds(r, tk, stride=SUBLANE), :]

    for hl in range(hpb):
        quad, j0 = hl // 4, (hl % 4) * 2
        kblks = tuple(rows(quad, j0 + c).astype(bf16) for c in range(2))
        vblk = jnp.concatenate([rows(2, hl), rows(3, hl)], axis=1).astype(bf16)
        update(hl, kblks, vblk)

    @pl.when(kb == pl.num_programs(2) - 1)
    def _():
        lam = _diff_lambda(lq1, lk1, lq2, lk2, lam_init)
        for hl in range(hpb):
            kblks = tuple(kn_ref[:, (2 * hl + c) * DIFF_HD:(2 * hl + c + 1) * DIFF_HD] for c in range(2))
            update(hl, kblks, vn_ref[:, hl * hw:(hl + 1) * hw])
            o = acc_s[2 * hl] / l_s[2 * hl] - lam * (acc_s[2 * hl + 1] / l_s[2 * hl + 1])
            o_ref[:, hl * hw:(hl + 1) * hw] = _diff_finish(o, gn_ref, lam_init).astype(o_ref.dtype)


def diff_sample(q, k, v, k_past, v_past, pb0, lams, gn, *, n_seq, T, heads, lam_init):
    hw = 2 * DIFF_HD
    hpb = SUBLANE
    past = k_past.shape[1]
    assert (past + T - 1) // CHUNK == past // CHUNK and past % CHUNK == 0
    assert heads % hpb == 0
    tk = _pick(past, (DIFF_SAMPLE_TK, 512, 256, 128))
    new = pl.BlockSpec((T, hpb * hw), lambda b, o, kb: (b, o))
    vec = pl.BlockSpec((1, DIFF_HD), lambda b, o, kb: (0, 0))
    tile = (None, tk, SUBLANE, DIFF_HD)
    return pl.pallas_call(
        functools.partial(_diff_sample_kernel, tk=tk, hpb=hpb, lam_init=lam_init),
        grid=(n_seq, heads // hpb, past // tk),
        in_specs=[new, new, new,
                  pl.BlockSpec(tile, lambda b, o, kb: (pb0 + b, kb, 2 * o, 0)),
                  pl.BlockSpec(tile, lambda b, o, kb: (pb0 + b, kb, 2 * o + 1, 0)),
                  pl.BlockSpec(tile, lambda b, o, kb: (pb0 + b, kb, o, 0)),
                  pl.BlockSpec(tile, lambda b, o, kb: (pb0 + b, kb, o, 1)),
                  vec, vec, vec, vec, pl.BlockSpec((1, hw), lambda b, o, kb: (0, 0))],
        out_specs=pl.BlockSpec((T, hpb * hw), lambda b, o, kb: (b, o)),
        out_shape=jax.ShapeDtypeStruct((n_seq * T, heads * hw), bf16),
        scratch_shapes=[pltpu.VMEM((2 * hpb, T, 1), f32), pltpu.VMEM((2 * hpb, T, 1), f32),
                        pltpu.VMEM((2 * hpb, T, hw), f32), pltpu.VMEM((4, tk * SUBLANE, DIFF_HD), f32)],
        compiler_params=_cparams(("parallel", "parallel", "arbitrary")),
        name="diff_sample",
    )(q, k, v, k_past, k_past, v_past, v_past, *lams, gn)


def _ffn_up_kernel(*refs, T, tm, tn, tiled_seq, emit_w):
    it = iter(refs)
    x_ref, wa_ref, wg_ref = next(it), next(it), next(it)
    if tiled_seq:
        prev_a, prev_g = (next(it),), (next(it),)
    else:
        prev_a = (next(it), next(it))
        prev_g = (next(it), next(it))
    cwa_ref, cwg_ref, cba_ref, cbg_ref = next(it), next(it), next(it), next(it)
    o_ref, ta_ref, tg_ref = next(it), next(it), next(it)
    wb_refs = (next(it), next(it)) if emit_w else None
    was_s, wgs_s, carry_s = next(it), next(it), next(it)
    i = pl.program_id(1)

    @pl.when(i == 0)
    def _():
        was_s[...] = wa_ref[...].astype(bf16)
        wgs_s[...] = wg_ref[...].astype(bf16)
        if emit_w:
            wb_refs[0][...] = was_s[...]
            wb_refs[1][...] = wgs_s[...]

    x = x_ref[...]
    row = lax.broadcasted_iota(jnp.int32, (tm, tn), 0)

    def half(idx, ws_s, prevs, cw_ref, cb_ref, tail_ref):
        u = jnp.dot(x, ws_s[...], preferred_element_type=f32)
        if tiled_seq:
            first = (i % (T // tm)) == 0
            c0 = jnp.where(first, prevs[0][0, 0:1, :], carry_s[idx, 0:1, :])
            c1 = jnp.where(first, prevs[0][0, 1:2, :], carry_s[idx, 1:2, :])
            u1 = jnp.where(row == 0, c1, pltpu.roll(u, 1, 0))
            u2 = jnp.where(row == 0, c0, jnp.where(row == 1, c1, pltpu.roll(u, 2, 0)))
            carry_s[idx] = u[tm - 2:tm]
            tail_ref[0] = u[tm - 2:tm]
        else:
            rmod = row % T
            p1 = prevs[1][...]
            u1 = jnp.where(rmod == 0, p1, pltpu.roll(u, 1, 0))
            u2 = jnp.where(rmod == 0, prevs[0][...], jnp.where(rmod == 1, pltpu.roll(p1, 1, 0), pltpu.roll(u, 2, 0)))
            tail_ref[...] = u
        return cb_ref[...] + u2 * cw_ref[0:1, :] + u1 * cw_ref[1:2, :] + u * cw_ref[2:3, :]

    a = half(0, was_s, prev_a, cwa_ref, cba_ref, ta_ref)
    g = half(1, wgs_s, prev_g, cwg_ref, cbg_ref, tg_ref)
    o_ref[...] = (g * (1.0 / (1.0 + jnp.exp(-g))) * a).astype(o_ref.dtype)


def ffn_up(x, w, prev, pb0, cw, cb, l, *, n_seq, T, dff):
    M, K = x.shape
    tn = _pick(dff, (256, 128))
    nb = dff // tn
    tm = _pick(M, (FFN_TM, 512, 256, 128, 64, 32, 16))
    tiled_seq = tm <= T
    assert T % tm == 0 if tiled_seq else tm % T == 0
    emit_w = not isinstance(w, tuple)
    if emit_w:
        w_args = [w, w]
        w_specs = [pl.BlockSpec((None, K, tn), lambda j, i: (l, 0, j)),
                   pl.BlockSpec((None, K, tn), lambda j, i: (l, 0, nb + j))]
    else:
        w_args = list(w)
        w_specs = [pl.BlockSpec((K, tn), lambda j, i: (0, j))] * 2
    if tiled_seq:
        spt = T // tm
        p_args = [prev, prev]
        p_specs = [pl.BlockSpec((1, CONV_W - 1, tn), lambda j, i: (pb0 + i // spt, 0, j)),
                   pl.BlockSpec((1, CONV_W - 1, tn), lambda j, i: (pb0 + i // spt, 0, nb + j))]
        tail_shape = jax.ShapeDtypeStruct((n_seq, CONV_W - 1, dff), f32)
        tail_spec = pl.BlockSpec((1, CONV_W - 1, tn), lambda j, i: (i // spt, 0, j))
    else:
        pv = prev[pb0:pb0 + n_seq]
        ex = [jnp.pad(pv[:, k:k + 1], ((0, 0), (0, T - 1), (0, 0))).reshape(M, 2 * dff) for k in range(CONV_W - 1)]
        p_args = [ex[0], ex[1], ex[0], ex[1]]
        p_specs = [pl.BlockSpec((tm, tn), lambda j, i: (i, j)), pl.BlockSpec((tm, tn), lambda j, i: (i, j)),
                   pl.BlockSpec((tm, tn), lambda j, i: (i, nb + j)), pl.BlockSpec((tm, tn), lambda j, i: (i, nb + j))]
        tail_shape = jax.ShapeDtypeStruct((M, dff), f32)
        tail_spec = pl.BlockSpec((tm, tn), lambda j, i: (i, j))
    c_specs = [pl.BlockSpec((None, CONV_W, tn), lambda j, i: (l, 0, j)),
               pl.BlockSpec((None, CONV_W, tn), lambda j, i: (l, 0, nb + j)),
               pl.BlockSpec((None, 1, tn), lambda j, i: (l, 0, j)),
               pl.BlockSpec((None, 1, tn), lambda j, i: (l, 0, nb + j))]
    cb3 = cb.reshape(cb.shape[0], 1, 2 * dff)
    out_shape = [jax.ShapeDtypeStruct((M, dff), bf16), tail_shape, tail_shape]
    out_specs = [pl.BlockSpec((tm, tn), lambda j, i: (i, j)), tail_spec, tail_spec]
    if emit_w:
        out_shape += [jax.ShapeDtypeStruct((K, dff), bf16)] * 2
        out_specs += [pl.BlockSpec((K, tn), lambda j, i: (0, j))] * 2
    outs = pl.pallas_call(
        functools.partial(_ffn_up_kernel, T=T, tm=tm, tn=tn, tiled_seq=tiled_seq, emit_w=emit_w),
        grid=(nb, M // tm),
        in_specs=[pl.BlockSpec((tm, K), lambda j, i: (i, 0))] + w_specs + p_specs + c_specs,
        out_specs=out_specs,
        out_shape=out_shape,
        scratch_shapes=[pltpu.VMEM((K, tn), bf16), pltpu.VMEM((K, tn), bf16), pltpu.VMEM((2, CONV_W - 1, tn), f32)],
        compiler_params=_cparams(("parallel", "arbitrary")),
        name="ffn_up",
    )(x, *w_args, *p_args, cw, cw, cb3, cb3)
    g, ta, tg = outs[:3]
    if tiled_seq:
        state = jnp.concatenate([ta, tg], axis=-1)
    else:
        state = jnp.concatenate([ta.reshape(n_seq, T, dff)[:, T - (CONV_W - 1):],
                                 tg.reshape(n_seq, T, dff)[:, T - (CONV_W - 1):]], axis=-1)
    return g, state, (tuple(outs[3:]) if emit_w else None)


class _Stream:
    def __init__(self, x, pos0, past):
        self.n_seq, self.T, d = x.shape
        self.h = x.reshape(self.n_seq * self.T, d)
        self.pos0 = pos0
        self.past = past
        pos = jnp.tile(pos0 + jnp.arange(self.T, dtype=jnp.int32), self.n_seq)
        self.tables = rope_tables(pos)


def kernel(x_prompt, x_sample, cache_gla_state, cache_dsa_k, cache_dsa_v, cache_idx_k, cache_diff_k, cache_diff_v, state_ffn_conv, norm_mix, norm_ffn, norm_final, w_in_ab, w_gla_a2, b_gla_a, gla_norm, w_out_ab, w_in_c, lambda_q1, lambda_k1, lambda_q2, lambda_k2, diff_norm, w_out_c, w_up, conv_w, conv_b, w_down):
    B, S, D = x_prompt.shape
    DB, DS, _ = x_sample.shape
    depth = norm_mix.shape[0]
    n_ab, _, gh, gdk, gdv = cache_gla_state.shape
    n_c = cache_diff_k.shape[0]
    past = cache_dsa_k.shape[2]
    kvh = cache_dsa_k.shape[3]
    kvw = kvh * DSA_HD
    dh = w_out_ab.shape[1] - gh * gdv
    dsa_heads = dh // DSA_HD
    idx_heads = (w_in_ab.shape[2] - (2 * gh * gdk + 2 * gh * gdv + GLA_RANK + dh + 2 * kvw + IDX_HD)) // (IDX_HD + 1)
    diff_heads = cache_diff_k.shape[3]
    dff = w_down.shape[1]
    cw_ = diff_heads * 2 * DIFF_HD

    streams = (_Stream(x_prompt, 0, 0), _Stream(x_sample, past, past))
    names = ("gla", "dk", "dv", "ik", "ck", "cv", "conv")
    outs = [{k: [] for k in names} for _ in streams]

    c_gla = cache_gla_state.reshape(n_ab * DB, gh, gdk, gdv)
    c_dk = cache_dsa_k.reshape(n_ab * DB, past * kvh, DSA_HD)
    c_dv = cache_dsa_v.reshape(n_ab * DB, past * kvh, DSA_HD)
    c_ik = cache_idx_k.reshape(n_ab * DB, past, IDX_HD)
    c_ck = cache_diff_k.reshape(n_c * DB, past, 2 * diff_heads, DIFF_HD)
    c_cv = cache_diff_v.reshape(n_c * DB, past, diff_heads, 2 * DIFF_HD)
    c_conv = state_ffn_conv.reshape(depth * DB, CONV_W - 1, 2 * dff)
    zero_state = jnp.zeros((B, gh, gdk, gdv), f32)
    zero_conv = jnp.zeros((B, CONV_W - 1, 2 * dff), f32)
    wd_all = w_down.astype(bf16)

    for l in range(depth):
        i = l // 2
        if l % 2 == 0:
            sizes = (gh * gdk, gh * gdk, gh * gdv, gh * gdv, GLA_RANK, dh, kvw, kvw, idx_heads * IDX_HD, IDX_HD,
                     idx_heads)
            offs = np.concatenate([[0], np.cumsum(sizes)]).tolist()
            tn_ab = kvw
            n_body = offs[10] - GLA_RANK
            n_out = -(-(n_body + LANE) // tn_ab) * tn_ab
            w_perm = prep_w_ab(w_in_ab, i, c_ga=offs[4], c_end_ik=offs[10], n_out=n_out)
            ab_segs = [(offs[4], False, (f32,)), (dh, True, (bf16,)), (kvw, True, (f32, bf16)),
                       (kvw, False, (f32, bf16)), (idx_heads * IDX_HD, True, (bf16,)),
                       (tn_ab, [(IDX_HD, True, (f32, bf16)), (LANE, False, (f32,))])]
            c_gq = 0
            c_gk = c_gq + gh * gdk
            c_gv = c_gk + gh * gdk
            c_gg = c_gv + gh * gdv
            wa2 = w_gla_a2[i].astype(bf16)
            ba = b_gla_a[i].reshape(1, gh * gdk)
            gn = gla_norm[i].reshape(1, gdv)
            w_out = w_out_ab[i].astype(bf16)
            for si, st in enumerate(streams):
                hn = rmsnorm(st.h, norm_mix[l], bf16)
                zg, dq_b, dk_f, dk_b, dv_f, dv_b, iq_b, ik_f, ik_b, misc = proj(hn, w_perm, st.tables, ab_segs, tn_ab,
                                                                                 "in_ab")
                if st.past:
                    s0, sb0 = c_gla, i * DB
                else:
                    s0, sb0 = zero_state, 0
                go, gst = gla(zg, misc, s0, sb0, wa2, ba, gn, n_seq=st.n_seq, T=st.T, heads=gh, dk=gdk, dv=gdv,
                              cols=(c_gq, c_gk, c_gv, c_gg))
                if st.past:
                    tq, kb, caches = st.T, DSA_KB_SAMPLE, (c_dk, c_dv, c_ik, i * DB)
                else:
                    tq, kb, caches = _pick(st.T, (DSA_TQ,)), _pick(st.T, (DSA_KB, 512, 256, 128)), None
                do = dsa(dq_b, iq_b, misc, dk_b, dv_b, ik_b, caches, n_seq=st.n_seq, T=st.T, pos0=st.pos0,
                         heads=dsa_heads, kv_heads=kvh, idx_heads=idx_heads, tq=tq, kb=kb)
                st.h = matmul((go, do), w_out, f32, res=st.h, name="out_ab")
                o = outs[si]
                o["gla"].append(gst)
                o["dk"].append(dk_f.reshape(st.n_seq, st.T, kvh, DSA_HD))
                o["dv"].append(dv_f.reshape(st.n_seq, st.T, kvh, DSA_HD))
                o["ik"].append(ik_f.reshape(st.n_seq, st.T, IDX_HD))
        else:
            lam_init = 0.8 - 0.6 * math.exp(-0.3 * l)
            w_in = w_in_c[i].astype(bf16)
            w_out = w_out_c[i].astype(bf16)
            lams = tuple(a[i].reshape(1, DIFF_HD) for a in (lambda_q1, lambda_k1, lambda_q2, lambda_k2))
            gn = diff_norm[i].reshape(1, 2 * DIFF_HD)
            c_segs = [(cw_, True, (bf16,)), (cw_, True, (f32, bf16)), (cw_, False, (f32, bf16))]
            for si, st in enumerate(streams):
                hn = rmsnorm(st.h, norm_mix[l], bf16)
                q_b, k_f, k_b, v_f, v_b = proj(hn, w_in, st.tables, c_segs, min(PROJ_TN, cw_), "in_c")
                if st.past:
                    at = diff_sample(q_b, k_b, v_b, c_ck, c_cv, i * DB, lams, gn, n_seq=st.n_seq, T=st.T,
                                     heads=diff_heads, lam_init=lam_init)
                else:
                    at = diff_prompt(q_b, k_b, v_b, lams, gn, n_seq=st.n_seq, T=st.T, heads=diff_heads,
                                     lam_init=lam_init)
                st.h = matmul(at, w_out, f32, res=st.h, name="out_c")
                o = outs[si]
                o["ck"].append(k_f.reshape(st.n_seq, st.T, diff_heads, 2, DIFF_HD))
                o["cv"].append(v_f.reshape(st.n_seq, st.T, diff_heads, 2 * DIFF_HD))

        wu = w_up
        for si, st in enumerate(streams):
            hn = rmsnorm(st.h, norm_ffn[l], bf16)
            prev, pb0 = (c_conv, l * DB) if st.past else (zero_conv, 0)
            g, conv_state, wu_b = ffn_up(hn, wu, prev, pb0, conv_w, conv_b, l, n_seq=st.n_seq, T=st.T, dff=dff)
            if wu_b is not None:
                wu = wu_b
            st.h = matmul(g, wd_all, f32, res=st.h, name="ffn_down", layer=l)
            outs[si]["conv"].append(conv_state)

    ys = [rmsnorm(st.h, norm_final, f32).reshape(st.n_seq, st.T, D) for st in streams]
    tail = [jnp.stack(o[k]) for o in outs for k in names]
    return (ys[0], ys[1], *tail)
```

```python
import functools
import math

import numpy as np
import jax
import jax.numpy as jnp
from jax import lax
from jax.experimental import pallas as pl
from jax.experimental.pallas import tpu as pltpu

CHUNK = 64
EPS = 1e-6
ROPE_THETA = 10000.0
GLA_RANK = 16
GLA_TAU = 16.0
DSA_HD = 128
IDX_HD = 128
IDX_TOPK = 256
DIFF_HD = 128
CONV_W = 3

LANE = 128
SUBLANE = 8
PROJ_TN = 512
GLA_SUB = 16
DSA_TQ, DSA_KB = 128, 1024
DSA_KB_SAMPLE = 1024
DIFF_TQ, DIFF_TK = 1024, 1024
DIFF_ROW_SPLIT = 2
FFN_TM = 1024
DIFF_SAMPLE_TK = 512
VMEM_LIMIT = 60 * 1024 * 1024
MM_VMEM_BUDGET = 52 * 1024 * 1024
NEG_INIT = -1e30
NEG_MASK = -3e38
INT_MIN = -2147483648
LOG2E = 1.4426950408889634

f32 = jnp.float32
bf16 = jnp.bfloat16
NT = (((1,), (1,)), ((), ()))
TN = (((0,), (0,)), ((), ()))


def _cparams(sem):
    return pltpu.CompilerParams(dimension_semantics=sem, vmem_limit_bytes=VMEM_LIMIT)


def _pick(dim, cands):
    for c in cands:
        if c <= dim and dim % c == 0:
            return c
    return dim


def _rmsnorm_kernel(x_ref, g_ref, o_ref):
    x = x_ref[...]
    y = x * lax.rsqrt(jnp.mean(x * x, axis=-1, keepdims=True) + EPS)
    o_ref[...] = (y * g_ref[...]).astype(o_ref.dtype)


def rmsnorm(x, g, out_dtype):
    M, D = x.shape
    tm = _pick(M, (256, 128, 64, 32, 16, 8))
    return pl.pallas_call(
        _rmsnorm_kernel,
        grid=(M // tm,),
        in_specs=[pl.BlockSpec((tm, D), lambda i: (i, 0)), pl.BlockSpec((1, D), lambda i: (0, 0))],
        out_specs=pl.BlockSpec((tm, D), lambda i: (i, 0)),
        out_shape=jax.ShapeDtypeStruct((M, D), out_dtype),
        compiler_params=_cparams(("parallel",)),
        name="rmsnorm",
    )(x, g.reshape(1, D).astype(f32))


def _mm_kernel(*refs, n_x, has_res):
    x_refs, w_ref = refs[:n_x], refs[n_x]
    o_ref = refs[-1]
    acc = None
    k0 = 0
    for x_ref in x_refs:
        kw = x_ref.shape[1]
        part = jnp.dot(x_ref[...], w_ref[k0:k0 + kw, :], preferred_element_type=f32)
        acc = part if acc is None else acc + part
        k0 += kw
    if has_res:
        acc = refs[n_x + 1][...] + acc
    o_ref[...] = acc.astype(o_ref.dtype)


def matmul(xs, w, out_dtype, res=None, name="matmul", layer=None):
    xs = tuple(xs) if isinstance(xs, (tuple, list)) else (xs,)
    M = xs[0].shape[0]
    K, N = w.shape[-2:]
    ob = jnp.dtype(out_dtype).itemsize
    best = None
    for tm in (1024, 768, 512, 384, 256, 128, 64, 32, 16, 8):
        if M % tm:
            continue
        for tn in (1024, 768, 512, 384, 256, 128):
            if N % tn:
                continue
            vm = 2 * (tm * K * 2 + K * tn * 2 + tm * tn * ob + (tm * tn * 4 if res is not None else 0))
            if vm > MM_VMEM_BUDGET:
                continue
            score = (tm * tn) / (tm + tn)
            if best is None or score > best[0]:
                best = (score, tm, tn)
    _, tm, tn = best
    in_specs = [pl.BlockSpec((tm, x.shape[1]), lambda i, j: (i, 0)) for x in xs]
    if layer is None:
        in_specs.append(pl.BlockSpec((K, tn), lambda i, j: (0, j)))
    else:
        in_specs.append(pl.BlockSpec((None, K, tn), lambda i, j: (layer, 0, j)))
    args = list(xs) + [w]
    if res is not None:
        in_specs.append(pl.BlockSpec((tm, tn), lambda i, j: (i, j)))
        args.append(res)
    return pl.pallas_call(
        functools.partial(_mm_kernel, n_x=len(xs), has_res=res is not None),
        grid=(M // tm, N // tn),
        in_specs=in_specs,
        out_specs=pl.BlockSpec((tm, tn), lambda i, j: (i, j)),
        out_shape=jax.ShapeDtypeStruct((M, N), out_dtype),
        compiler_params=_cparams(("parallel", "arbitrary")),
        name=name,
    )(*args)


def _proj_kernel(x_ref, w_ref, c_ref, s_ref, *o_refs, segs, tn):
    j = pl.program_id(1)

    @pl.when(j == 0)
    def _():
        for (j0, j1, rotate, out_ids, widths) in segs:
            if j0 > 0:
                for k in jax.tree_util.tree_leaves(out_ids):
                    o_refs[k][...] = jnp.zeros(o_refs[k].shape, o_refs[k].dtype)

    for (j0, j1, rotate, out_ids, widths) in segs:
        @pl.when(jnp.logical_and(j >= j0, j < j1))
        def _(rotate=rotate, out_ids=out_ids, widths=widths):
            acc = jnp.dot(x_ref[...], w_ref[...], preferred_element_type=f32)
            cos, sin = c_ref[...], s_ref[...]
            col = 0
            for part, k_ids in enumerate(out_ids if widths else (out_ids,)):
                wcols = widths[part] if widths else tn
                rot = rotate[part] if widths else rotate
                for g in range(wcols // LANE):
                    y = acc[:, col + g * LANE:col + (g + 1) * LANE]
                    if rot:
                        y = y * cos + pltpu.roll(y, LANE // 2, 1) * sin
                    for k in (k_ids if isinstance(k_ids, tuple) else (k_ids,)):
                        o_refs[k][:, g * LANE:(g + 1) * LANE] = y.astype(o_refs[k].dtype)
                col += wcols


def proj(x, w, tables, segs_cols, tn, name):
    M, K = x.shape
    N = w.shape[1]
    tm = _pick(M, (1024, 768, 512, 256, 128, 64, 32, 16, 8))
    segs, out_shape, out_specs = [], [], []
    j0 = 0
    for sc in segs_cols:
        n_cols = sc[0]
        assert n_cols % tn == 0 or (len(sc) == 2 and n_cols <= tn)
        nj = max(n_cols // tn, 1)
        lo, hi = j0, j0 + nj

        def omap(i, j, lo=lo, hi=hi):
            return (i, jnp.clip(j - lo, 0, hi - lo - 1))

        if len(sc) == 3:
            _, rotate, dtypes = sc
            ids = []
            for dt in dtypes:
                ids.append(len(out_shape))
                out_shape.append(jax.ShapeDtypeStruct((M, n_cols), dt))
                out_specs.append(pl.BlockSpec((tm, tn), omap))
            segs.append((lo, hi, rotate, tuple(ids), None))
        else:
            parts = sc[1]
            ids, widths, rots = [], [], []
            for (width, rotate, dtypes) in parts:
                pid = []
                for dt in dtypes:
                    pid.append(len(out_shape))
                    out_shape.append(jax.ShapeDtypeStruct((M, width), dt))
                    out_specs.append(pl.BlockSpec((tm, width), lambda i, j: (i, 0)))
                ids.append(tuple(pid))
                widths.append(width)
                rots.append(rotate)
            segs.append((lo, hi, tuple(rots), tuple(ids), tuple(widths)))
        j0 = hi
    assert j0 * tn == N, (j0, tn, N)
    tab = pl.BlockSpec((tm, LANE), lambda i, j: (i, 0))
    return pl.pallas_call(
        functools.partial(_proj_kernel, segs=tuple(segs), tn=tn),
        grid=(M // tm, N // tn),
        in_specs=[pl.BlockSpec((tm, K), lambda i, j: (i, 0)), pl.BlockSpec((K, tn), lambda i, j: (0, j)), tab, tab],
        out_specs=out_specs,
        out_shape=out_shape,
        compiler_params=_cparams(("parallel", "arbitrary")),
        name=name,
    )(x, w, *tables)


def _prep_w_ab_kernel(a_ref, b_ref, o_ref, *, j_shift, j_misc, shift):
    j = pl.program_id(0)

    @pl.when(j < j_shift)
    def _():
        o_ref[...] = a_ref[...].T.astype(bf16)

    @pl.when(jnp.logical_and(j >= j_shift, j < j_misc))
    def _():
        o_ref[...] = jnp.concatenate([a_ref[shift:, :], b_ref[...]], axis=0).T.astype(bf16)

    @pl.when(j == j_misc)
    def _():
        pad = jnp.zeros((LANE - 2 * shift, a_ref.shape[1]), f32)
        o_ref[...] = jnp.concatenate([a_ref[:shift, :], b_ref[...], pad], axis=0).T.astype(bf16)

    @pl.when(j > j_misc)
    def _():
        o_ref[...] = jnp.zeros(o_ref.shape, bf16)


def prep_w_ab(w, layer, *, c_ga, c_end_ik, n_out):
    _, K, n_in = w.shape
    shift = GLA_RANK
    assert c_ga % LANE == 0 and (c_end_ik - shift) % LANE == 0 and n_in - c_end_ik == shift
    j_shift = c_ga // LANE
    j_misc = (c_end_ik - shift) // LANE
    last = (n_in - 1) // LANE

    def amap(j):
        return (layer, jnp.where(j == j_misc, j_shift, jnp.minimum(j, last)), 0)

    per = LANE // shift
    assert LANE % shift == 0 and shift % SUBLANE == 0

    def bmap(j):
        return (layer, jnp.where(j == j_misc, j_misc * per + 1, jnp.minimum((j + 1) * per, (n_in - 1) // shift)), 0)

    wt = jnp.swapaxes(w, 1, 2)
    return pl.pallas_call(
        functools.partial(_prep_w_ab_kernel, j_shift=j_shift, j_misc=j_misc, shift=shift),
        grid=(n_out // LANE,),
        in_specs=[pl.BlockSpec((None, LANE, K), amap), pl.BlockSpec((None, shift, K), bmap)],
        out_specs=pl.BlockSpec((K, LANE), lambda j: (0, j)),
        out_shape=jax.ShapeDtypeStruct((K, n_out), bf16),
        compiler_params=_cparams(("parallel",)),
        name="prep_w_ab",
    )(wt, wt)


def rope_tables(pos):
    half = LANE // 2
    inv = 1.0 / (ROPE_THETA ** (jnp.arange(half, dtype=f32) * (2.0 / LANE)))
    ang = pos.astype(f32)[:, None] * inv[None, :]
    cos, sin = jnp.cos(ang), jnp.sin(ang)
    return jnp.concatenate([cos, cos], axis=-1), jnp.concatenate([-sin, sin], axis=-1)


def _gla_kernel(q_ref, k_ref, v_ref, gg_ref, misc_ref, wa2_ref, ba_ref, gn_ref, s0_ref,
                o_ref, sout_ref, st_ref, *, c, n_inner, dk, dv):
    step = pl.program_id(2)

    @pl.when(step == 0)
    def _():
        st_ref[...] = s0_ref[0, 0].T

    nsub = c // GLA_SUB
    row_c = lax.broadcasted_iota(jnp.int32, (c, dk), 0)
    row_a = lax.broadcasted_iota(jnp.int32, (c, c), 0)
    lane_a = lax.broadcasted_iota(jnp.int32, (c, c), 1)
    sub_a = row_a % GLA_SUB

    def chunk(ci, carry):
        r = pl.multiple_of(ci * c, c)
        q = q_ref[pl.ds(r, c), :] * (dk ** -0.5)
        k = k_ref[pl.ds(r, c), :]
        v = v_ref[pl.ds(r, c), :]
        ga = misc_ref[pl.ds(r, c), :][:, :GLA_RANK]
        x = jnp.dot(ga.astype(bf16), wa2_ref[...], preferred_element_type=f32) + ba_ref[...]
        loga = (jnp.minimum(x, 0.0) - jnp.log1p(jnp.exp(-jnp.abs(x)))) / GLA_TAU
        b = loga
        d = 1
        while d < c:
            b = b + jnp.where(row_c >= d, pltpu.roll(b, d, 0), 0.0)
            d *= 2
        st = st_ref[...]
        vb = v.astype(bf16)
        o_inter = lax.dot_general((q * jnp.exp(b)).astype(bf16), st.astype(bf16), NT,
                                  preferred_element_type=f32)
        att = jnp.zeros((c, c), f32)
        for delta in range(GLA_SUB):
            if delta == 0:
                w = q * k
            else:
                w = q * pltpu.roll(k, delta, 0) * jnp.exp(jnp.minimum(b - pltpu.roll(b, delta, 0), 0.0))
            col = jnp.sum(w, axis=-1, keepdims=True)
            att = jnp.where(jnp.logical_and(lane_a == row_a - delta, sub_a >= delta), col, att)
        if nsub > 1:
            offs = [jnp.zeros((GLA_SUB, c), f32)]
            for i in range(1, nsub):
                r0 = i * GLA_SUB
                bref = b[r0 - 1:r0]
                qq = q[r0:r0 + GLA_SUB] * jnp.exp(b[r0:r0 + GLA_SUB] - bref)
                kk = jnp.where(row_c < r0, k * jnp.exp(jnp.minimum(bref - b, 0.0)), 0.0)
                offs.append(lax.dot_general(qq.astype(bf16), kk.astype(bf16), NT, preferred_element_type=f32))
            att = att + jnp.concatenate(offs, axis=0)
        o = o_inter + jnp.dot(att.astype(bf16), vb, preferred_element_type=f32)
        b_last = b[c - 1:c]
        kd = k * jnp.exp(b_last - b)
        st_ref[...] = jnp.exp(b_last) * st + lax.dot_general(vb, kd.astype(bf16), TN, preferred_element_type=f32)
        y = o * lax.rsqrt(jnp.mean(o * o, axis=-1, keepdims=True) + EPS) * gn_ref[...]
        gg = gg_ref[pl.ds(r, c), :]
        y = y * (gg * (1.0 / (1.0 + jnp.exp(-gg))))
        o_ref[pl.ds(r, c), :] = y.astype(o_ref.dtype)
        return carry

    lax.fori_loop(0, n_inner, chunk, 0)

    @pl.when(step == pl.num_programs(2) - 1)
    def _():
        sout_ref[0, 0] = st_ref[...].T


def gla(z, misc, s0, sb0, wa2, ba, gn, *, n_seq, T, heads, dk, dv, cols):
    c = min(CHUNK, T)
    tb = max(_pick(T, (256, 128, 64, 32, 16)), c)
    n_steps = T // tb
    cq, ck, cv, cg = cols
    in_specs = [
        pl.BlockSpec((tb, dk), lambda b, h, s: (b * n_steps + s, cq // dk + h)),
        pl.BlockSpec((tb, dk), lambda b, h, s: (b * n_steps + s, ck // dk + h)),
        pl.BlockSpec((tb, dv), lambda b, h, s: (b * n_steps + s, cv // dv + h)),
        pl.BlockSpec((tb, dv), lambda b, h, s: (b * n_steps + s, cg // dv + h)),
        pl.BlockSpec((tb, LANE), lambda b, h, s: (b * n_steps + s, 0)),
        pl.BlockSpec((GLA_RANK, dk), lambda b, h, s: (0, h)),
        pl.BlockSpec((1, dk), lambda b, h, s: (0, h)),
        pl.BlockSpec((1, dv), lambda b, h, s: (0, 0)),
        pl.BlockSpec((1, 1, dk, dv), lambda b, h, s: (sb0 + b, h, 0, 0)),
    ]
    out_specs = [
        pl.BlockSpec((tb, dv), lambda b, h, s: (b * n_steps + s, h)),
        pl.BlockSpec((1, 1, dk, dv), lambda b, h, s: (b, h, 0, 0)),
    ]
    return pl.pallas_call(
        functools.partial(_gla_kernel, c=c, n_inner=tb // c, dk=dk, dv=dv),
        grid=(n_seq, heads, n_steps),
        in_specs=in_specs,
        out_specs=out_specs,
        out_shape=[jax.ShapeDtypeStruct((n_seq * T, heads * dv), bf16),
                   jax.ShapeDtypeStruct((n_seq, heads, dk, dv), f32)],
        scratch_shapes=[pltpu.VMEM((dv, dk), f32)],
        compiler_params=_cparams(("parallel", "parallel", "arbitrary")),
        name="gla",
    )(z, z, z, z, misc, wa2, ba, gn, s0)


def _dsa_kernel(*refs, tq, kb, s_valid, pos0, past, topk, heads, kv_heads, idx_heads):
    if past:
        (dq_ref, iq_ref, misc_ref, k_ref, v_ref, ik_ref, kc_ref, vc_ref, ikc_ref, o_ref,
         key_s, q_s, m_s, acc_s, ikt_s, kt_s, vt_s) = refs
    else:
        dq_ref, iq_ref, misc_ref, k_ref, v_ref, ik_ref, o_ref, key_s, q_s, m_s, acc_s = refs
    j = pl.program_id(1)
    qpos0 = pos0 + j * tq
    qchunk = (qpos0 + lax.broadcasted_iota(jnp.int32, (tq, 1), 0)) // CHUNK
    w = misc_ref[...][:, GLA_RANK:GLA_RANK + idx_heads]
    group = heads // kv_heads
    T = dq_ref.shape[0]

    if past:
        n_cache = past // kb
        nkb = n_cache + 1
        ikt_s[...] = jnp.zeros(ikt_s.shape, bf16)
        ikt_s[0:T, :] = ik_ref[...]
        kt_s[...] = jnp.zeros(kt_s.shape, bf16)
        vt_s[...] = jnp.zeros(vt_s.shape, bf16)
        for g in range(kv_heads):
            kt_s[g, 0:T, :] = k_ref[:, g * DSA_HD:(g + 1) * DSA_HD]
            vt_s[g, 0:T, :] = v_ref[:, g * DSA_HD:(g + 1) * DSA_HD]

        def over_blocks(cache_fn, tail_fn):
            lax.fori_loop(0, n_cache, lambda b, c: (cache_fn(b), c)[1], 0)
            tail_fn()

        def cached(ref, b, g):
            return ref[pl.ds(b * (kb * kv_heads) + g, kb, stride=kv_heads), :].astype(bf16)
    else:
        vis_end = jnp.minimum(((qpos0 + tq - 1) // CHUNK + 1) * CHUNK, s_valid)
        nkb = (vis_end + kb - 1) // kb

    def score_block(c0, ikb):
        acc = jnp.zeros((tq, kb), f32)
        for h in range(idx_heads):
            s = lax.dot_general(iq_ref[:, h * IDX_HD:(h + 1) * IDX_HD], ikb, NT, preferred_element_type=f32)
            acc = acc + w[:, h:h + 1] * jnp.maximum(s, 0.0)
        kpos = c0 + lax.broadcasted_iota(jnp.int32, (1, kb), 1)
        vis = jnp.logical_and(kpos // CHUNK <= qchunk, kpos < s_valid)
        bits = pltpu.bitcast(acc, jnp.int32)
        key = bits ^ ((bits >> 31) & 0x7FFFFFFF)
        key_s[:, pl.ds(c0, kb)] = jnp.where(vis, key, INT_MIN)

    if past:
        over_blocks(lambda b: score_block(pl.multiple_of(b * kb, kb),
                                          ikc_ref[pl.ds(pl.multiple_of(b * kb, kb), kb), :].astype(bf16)),
                    lambda: score_block(n_cache * kb, ikt_s[...]))
    else:
        def score_body(b, carry):
            c0 = pl.multiple_of(b * kb, kb)
            score_block(c0, ik_ref[0, pl.ds(c0, kb), :])
            return carry

        lax.fori_loop(0, nkb, score_body, 0)

    one, zero = jnp.int32(1), jnp.int32(0)
    izeros = jnp.zeros((tq, LANE), jnp.int32)
    lane_i = lax.broadcasted_iota(jnp.int32, (1, LANE), 1)

    def count(pred):
        def body(b, cacc):
            c0 = pl.multiple_of(b * kb, kb)
            blk = key_s[:, pl.ds(c0, kb)]
            for t in range(kb // LANE):
                cacc = cacc + jnp.where(pred(blk[:, t * LANE:(t + 1) * LANE], c0 + t * LANE + lane_i), one, zero)
            return cacc
        return jnp.sum(lax.fori_loop(0, nkb, body, izeros), axis=-1, keepdims=True)

    def bit_body(i, carry):
        t_u, c_at = carry
        cand_u = t_u | lax.shift_left(one, 31 - i)
        cand = cand_u ^ INT_MIN
        cnt = count(lambda kt, it: kt >= cand)
        ok = cnt >= topk
        return jnp.where(ok, cand_u, t_u), jnp.where(ok, cnt, c_at)

    t_u, c_at = lax.fori_loop(0, 32, bit_body, (izeros, izeros))
    thr = jnp.maximum(t_u ^ INT_MIN, INT_MIN + 1)

    @pl.when(jnp.max(c_at) > topk)
    def _():
        need = topk - count(lambda kt, it: kt > thr)
        nbits = int(s_valid).bit_length()

        def idx_body(i, lo):
            cand = lo | lax.shift_left(one, nbits - 1 - i)
            c = count(lambda kt, it: jnp.logical_and(kt == thr, it < cand))
            return jnp.where(c < need, cand, lo)

        lo = lax.fori_loop(0, nbits, idx_body, izeros)

        def drop_body(b, carry):
            c0 = pl.multiple_of(b * kb, kb)
            blk = key_s[:, pl.ds(c0, kb)]
            tiles = []
            for t in range(kb // LANE):
                kt = blk[:, t * LANE:(t + 1) * LANE]
                drop = jnp.logical_and(kt == thr, c0 + t * LANE + lane_i > lo)
                tiles.append(jnp.where(drop, INT_MIN, kt))
            key_s[:, pl.ds(c0, kb)] = jnp.concatenate(tiles, axis=1)
            return carry

        lax.fori_loop(0, nkb, drop_body, 0)

    c2 = DSA_HD ** -0.5 * LOG2E
    for g in range(kv_heads):
        for i in range(group):
            hh = g * group + i
            q_s[g, i * tq:(i + 1) * tq, :] = dq_ref[:, hh * DSA_HD:(hh + 1) * DSA_HD]
    m_s[...] = jnp.full(m_s.shape, NEG_INIT, f32)
    acc_s[...] = jnp.zeros(acc_s.shape, f32)

    def attend_block(c0, kv_fn):
        keys = key_s[:, pl.ds(c0, kb)]
        bias = jnp.concatenate(
            [jnp.where(keys[:, t * LANE:(t + 1) * LANE] >= thr, 0.0, NEG_MASK) for t in range(kb // LANE)],
            axis=1)
        bias = jnp.concatenate([bias] * group, axis=0)
        ones = jnp.ones((kb, DSA_HD), bf16)
        for g in range(kv_heads):
            kblk, vblk = kv_fn(g)
            s = lax.dot_general(q_s[g], kblk, NT, preferred_element_type=f32) * c2 + bias
            m_old = m_s[g]
            m_new = jnp.maximum(m_old, jnp.max(s, axis=-1, keepdims=True))
            p = jnp.exp2((s - m_new).astype(bf16))
            alpha = jnp.exp2(m_old - m_new)
            acc_s[g] = alpha * acc_s[g] + jnp.dot(p, jnp.concatenate([vblk, ones], axis=1),
                                                  preferred_element_type=f32)
            m_s[g] = m_new

    if past:
        over_blocks(lambda b: attend_block(pl.multiple_of(b * kb, kb),
                                           lambda g: (cached(kc_ref, b, g), cached(vc_ref, b, g))),
                    lambda: attend_block(n_cache * kb, lambda g: (kt_s[g], vt_s[g])))
    else:
        def kv_body(b, carry):
            c0 = pl.multiple_of(b * kb, kb)
            attend_block(c0, lambda g: (k_ref[0, pl.ds(c0, kb), g * DSA_HD:(g + 1) * DSA_HD],
                                        v_ref[0, pl.ds(c0, kb), g * DSA_HD:(g + 1) * DSA_HD]))
            return carry

        lax.fori_loop(0, nkb, kv_body, 0)
    for g in range(kv_heads):
        out = acc_s[g][:, :DSA_HD] / acc_s[g][:, DSA_HD:DSA_HD + 1]
        for i in range(group):
            hh = g * group + i
            o_ref[:, hh * DSA_HD:(hh + 1) * DSA_HD] = out[i * tq:(i + 1) * tq].astype(o_ref.dtype)


def dsa(dq, iq, misc, k_new, v_new, ik_new, caches, *, n_seq, T, pos0, heads, kv_heads, idx_heads, tq, kb):
    nq = T // tq
    group = heads // kv_heads
    kvw = kv_heads * DSA_HD
    past = 0 if caches is None else caches[2].shape[1]
    s_valid = past + T
    topk = min(IDX_TOPK, s_valid // 4)
    in_specs = [
        pl.BlockSpec((tq, heads * DSA_HD), lambda b, j: (b * nq + j, 0)),
        pl.BlockSpec((tq, idx_heads * IDX_HD), lambda b, j: (b * nq + j, 0)),
        pl.BlockSpec((tq, LANE), lambda b, j: (b * nq + j, 0)),
    ]
    scratch = [pltpu.VMEM((kv_heads, group * tq, DSA_HD), bf16), pltpu.VMEM((kv_heads, group * tq, 1), f32),
               pltpu.VMEM((kv_heads, group * tq, 2 * DSA_HD), f32)]
    if past:
        assert nq == 1 and past % kb == 0 and T <= kb
        kc, vc, ikc, pb0 = caches
        s_pad = past + kb
        in_specs += [pl.BlockSpec((T, kvw), lambda b, j: (b, 0)), pl.BlockSpec((T, kvw), lambda b, j: (b, 0)),
                     pl.BlockSpec((T, IDX_HD), lambda b, j: (b, 0)),
                     pl.BlockSpec((None, past * kv_heads, DSA_HD), lambda b, j: (pb0 + b, 0, 0)),
                     pl.BlockSpec((None, past * kv_heads, DSA_HD), lambda b, j: (pb0 + b, 0, 0)),
                     pl.BlockSpec((None, past, IDX_HD), lambda b, j: (pb0 + b, 0, 0))]
        args = (dq, iq, misc, k_new, v_new, ik_new, kc, vc, ikc)
        scratch += [pltpu.VMEM((kb, IDX_HD), bf16), pltpu.VMEM((kv_heads, kb, DSA_HD), bf16),
                    pltpu.VMEM((kv_heads, kb, DSA_HD), bf16)]
    else:
        s_pad = T
        in_specs += [pl.BlockSpec((1, T, kvw), lambda b, j: (b, 0, 0)), pl.BlockSpec((1, T, kvw), lambda b, j: (b, 0, 0)),
                     pl.BlockSpec((1, T, IDX_HD), lambda b, j: (b, 0, 0))]
        args = (dq, iq, misc, k_new.reshape(n_seq, T, kvw), v_new.reshape(n_seq, T, kvw),
                ik_new.reshape(n_seq, T, IDX_HD))
    kern = functools.partial(_dsa_kernel, tq=tq, kb=kb, s_valid=s_valid, pos0=pos0, past=past, topk=topk,
                             heads=heads, kv_heads=kv_heads, idx_heads=idx_heads)
    return pl.pallas_call(
        kern,
        grid=(n_seq, nq),
        in_specs=in_specs,
        out_specs=pl.BlockSpec((tq, heads * DSA_HD), lambda b, j: (b * nq + j, 0)),
        out_shape=jax.ShapeDtypeStruct((n_seq * T, heads * DSA_HD), bf16),
        scratch_shapes=[pltpu.VMEM((tq, s_pad), jnp.int32)] + scratch,
        compiler_params=_cparams(("parallel", "arbitrary")),
        name="dsa",
    )(*args)


def _diff_lambda(lq1, lk1, lq2, lk2, lam_init):
    return (jnp.exp(jnp.sum(lq1[...] * lk1[...], axis=-1, keepdims=True))
            - jnp.exp(jnp.sum(lq2[...] * lk2[...], axis=-1, keepdims=True)) + lam_init)


def _diff_finish(o, gn_ref, lam_init):
    y = o * lax.rsqrt(jnp.mean(o * o, axis=-1, keepdims=True) + EPS) * gn_ref[...]
    return y * (1.0 - lam_init)


def _diff_prompt_kernel(qt_ref, kt_ref, q_ref, k_ref, v_ref, lq1, lk1, lq2, lk2, gn_ref, o_ref, m_s, l_s, acc_s,
                        *, tq, tk, lam_init):
    p_id = pl.program_id(2)
    qi = qt_ref[p_id]
    ki = kt_ref[p_id]
    last_k = ((qi + 1) * tq - 1) // tk
    c2 = DIFF_HD ** -0.5 * LOG2E

    @pl.when(ki == 0)
    def _():
        m_s[...] = jnp.full(m_s.shape, NEG_INIT, f32)
        l_s[...] = jnp.zeros(l_s.shape, f32)
        acc_s[...] = jnp.zeros(acc_s.shape, f32)

    def update(masked):
        vb = v_ref[...]
        if masked:
            qc = (qi * tq + lax.broadcasted_iota(jnp.int32, (tq, 1), 0)) // CHUNK
            kc = (ki * tk + lax.broadcasted_iota(jnp.int32, (1, tk), 1)) // CHUNK
            bias = jnp.where(kc <= qc, 0.0, NEG_MASK)
        hq = tq // DIFF_ROW_SPLIT
        for c in range(2):
            kc_ = k_ref[:, c * DIFF_HD:(c + 1) * DIFF_HD]
            for r0 in range(0, tq, hq):
                s = lax.dot_general(q_ref[r0:r0 + hq, c * DIFF_HD:(c + 1) * DIFF_HD], kc_,
                                    NT, preferred_element_type=f32) * c2
                if masked:
                    s = s + bias[r0:r0 + hq]
                m_old = m_s[c, r0:r0 + hq]
                m_new = jnp.maximum(m_old, jnp.max(s, axis=-1, keepdims=True))
                p = jnp.exp2(s - m_new)
                alpha = jnp.exp2(m_old - m_new)
                l_s[c, r0:r0 + hq] = alpha * l_s[c, r0:r0 + hq] + jnp.sum(p, axis=-1, keepdims=True)
                acc_s[c, r0:r0 + hq] = alpha * acc_s[c, r0:r0 + hq] + jnp.dot(p.astype(bf16), vb,
                                                                               preferred_element_type=f32)
                m_s[c, r0:r0 + hq] = m_new

    needs_mask = (ki + 1) * tk > qi * tq + CHUNK

    @pl.when(needs_mask)
    def _():
        update(True)

    @pl.when(jnp.logical_not(needs_mask))
    def _():
        update(False)

    @pl.when(ki == last_k)
    def _():
        lam = _diff_lambda(lq1, lk1, lq2, lk2, lam_init)
        o = acc_s[0] / l_s[0] - lam * (acc_s[1] / l_s[1])
        o_ref[...] = _diff_finish(o, gn_ref, lam_init).astype(o_ref.dtype)


def diff_prompt(q, k, v, lams, gn, *, n_seq, T, heads, lam_init):
    hw = 2 * DIFF_HD
    tq = _pick(T, (DIFF_TQ, 512, 256, 128))
    tk = _pick(T, (DIFF_TK, 512, 256, 128))
    nq, nk = T // tq, T // tk
    pairs = [(qi, ki) for qi in range(nq) for ki in range(((qi + 1) * tq - 1) // tk + 1)]
    qt = jnp.asarray(np.array([p[0] for p in pairs], np.int32))
    kt = jnp.asarray(np.array([p[1] for p in pairs], np.int32))
    vec = pl.BlockSpec((1, DIFF_HD), lambda b, h, p, qt, kt: (0, 0))
    kv = pl.BlockSpec((tk, hw), lambda b, h, p, qt, kt: (b * nk + kt[p], h))
    qo = pl.BlockSpec((tq, hw), lambda b, h, p, qt, kt: (b * nq + qt[p], h))
    return pl.pallas_call(
        functools.partial(_diff_prompt_kernel, tq=tq, tk=tk, lam_init=lam_init),
        grid_spec=pltpu.PrefetchScalarGridSpec(
            num_scalar_prefetch=2,
            grid=(n_seq, heads, len(pairs)),
            in_specs=[qo, kv, kv, vec, vec, vec, vec, pl.BlockSpec((1, hw), lambda b, h, p, qt, kt: (0, 0))],
            out_specs=qo,
            scratch_shapes=[pltpu.VMEM((2, tq, 1), f32), pltpu.VMEM((2, tq, 1), f32), pltpu.VMEM((2, tq, hw), f32)],
        ),
        out_shape=jax.ShapeDtypeStruct((n_seq * T, heads * hw), bf16),
        compiler_params=_cparams(("parallel", "parallel", "arbitrary")),
        name="diff_prompt",
    )(qt, kt, q, k, v, *lams, gn)


def _diff_sample_kernel(q_ref, kn_ref, vn_ref, k0_ref, k1_ref, vlo_ref, vhi_ref, lq1, lk1, lq2, lk2, gn_ref, o_ref,
                        m_s, l_s, acc_s, flat_s, *, tk, hpb, lam_init):
    kb = pl.program_id(2)
    T = q_ref.shape[0]
    hw = 2 * DIFF_HD
    c2 = DIFF_HD ** -0.5 * LOG2E

    @pl.when(kb == 0)
    def _():
        m_s[...] = jnp.full(m_s.shape, NEG_INIT, f32)
        l_s[...] = jnp.zeros(l_s.shape, f32)
        acc_s[...] = jnp.zeros(acc_s.shape, f32)

    def update(hl, kblks, vblk):
        ps = []
        for c in range(2):
            idx = 2 * hl + c
            s = lax.dot_general(q_ref[:, idx * DIFF_HD:(idx + 1) * DIFF_HD], kblks[c], NT,
                                preferred_element_type=f32) * c2
            m_old = m_s[idx]
            m_new = jnp.maximum(m_old, jnp.max(s, axis=-1, keepdims=True))
            p = jnp.exp2(s - m_new)
            alpha = jnp.exp2(m_old - m_new)
            l_s[idx] = alpha * l_s[idx] + jnp.sum(p, axis=-1, keepdims=True)
            m_s[idx] = m_new
            acc_s[idx] = alpha * acc_s[idx]
            ps.append(p.astype(bf16))
        pv = jnp.dot(jnp.concatenate(ps, axis=0), vblk, preferred_element_type=f32)
        acc_s[2 * hl] = acc_s[2 * hl] + pv[:T]
        acc_s[2 * hl + 1] = acc_s[2 * hl + 1] + pv[T:]

    for n, ref in enumerate((k0_ref, k1_ref, vlo_ref, vhi_ref)):
        flat_s[n] = ref[...].reshape(tk * SUBLANE, DIFF_HD)

    def rows(n, r):
        return flat_s[n, pl.ds(r, tk, stride=SUBLANE), :]

    for hl in range(hpb):
        quad, j0 = hl // 4, (hl % 4) * 2
        kblks = tuple(rows(quad, j0 + c).astype(bf16) for c in range(2))
        vblk = jnp.concatenate([rows(2, hl), rows(3, hl)], axis=1).astype(bf16)
        update(hl, kblks, vblk)

    @pl.when(kb == pl.num_programs(2) - 1)
    def _():
        lam = _diff_lambda(lq1, lk1, lq2, lk2, lam_init)
        for hl in range(hpb):
            kblks = tuple(kn_ref[:, (2 * hl + c) * DIFF_HD:(2 * hl + c + 1) * DIFF_HD] for c in range(2))
            update(hl, kblks, vn_ref[:, hl * hw:(hl + 1) * hw])
            o = acc_s[2 * hl] / l_s[2 * hl] - lam * (acc_s[2 * hl + 1] / l_s[2 * hl + 1])
            o_ref[:, hl * hw:(hl + 1) * hw] = _diff_finish(o, gn_ref, lam_init).astype(o_ref.dtype)


def diff_sample(q, k, v, k_past, v_past, pb0, lams, gn, *, n_seq, T, heads, lam_init):
    hw = 2 * DIFF_HD
    hpb = SUBLANE
    past = k_past.shape[1]
    assert (past + T - 1) // CHUNK == past // CHUNK and past % CHUNK == 0
    assert heads % hpb == 0
    tk = _pick(past, (DIFF_SAMPLE_TK, 512, 256, 128))
    new = pl.BlockSpec((T, hpb * hw), lambda b, o, kb: (b, o))
    vec = pl.BlockSpec((1, DIFF_HD), lambda b, o, kb: (0, 0))
    tile = (None, tk, SUBLANE, DIFF_HD)
    return pl.pallas_call(
        functools.partial(_diff_sample_kernel, tk=tk, hpb=hpb, lam_init=lam_init),
        grid=(n_seq, heads // hpb, past // tk),
        in_specs=[new, new, new,
                  pl.BlockSpec(tile, lambda b, o, kb: (pb0 + b, kb, 2 * o, 0)),
                  pl.BlockSpec(tile, lambda b, o, kb: (pb0 + b, kb, 2 * o + 1, 0)),
                  pl.BlockSpec(tile, lambda b, o, kb: (pb0 + b, kb, o, 0)),
                  pl.BlockSpec(tile, lambda b, o, kb: (pb0 + b, kb, o, 1)),
                  vec, vec, vec, vec, pl.BlockSpec((1, hw), lambda b, o, kb: (0, 0))],
        out_specs=pl.BlockSpec((T, hpb * hw), lambda b, o, kb: (b, o)),
        out_shape=jax.ShapeDtypeStruct((n_seq * T, heads * hw), bf16),
        scratch_shapes=[pltpu.VMEM((2 * hpb, T, 1), f32), pltpu.VMEM((2 * hpb, T, 1), f32),
                        pltpu.VMEM((2 * hpb, T, hw), f32), pltpu.VMEM((4, tk * SUBLANE, DIFF_HD), f32)],
        compiler_params=_cparams(("parallel", "parallel", "arbitrary")),
        name="diff_sample",
    )(q, k, v, k_past, k_past, v_past, v_past, *lams, gn)


def _ffn_up_kernel(xp_ref, xs_ref, wa_ref, wg_ref, pa_ref, pg_ref, e0a_ref, e1a_ref, e0g_ref, e1g_ref,
                   cwa_ref, cwg_ref, cba_ref, cbg_ref, op_ref, tpa_ref, tpg_ref, os_ref, usa_ref, usg_ref,
                   was_s, wgs_s, carry_s, *, T, Ts, tm, tn):
    i = pl.program_id(1)
    sides = ((was_s, pa_ref, (e0a_ref, e1a_ref), cwa_ref, cba_ref, tpa_ref, usa_ref),
             (wgs_s, pg_ref, (e0g_ref, e1g_ref), cwg_ref, cbg_ref, tpg_ref, usg_ref))

    @pl.when(i == 0)
    def _():
        was_s[...] = wa_ref[...].astype(bf16)
        wgs_s[...] = wg_ref[...].astype(bf16)

    def gate(a, g):
        return g * (1.0 / (1.0 + jnp.exp(-g))) * a

    @pl.when(i > 0)
    def _():
        x = xp_ref[...]
        row = lax.broadcasted_iota(jnp.int32, (tm, tn), 0)
        first = ((i - 1) % (T // tm)) == 0
        cs = []
        for idx, (ws_s, p_ref, _, cw_ref, cb_ref, tail_ref, _) in enumerate(sides):
            u = jnp.dot(x, ws_s[...], preferred_element_type=f32)
            c0 = jnp.where(first, p_ref[0, 0:1, :], carry_s[idx, 0:1, :])
            c1 = jnp.where(first, p_ref[0, 1:2, :], carry_s[idx, 1:2, :])
            u1 = jnp.where(row == 0, c1, pltpu.roll(u, 1, 0))
            u2 = jnp.where(row == 0, c0, jnp.where(row == 1, c1, pltpu.roll(u, 2, 0)))
            carry_s[idx] = u[tm - 2:tm]
            tail_ref[0] = u[tm - 2:tm]
            cs.append(cb_ref[...] + u2 * cw_ref[0:1, :] + u1 * cw_ref[1:2, :] + u * cw_ref[2:3, :])
        op_ref[...] = gate(*cs).astype(op_ref.dtype)

    @pl.when(i == 0)
    def _():
        x = xs_ref[...]
        rmod = lax.broadcasted_iota(jnp.int32, (xs_ref.shape[0], tn), 0) % Ts
        cs = []
        for idx, (ws_s, _, (e0_ref, e1_ref), cw_ref, cb_ref, _, us_ref) in enumerate(sides):
            u = jnp.dot(x, ws_s[...], preferred_element_type=f32)
            p1 = e1_ref[...]
            u1 = jnp.where(rmod == 0, p1, pltpu.roll(u, 1, 0))
            u2 = jnp.where(rmod == 0, e0_ref[...], jnp.where(rmod == 1, pltpu.roll(p1, 1, 0), pltpu.roll(u, 2, 0)))
            us_ref[...] = u
            cs.append(cb_ref[...] + u2 * cw_ref[0:1, :] + u1 * cw_ref[1:2, :] + u * cw_ref[2:3, :])
        os_ref[...] = gate(*cs).astype(os_ref.dtype)


def ffn_up(xp, xs, w, prev_p, prev_s, cw, cb, l, *, n_p, T, n_s, Ts, dff):
    Mp, K = xp.shape
    Ms = xs.shape[0]
    tn = _pick(dff, (256, 128))
    nb = dff // tn
    tm = _pick(Mp, (FFN_TM, 512, 256, 128, 64, 32, 16))
    assert tm <= T and T % tm == 0 and Ms == n_s * Ts and Ms % SUBLANE == 0
    n_i = Mp // tm
    spt = T // tm
    ex = [jnp.pad(prev_s[:, k:k + 1], ((0, 0), (0, Ts - 1), (0, 0))).reshape(Ms, 2 * dff) for k in range(CONV_W - 1)]

    def it(i):
        return jnp.maximum(i - 1, 0)

    half = lambda off: (lambda j, i: (0, off + j))
    in_specs = [pl.BlockSpec((tm, K), lambda j, i: (it(i), 0)),
                pl.BlockSpec((Ms, K), lambda j, i: (0, 0)),
                pl.BlockSpec((None, K, tn), lambda j, i: (l, 0, j)),
                pl.BlockSpec((None, K, tn), lambda j, i: (l, 0, nb + j)),
                pl.BlockSpec((1, CONV_W - 1, tn), lambda j, i: (it(i) // spt, 0, j)),
                pl.BlockSpec((1, CONV_W - 1, tn), lambda j, i: (it(i) // spt, 0, nb + j)),
                pl.BlockSpec((Ms, tn), half(0)), pl.BlockSpec((Ms, tn), half(0)),
                pl.BlockSpec((Ms, tn), half(nb)), pl.BlockSpec((Ms, tn), half(nb)),
                pl.BlockSpec((None, CONV_W, tn), lambda j, i: (l, 0, j)),
                pl.BlockSpec((None, CONV_W, tn), lambda j, i: (l, 0, nb + j)),
                pl.BlockSpec((None, 1, tn), lambda j, i: (l, 0, j)),
                pl.BlockSpec((None, 1, tn), lambda j, i: (l, 0, nb + j))]
    tail_p = pl.BlockSpec((1, CONV_W - 1, tn), lambda j, i: (it(i) // spt, 0, j))
    blk_s = pl.BlockSpec((Ms, tn), half(0))
    out_specs = [pl.BlockSpec((tm, tn), lambda j, i: (it(i), j)), tail_p, tail_p, blk_s, blk_s, blk_s]
    out_shape = [jax.ShapeDtypeStruct((Mp, dff), bf16),
                 jax.ShapeDtypeStruct((n_p, CONV_W - 1, dff), f32), jax.ShapeDtypeStruct((n_p, CONV_W - 1, dff), f32),
                 jax.ShapeDtypeStruct((Ms, dff), bf16),
                 jax.ShapeDtypeStruct((Ms, dff), f32), jax.ShapeDtypeStruct((Ms, dff), f32)]
    cb3 = cb.reshape(cb.shape[0], 1, 2 * dff)
    g_p, ta, tg, g_s, ua, ug = pl.pallas_call(
        functools.partial(_ffn_up_kernel, T=T, Ts=Ts, tm=tm, tn=tn),
        grid=(nb, n_i + 1),
        in_specs=in_specs,
        out_specs=out_specs,
        out_shape=out_shape,
        scratch_shapes=[pltpu.VMEM((K, tn), bf16), pltpu.VMEM((K, tn), bf16), pltpu.VMEM((2, CONV_W - 1, tn), f32)],
        compiler_params=_cparams(("parallel", "arbitrary")),
        name="ffn_up",
    )(xp, xs, w, w, prev_p, prev_p, ex[0], ex[1], ex[0], ex[1], cw, cw, cb3, cb3)
    state_p = jnp.concatenate([ta, tg], axis=-1)
    state_s = jnp.concatenate([ua.reshape(n_s, Ts, dff)[:, Ts - (CONV_W - 1):],
                               ug.reshape(n_s, Ts, dff)[:, Ts - (CONV_W - 1):]], axis=-1)
    return g_p, state_p, g_s, state_s


class _Stream:
    def __init__(self, x, pos0, past):
        self.n_seq, self.T, d = x.shape
        self.h = x.reshape(self.n_seq * self.T, d)
        self.pos0 = pos0
        self.past = past
        pos = jnp.tile(pos0 + jnp.arange(self.T, dtype=jnp.int32), self.n_seq)
        self.tables = rope_tables(pos)


def kernel(x_prompt, x_sample, cache_gla_state, cache_dsa_k, cache_dsa_v, cache_idx_k, cache_diff_k, cache_diff_v, state_ffn_conv, norm_mix, norm_ffn, norm_final, w_in_ab, w_gla_a2, b_gla_a, gla_norm, w_out_ab, w_in_c, lambda_q1, lambda_k1, lambda_q2, lambda_k2, diff_norm, w_out_c, w_up, conv_w, conv_b, w_down):
    B, S, D = x_prompt.shape
    DB, DS, _ = x_sample.shape
    depth = norm_mix.shape[0]
    n_ab, _, gh, gdk, gdv = cache_gla_state.shape
    n_c = cache_diff_k.shape[0]
    past = cache_dsa_k.shape[2]
    kvh = cache_dsa_k.shape[3]
    kvw = kvh * DSA_HD
    dh = w_out_ab.shape[1] - gh * gdv
    dsa_heads = dh // DSA_HD
    idx_heads = (w_in_ab.shape[2] - (2 * gh * gdk + 2 * gh * gdv + GLA_RANK + dh + 2 * kvw + IDX_HD)) // (IDX_HD + 1)
    diff_heads = cache_diff_k.shape[3]
    dff = w_down.shape[1]
    cw_ = diff_heads * 2 * DIFF_HD

    streams = (_Stream(x_prompt, 0, 0), _Stream(x_sample, past, past))
    names = ("gla", "dk", "dv", "ik", "ck", "cv", "conv")
    outs = [{k: [] for k in names} for _ in streams]

    c_gla = cache_gla_state.reshape(n_ab * DB, gh, gdk, gdv)
    c_dk = cache_dsa_k.reshape(n_ab * DB, past * kvh, DSA_HD)
    c_dv = cache_dsa_v.reshape(n_ab * DB, past * kvh, DSA_HD)
    c_ik = cache_idx_k.reshape(n_ab * DB, past, IDX_HD)
    c_ck = cache_diff_k.reshape(n_c * DB, past, 2 * diff_heads, DIFF_HD)
    c_cv = cache_diff_v.reshape(n_c * DB, past, diff_heads, 2 * DIFF_HD)
    c_conv = state_ffn_conv.reshape(depth * DB, CONV_W - 1, 2 * dff)
    zero_state = jnp.zeros((B, gh, gdk, gdv), f32)
    zero_conv = jnp.zeros((B, CONV_W - 1, 2 * dff), f32)
    wd_all = w_down.astype(bf16)

    for l in range(depth):
        i = l // 2
        if l % 2 == 0:
            sizes = (gh * gdk, gh * gdk, gh * gdv, gh * gdv, GLA_RANK, dh, kvw, kvw, idx_heads * IDX_HD, IDX_HD,
                     idx_heads)
            offs = np.concatenate([[0], np.cumsum(sizes)]).tolist()
            tn_ab = kvw
            n_body = offs[10] - GLA_RANK
            n_out = -(-(n_body + LANE) // tn_ab) * tn_ab
            w_perm = prep_w_ab(w_in_ab, i, c_ga=offs[4], c_end_ik=offs[10], n_out=n_out)
            ab_segs = [(offs[4], False, (f32,)), (dh, True, (bf16,)), (kvw, True, (f32, bf16)),
                       (kvw, False, (f32, bf16)), (idx_heads * IDX_HD, True, (bf16,)),
                       (tn_ab, [(IDX_HD, True, (f32, bf16)), (LANE, False, (f32,))])]
            c_gq = 0
            c_gk = c_gq + gh * gdk
            c_gv = c_gk + gh * gdk
            c_gg = c_gv + gh * gdv
            wa2 = w_gla_a2[i].astype(bf16)
            ba = b_gla_a[i].reshape(1, gh * gdk)
            gn = gla_norm[i].reshape(1, gdv)
            w_out = w_out_ab[i].astype(bf16)
            for si, st in enumerate(streams):
                hn = rmsnorm(st.h, norm_mix[l], bf16)
                zg, dq_b, dk_f, dk_b, dv_f, dv_b, iq_b, ik_f, ik_b, misc = proj(hn, w_perm, st.tables, ab_segs, tn_ab,
                                                                                 "in_ab")
                if st.past:
                    s0, sb0 = c_gla, i * DB
                else:
                    s0, sb0 = zero_state, 0
                go, gst = gla(zg, misc, s0, sb0, wa2, ba, gn, n_seq=st.n_seq, T=st.T, heads=gh, dk=gdk, dv=gdv,
                              cols=(c_gq, c_gk, c_gv, c_gg))
                if st.past:
                    tq, kb, caches = st.T, DSA_KB_SAMPLE, (c_dk, c_dv, c_ik, i * DB)
                else:
                    tq, kb, caches = _pick(st.T, (DSA_TQ,)), _pick(st.T, (DSA_KB, 512, 256, 128)), None
                do = dsa(dq_b, iq_b, misc, dk_b, dv_b, ik_b, caches, n_seq=st.n_seq, T=st.T, pos0=st.pos0,
                         heads=dsa_heads, kv_heads=kvh, idx_heads=idx_heads, tq=tq, kb=kb)
                st.h = matmul((go, do), w_out, f32, res=st.h, name="out_ab")
                o = outs[si]
                o["gla"].append(gst)
                o["dk"].append(dk_f.reshape(st.n_seq, st.T, kvh, DSA_HD))
                o["dv"].append(dv_f.reshape(st.n_seq, st.T, kvh, DSA_HD))
                o["ik"].append(ik_f.reshape(st.n_seq, st.T, IDX_HD))
        else:
            lam_init = 0.8 - 0.6 * math.exp(-0.3 * l)
            w_in = w_in_c[i].astype(bf16)
            w_out = w_out_c[i].astype(bf16)
            lams = tuple(a[i].reshape(1, DIFF_HD) for a in (lambda_q1, lambda_k1, lambda_q2, lambda_k2))
            gn = diff_norm[i].reshape(1, 2 * DIFF_HD)
            c_segs = [(cw_, True, (bf16,)), (cw_, True, (f32, bf16)), (cw_, False, (f32, bf16))]
            for si, st in enumerate(streams):
                hn = rmsnorm(st.h, norm_mix[l], bf16)
                q_b, k_f, k_b, v_f, v_b = proj(hn, w_in, st.tables, c_segs, min(PROJ_TN, cw_), "in_c")
                if st.past:
                    at = diff_sample(q_b, k_b, v_b, c_ck, c_cv, i * DB, lams, gn, n_seq=st.n_seq, T=st.T,
                                     heads=diff_heads, lam_init=lam_init)
                else:
                    at = diff_prompt(q_b, k_b, v_b, lams, gn, n_seq=st.n_seq, T=st.T, heads=diff_heads,
                                     lam_init=lam_init)
                st.h = matmul(at, w_out, f32, res=st.h, name="out_c")
                o = outs[si]
                o["ck"].append(k_f.reshape(st.n_seq, st.T, diff_heads, 2, DIFF_HD))
                o["cv"].append(v_f.reshape(st.n_seq, st.T, diff_heads, 2 * DIFF_HD))

        sp, ss = streams
        hn_p = rmsnorm(sp.h, norm_ffn[l], bf16)
        hn_s = rmsnorm(ss.h, norm_ffn[l], bf16)
        g_p, conv_p, g_s, conv_s = ffn_up(hn_p, hn_s, w_up, zero_conv, c_conv[l * DB:(l + 1) * DB], conv_w, conv_b, l,
                                          n_p=sp.n_seq, T=sp.T, n_s=ss.n_seq, Ts=ss.T, dff=dff)
        for si, (st, g, conv_state) in enumerate(((sp, g_p, conv_p), (ss, g_s, conv_s))):
            st.h = matmul(g, wd_all, f32, res=st.h, name="ffn_down", layer=l)
            outs[si]["conv"].append(conv_state)

    ys = [rmsnorm(st.h, norm_final, f32).reshape(st.n_seq, st.T, D) for st in streams]
    tail = [jnp.stack(o[k]) for o in outs for k in names]
    return (ys[0], ys[1], *tail)
```
